```python
import math
import jax
import jax.numpy as jnp
from jax import lax
import numpy as np

D_MODEL = 2048
BATCH = 1
SEQ = 16384
DEPTH = 2

GRID_W = 64
CTX_LEN = 256
Q_BLOCK = 128
ROPE_THETA = 10000.0
EPS = 1e-6
M_INIT = -1e30
N_EVEN = (DEPTH + 1) // 2
N_ODD = DEPTH // 2

A_HEAD_DIM = 128
A_HEADS = D_MODEL // (2 * A_HEAD_DIM)
A_KV_HEADS = A_HEADS // 4
A_GROUP = A_HEADS // A_KV_HEADS
B_QK_DIM = 128
B_V_DIM = 256
B_HEADS = D_MODEL // (2 * B_V_DIM)
MLSTM_CHUNK = 64
C_HEAD_DIM = A_HEAD_DIM
C_V_DIM = 2 * C_HEAD_DIM
C_HEADS = D_MODEL // C_V_DIM
FF_DENSE = ((8 * D_MODEL) // 3 + 255) // 256 * 256
N_EXPERTS = 8
TOP_K = 2
FF_EXPERT = 7 * D_MODEL // 2

A_Q = A_HEADS * A_HEAD_DIM
A_KV = A_KV_HEADS * A_HEAD_DIM
B_QK = B_HEADS * B_QK_DIM
B_V = B_HEADS * B_V_DIM
N_GATES = 4 * B_HEADS
EVEN_SIZES = (A_Q, A_KV, A_KV, B_QK, B_QK, B_V, B_V, N_GATES)
EVEN_IN = A_Q + 2 * A_KV + 2 * B_QK + 2 * B_V + N_GATES
EVEN_MIX = A_Q + B_V
C_QK = C_HEADS * 2 * C_HEAD_DIM
ODD_IN = 2 * C_QK + C_HEADS * C_V_DIM
ODD_MIX = C_HEADS * C_V_DIM
F32 = jnp.float32

kernel_name = 'hybrid_dit_gqa_mlstm_diffattn_moe'


def rms_norm(x, gain=None):
    xf = x.astype(F32)
    y = xf * lax.rsqrt(jnp.mean(xf * xf, axis=-1, keepdims=True) + EPS)
    if gain is not None:
        y = y * gain
    return y.astype(x.dtype)


def split_cols(a, sizes):
    offs, acc = [], 0
    for s in sizes[:-1]:
        acc += s
        offs.append(acc)
    return jnp.split(a, offs, axis=-1)


def axial_rope(rows, head_dim):
    row = jnp.broadcast_to(jnp.arange(rows)[:, None], (rows, GRID_W)).reshape(-1).astype(F32)
    col = jnp.broadcast_to(jnp.arange(GRID_W)[None, :], (rows, GRID_W)).reshape(-1).astype(F32)
    n_freq = head_dim // 4
    inv = ROPE_THETA ** (-jnp.arange(n_freq, dtype=F32) / n_freq)
    ang = jnp.concatenate([row[:, None] * inv, col[:, None] * inv], axis=-1)
    return jnp.cos(ang), jnp.sin(ang)


def apply_rope(x, cos, sin):
    xf = x.astype(F32).reshape(x.shape[:-1] + (x.shape[-1] // 2, 2))
    extra = (1,) * (x.ndim - 3)
    cos = cos.reshape(cos.shape[:1] + extra + cos.shape[1:])
    sin = sin.reshape(sin.shape[:1] + extra + sin.shape[1:])
    x0, x1 = xf[..., 0], xf[..., 1]
    out = jnp.stack([x0 * cos - x1 * sin, x0 * sin + x1 * cos], axis=-1)
    return out.reshape(x.shape).astype(x.dtype)


def sweep_query_blocks(fn, q):
    bsz, s = q.shape[:2]
    nb = s // Q_BLOCK
    qb = jnp.moveaxis(q.reshape((bsz, nb, Q_BLOCK) + q.shape[2:]), 1, 0)
    out = lax.map(fn, qb)
    return jnp.moveaxis(out, 0, 1).reshape((bsz, s) + out.shape[3:])


def gqa_attend(q, k, v):
    s = jnp.einsum('bqngd,bknd->bngqk', q, k).astype(F32) * (q.shape[-1] ** -0.5)
    p = jax.nn.softmax(s, axis=-1).astype(v.dtype)
    return jnp.einsum('bngqk,bknd->bqngd', p, v)


def diff_attend(q, k, v, lam):
    s = jnp.einsum('bqhrd,bkhrd->bhrqk', q, k).astype(F32) * (q.shape[-1] ** -0.5)
    p = jax.nn.softmax(s, axis=-1)
    a = (p[:, :, 0] - lam * p[:, :, 1]).astype(v.dtype)
    return jnp.einsum('bhqk,bkhe->bqhe', a, v)


def mlstm_chunkwise(q, k, v, ig, lf, state):
    bsz, nh, t, _ = q.shape
    dv = v.shape[-1]
    nc = t // MLSTM_CHUNK

    def chunks(a):
        a = a.astype(F32)
        return jnp.moveaxis(a.reshape(a.shape[:2] + (nc, MLSTM_CHUNK) + a.shape[3:]), 2, 0)

    xs = (chunks(q), chunks(k), chunks(v), chunks(ig), chunks(lf))
    lower = jnp.tril(jnp.ones((MLSTM_CHUNK, MLSTM_CHUNK), dtype=bool))

    def step(carry, inp):
        c_mem, n_mem, m = carry
        qc, kc, vc, ic, fc = inp
        bcum = jnp.cumsum(fc, axis=-1)
        dlog = jnp.where(lower, bcum[..., :, None] - bcum[..., None, :] + ic[..., None, :], -jnp.inf)
        inter = bcum + m[..., None]
        m_out = jnp.maximum(inter, jnp.max(dlog, axis=-1))
        w_intra = jnp.exp(dlog - m_out[..., None])
        w_inter = jnp.exp(inter - m_out)
        qk = jnp.einsum('bhjd,bhsd->bhjs', qc, kc) * w_intra
        num = w_inter[..., None] * jnp.einsum('bhjd,bhde->bhje', qc, c_mem) + jnp.einsum('bhjs,bhse->bhje', qk, vc)
        den = w_inter * jnp.einsum('bhjd,bhd->bhj', qc, n_mem) + jnp.sum(qk, axis=-1)
        h = num / jnp.maximum(jnp.abs(den), jnp.exp(-m_out))[..., None]
        b_last = bcum[..., -1]
        g = b_last[..., None] - bcum + ic
        m_new = jnp.maximum(b_last + m, jnp.max(g, axis=-1))
        a_dec = jnp.exp(b_last + m - m_new)
        w_in = jnp.exp(g - m_new[..., None])
        c_new = a_dec[..., None, None] * c_mem + jnp.einsum('bhs,bhsd,bhse->bhde', w_in, kc, vc)
        n_new = a_dec[..., None] * n_mem + jnp.einsum('bhs,bhsd->bhd', w_in, kc)
        return (c_new, n_new, m_new), h

    state, hs = lax.scan(step, state, xs)
    return state, jnp.moveaxis(hs, 0, 2).reshape(bsz, nh, t, dv)


def mlstm_direction(lat, ctx, reverse):
    if reverse:
        lat = tuple(jnp.flip(a, axis=2) for a in lat)
        ctx = tuple(jnp.flip(a, axis=2) for a in ctx)
    bsz, nh, _, dk = ctx[0].shape
    dv = ctx[2].shape[-1]
    state0 = (jnp.zeros((bsz, nh, dk, dv), F32), jnp.zeros((bsz, nh, dk), F32), jnp.full((bsz, nh), M_INIT, F32))
    state_ctx, h_c = mlstm_chunkwise(*ctx, state0)
    _, h_l = mlstm_chunkwise(*lat, state_ctx)
    if reverse:
        h_l, h_c = jnp.flip(h_l, axis=2), jnp.flip(h_c, axis=2)
    return h_l, h_c


def even_mixer(h_lat, h_ctx, w_in, gate_b, q_gain, k_gain, h_gain, w_out, cos, sin):
    def project(h, rope):
        bsz, t = h.shape[:2]
        qa, ka, va, qb, kb, vb, ob, gates = split_cols(h @ w_in, EVEN_SIZES)
        qa = rms_norm(qa.reshape(bsz, t, A_KV_HEADS, A_GROUP, A_HEAD_DIM), q_gain)
        ka = rms_norm(ka.reshape(bsz, t, A_KV_HEADS, A_HEAD_DIM), k_gain)
        va = va.reshape(bsz, t, A_KV_HEADS, A_HEAD_DIM)
        if rope:
            qa, ka = apply_rope(qa, cos, sin), apply_rope(ka, cos, sin)

        def heads(a, d):
            return jnp.swapaxes(a.reshape(bsz, t, B_HEADS, d), 1, 2)

        qb = heads(qb, B_QK_DIM) * (B_QK_DIM ** -0.5)
        kb, vb = heads(kb, B_QK_DIM), heads(vb, B_V_DIM)
        g = jnp.moveaxis((gates.astype(F32) + gate_b).reshape(bsz, t, 4, B_HEADS), 1, 3)
        fwd = (qb, kb, vb, g[:, 0], jax.nn.log_sigmoid(g[:, 1]))
        bwd = (qb, kb, vb, g[:, 2], jax.nn.log_sigmoid(g[:, 3]))
        return (qa, ka, va), fwd, bwd, ob.reshape(bsz, t, B_HEADS, B_V_DIM)

    att_l, fwd_l, bwd_l, o_l = project(h_lat, True)
    att_c, fwd_c, bwd_c, o_c = project(h_ctx, False)
    k_all = jnp.concatenate([att_l[1], att_c[1]], axis=1)
    v_all = jnp.concatenate([att_l[2], att_c[2]], axis=1)
    a_l = sweep_query_blocks(lambda qb: gqa_attend(qb, k_all, v_all), att_l[0])
    a_c = gqa_attend(*att_c)
    hf_l, hf_c = mlstm_direction(fwd_l, fwd_c, False)
    hb_l, hb_c = mlstm_direction(bwd_l, bwd_c, True)

    def merge(a, hf, hb, o):
        bsz, t = a.shape[:2]
        hm = rms_norm(jnp.swapaxes(hf + hb, 1, 2), h_gain) * jax.nn.sigmoid(o.astype(F32))
        mix = jnp.concatenate([a.reshape(bsz, t, A_Q), hm.reshape(bsz, t, B_V).astype(a.dtype)], axis=-1)
        return mix @ w_out

    return merge(a_l, hf_l, hb_l, o_l), merge(a_c, hf_c, hb_c, o_c)


def odd_mixer(h_lat, h_ctx, w_in, q_gain, k_gain, lam_q1, lam_k1, lam_q2, lam_k2, sub_gain, w_out,
              cos, sin, layer, update_ctx):
    lam_init = 0.8 - 0.6 * math.exp(-0.3 * layer)
    lam = (jnp.exp(jnp.sum(lam_q1 * lam_k1).astype(F32)) - jnp.exp(jnp.sum(lam_q2 * lam_k2).astype(F32))
           + lam_init)

    def qk_heads(a, gain):
        return rms_norm(a.reshape(a.shape[:2] + (C_HEADS, 2, C_HEAD_DIM)), gain)

    def v_heads(a):
        return a.reshape(a.shape[:2] + (C_HEADS, C_V_DIM))

    q_l, k_l, v_l = split_cols(h_lat @ w_in, (C_QK, C_QK, C_HEADS * C_V_DIM))
    q_l = apply_rope(qk_heads(q_l, q_gain), cos, sin)
    k_l = apply_rope(qk_heads(k_l, k_gain), cos, sin)
    v_l = v_heads(v_l)
    if update_ctx:
        q_c, k_c, v_c = split_cols(h_ctx @ w_in, (C_QK, C_QK, C_HEADS * C_V_DIM))
        q_c = qk_heads(q_c, q_gain)
    else:
        k_c, v_c = split_cols(h_ctx @ w_in[:, C_QK:], (C_QK, C_HEADS * C_V_DIM))
    k_c, v_c = qk_heads(k_c, k_gain), v_heads(v_c)
    k_all = jnp.concatenate([k_l, k_c], axis=1)
    v_all = jnp.concatenate([v_l, v_c], axis=1)

    def finish(o):
        o = rms_norm(o, sub_gain) * (1.0 - lam_init)
        return o.reshape(o.shape[:2] + (ODD_MIX,)) @ w_out

    y_l = finish(sweep_query_blocks(lambda qb: diff_attend(qb, k_all, v_all, lam), q_l))
    y_c = finish(diff_attend(q_c, k_c, v_c, lam)) if update_ctx else None
    return y_l, y_c


def swiglu(h, w_gate, w_up, w_down):
    return (jax.nn.silu(h @ w_gate) * (h @ w_up)) @ w_down


def moe_swiglu(h, router, w_gate, w_up, w_down):
    logits = (h @ router).astype(F32)
    top_v, top_i = lax.top_k(logits, TOP_K)
    top_w = jax.nn.softmax(top_v, axis=-1)
    gate = jnp.sum(jax.nn.one_hot(top_i, N_EXPERTS, dtype=F32) * top_w[..., None], axis=-2).astype(h.dtype)
    y = jnp.zeros(h.shape, h.dtype)
    for e in range(N_EXPERTS):
        y = y + gate[..., e:e + 1] * swiglu(h, w_gate[e], w_up[e], w_down[e])
    return y


def modulation(cvec, w, b):
    return jnp.split(jax.nn.silu(cvec) @ w + b, 6, axis=-1)


def setup_inputs(seed: int = 0) -> dict:
    key = jax.random.key(seed)
    ks = jax.random.split(key, 32)
    D = D_MODEL

    def nrm(k, shape, s):
        return jax.random.normal(k, shape, F32) * s

    gate_base = jnp.array([0.0, 3.0, 0.0, 3.0], F32)[None, :, None]
    return {
        'x': nrm(ks[0], (BATCH, SEQ, D), 1.0),
        'c': nrm(ks[1], (BATCH, D), 1.0),
        'ctx': nrm(ks[2], (BATCH, CTX_LEN, D), 1.0),
        'c_ctx': nrm(ks[3], (D,), 1.0),
        'ada_w': nrm(ks[4], (DEPTH, D, 6 * D), 0.5 * D ** -0.5),
        'ada_b': nrm(ks[5], (DEPTH, 6 * D), 0.01),
        'ev_w_in': nrm(ks[6], (N_EVEN, D, EVEN_IN), D ** -0.5),
        'ev_gate_b': (nrm(ks[7], (N_EVEN, 4, B_HEADS), 0.1) + gate_base).reshape(N_EVEN, N_GATES),
        'ev_q_gain': 1.0 + nrm(ks[8], (N_EVEN, A_HEAD_DIM), 0.02),
        'ev_k_gain': 1.0 + nrm(ks[9], (N_EVEN, A_HEAD_DIM), 0.02),
        'ev_h_gain': 1.0 + nrm(ks[10], (N_EVEN, B_HEADS, B_V_DIM), 0.02),
        'ev_w_out': nrm(ks[11], (N_EVEN, EVEN_MIX, D), EVEN_MIX ** -0.5),
        'ev_ff_gate': nrm(ks[12], (N_EVEN, D, FF_DENSE), D ** -0.5),
        'ev_ff_up': nrm(ks[13], (N_EVEN, D, FF_DENSE), D ** -0.5),
        'ev_ff_down': nrm(ks[14], (N_EVEN, FF_DENSE, D), FF_DENSE ** -0.5),
        'od_w_in': nrm(ks[15], (N_ODD, D, ODD_IN), D ** -0.5),
        'od_q_gain': 1.0 + nrm(ks[16], (N_ODD, C_HEAD_DIM), 0.02),
        'od_k_gain': 1.0 + nrm(ks[17], (N_ODD, C_HEAD_DIM), 0.02),
        'od_lam_q1': nrm(ks[18], (N_ODD, C_HEAD_DIM), 0.1),
        'od_lam_k1': nrm(ks[19], (N_ODD, C_HEAD_DIM), 0.1),
        'od_lam_q2': nrm(ks[20], (N_ODD, C_HEAD_DIM), 0.1),
        'od_lam_k2': nrm(ks[21], (N_ODD, C_HEAD_DIM), 0.1),
        'od_sub_gain': 1.0 + nrm(ks[22], (N_ODD, C_V_DIM), 0.02),
        'od_w_out': nrm(ks[23], (N_ODD, ODD_MIX, D), ODD_MIX ** -0.5),
        'od_router': nrm(ks[24], (N_ODD, D, N_EXPERTS), D ** -0.5),
        'od_ex_gate': nrm(ks[25], (N_ODD, N_EXPERTS, D, FF_EXPERT), D ** -0.5),
        'od_ex_up': nrm(ks[26], (N_ODD, N_EXPERTS, D, FF_EXPERT), D ** -0.5),
        'od_ex_down': nrm(ks[27], (N_ODD, N_EXPERTS, FF_EXPERT, D), FF_EXPERT ** -0.5),
    }


def reference(x, c, ctx, c_ctx, ada_w, ada_b,
              ev_w_in, ev_gate_b, ev_q_gain, ev_k_gain, ev_h_gain, ev_w_out,
              ev_ff_gate, ev_ff_up, ev_ff_down,
              od_w_in, od_q_gain, od_k_gain, od_lam_q1, od_lam_k1, od_lam_q2, od_lam_k2,
              od_sub_gain, od_w_out, od_router, od_ex_gate, od_ex_up, od_ex_down):
    rows = x.shape[1] // GRID_W
    cos, sin = axial_rope(rows, A_HEAD_DIM)
    x_lat, x_ctx = x, ctx
    for layer in range(DEPTH):
        update_ctx = layer < DEPTH - 1
        i = layer // 2
        m_lat = [m[:, None, :] for m in modulation(c, ada_w[layer], ada_b[layer])]
        m_ctx = modulation(c_ctx, ada_w[layer], ada_b[layer])
        h_lat = rms_norm(x_lat) * (1.0 + m_lat[1]) + m_lat[0]
        h_ctx = rms_norm(x_ctx) * (1.0 + m_ctx[1]) + m_ctx[0]
        if layer % 2 == 0:
            y_lat, y_ctx = even_mixer(h_lat, h_ctx, ev_w_in[i], ev_gate_b[i], ev_q_gain[i], ev_k_gain[i],
                                      ev_h_gain[i], ev_w_out[i], cos, sin)
            ffn = lambda h, i=i: swiglu(h, ev_ff_gate[i], ev_ff_up[i], ev_ff_down[i])
        else:
            y_lat, y_ctx = odd_mixer(h_lat, h_ctx, od_w_in[i], od_q_gain[i], od_k_gain[i], od_lam_q1[i],
                                     od_lam_k1[i], od_lam_q2[i], od_lam_k2[i], od_sub_gain[i], od_w_out[i],
                                     cos, sin, layer, update_ctx)
            ffn = lambda h, i=i: moe_swiglu(h, od_router[i], od_ex_gate[i], od_ex_up[i], od_ex_down[i])
        x_lat = x_lat + m_lat[2] * y_lat
        x_lat = x_lat + m_lat[5] * ffn(rms_norm(x_lat) * (1.0 + m_lat[4]) + m_lat[3])
        if update_ctx:
            x_ctx = x_ctx + m_ctx[2] * y_ctx
            x_ctx = x_ctx + m_ctx[5] * ffn(rms_norm(x_ctx) * (1.0 + m_ctx[4]) + m_ctx[3])
    return x_lat
```

```python
import functools
import math

import jax
import jax.numpy as jnp
from jax import lax
from jax.experimental import pallas as pl
from jax.experimental.pallas import tpu as pltpu

F32 = jnp.float32
BF16 = jnp.bfloat16

GRID_W = 64
ROPE_THETA = 10000.0
EPS = 1e-6
M_INIT = -1e30
HEAD = 128
V_HEAD = 256
GQA_GROUP = 4
TOP_K = 2
MLSTM_CHUNK = 256
LANES = 128
VMEM_LIMIT = 56 * 1024 * 1024
NEG_INF = float("-inf")


def _cparams(sem):
    return pltpu.CompilerParams(dimension_semantics=sem, vmem_limit_bytes=VMEM_LIMIT)


def _dot(a, b):
    return jnp.dot(a, b, preferred_element_type=F32)


def _dot_nt(a, b):
    return lax.dot_general(a, b, (((1,), (1,)), ((), ())), preferred_element_type=F32)


def _dot_tn(a, b):
    return lax.dot_general(a, b, (((0,), (0,)), ((), ())), preferred_element_type=F32)


def _rms(x):
    return x * lax.rsqrt(jnp.mean(x * x, axis=-1, keepdims=True) + EPS)


def _mod_kernel(c_ref, w_ref, b_ref, o_ref):
    c = c_ref[...]
    s = c * jax.nn.sigmoid(c)
    o_ref[...] = _dot(s.astype(BF16), w_ref[...].astype(BF16)) + b_ref[...]


def _modulation(cvecs, ada_w, ada_b):
    depth, d, n = ada_w.shape
    tn = 1024
    return pl.pallas_call(
        _mod_kernel,
        grid=(depth, n // tn),
        in_specs=[
            pl.BlockSpec((8, d), lambda l, j: (0, 0)),
            pl.BlockSpec((None, d, tn), lambda l, j: (l, 0, j)),
            pl.BlockSpec((None, 1, tn), lambda l, j: (l, 0, j)),
        ],
        out_specs=pl.BlockSpec((None, 8, tn), lambda l, j: (l, 0, j)),
        out_shape=jax.ShapeDtypeStruct((depth, 8, n), F32),
        compiler_params=_cparams(("parallel", "parallel")),
        name="adaln_mod",
    )(cvecs, ada_w, ada_b.reshape(depth, 1, n))


def _proj_kernel(patterns, n_bf_tiles, has_f32, has_gates, tn, *refs):
    it = iter(refs)
    x_ref, sh_ref, sc_ref, w_ref, cs_ref, cos_ref, sin_ref = (next(it) for _ in range(7))
    if has_gates:
        wg_ref, gb_ref = next(it), next(it)
    ob_ref = next(it)
    of_ref = next(it) if has_f32 else None
    og_ref = next(it) if has_gates else None
    h_scr = next(it)

    j = pl.program_id(1)

    @pl.when(j == 0)
    def _():
        h = _rms(x_ref[...]) * (1.0 + sc_ref[...]) + sh_ref[...]
        h_scr[...] = h.astype(BF16)
        if has_gates:
            og_ref[...] = _dot(h_scr[...], wg_ref[...]) + gb_ref[...]

    acc = _dot(h_scr[...], w_ref[...])
    cs = cs_ref[...]

    for lo, hi, pat in patterns:
        @pl.when(jnp.logical_and(j >= lo, j < hi))
        def _(pat=pat, lo=lo):
            to_f32 = has_f32 and lo >= n_bf_tiles
            out_ref = of_ref if to_f32 else ob_ref
            for hidx, mode in enumerate(pat):
                sl = slice(hidx * HEAD, (hidx + 1) * HEAD)
                y = acc[:, sl]
                if mode == "R":
                    y = _rms(y) * cs[:, sl]
                    y = y * cos_ref[...] + pltpu.roll(y, HEAD // 2, 1) * sin_ref[...]
                else:
                    y = y * cs[:, sl]
                out_ref[:, sl] = y.astype(out_ref.dtype)


def _proj(x, shift, scale, w, colscale, cos2, sin2, head_modes, n_bf_cols, wg=None, gb=None, tm=1024, tn=512):
    m, d = x.shape
    n = w.shape[1]
    tm = min(tm, m)
    nj = n // tn
    n_bf_tiles = n_bf_cols // tn
    has_f32 = n_bf_cols < n
    has_gates = wg is not None
    hpt = tn // HEAD
    tiles = [tuple(head_modes[t * hpt:(t + 1) * hpt]) for t in range(nj)]
    patterns = []
    for t, pat in enumerate(tiles):
        if patterns and patterns[-1][2] == pat and not (has_f32 and t == n_bf_tiles):
            patterns[-1] = (patterns[-1][0], t + 1, pat)
        else:
            patterns.append((t, t + 1, pat))

    in_specs = [
        pl.BlockSpec((tm, d), lambda i, j: (i, 0)),
        pl.BlockSpec((1, d), lambda i, j: (0, 0)),
        pl.BlockSpec((1, d), lambda i, j: (0, 0)),
        pl.BlockSpec((d, tn), lambda i, j: (0, j)),
        pl.BlockSpec((1, tn), lambda i, j: (0, j)),
        pl.BlockSpec((tm, HEAD), lambda i, j: (i, 0)),
        pl.BlockSpec((tm, HEAD), lambda i, j: (i, 0)),
    ]
    args = [x, shift, scale, w, colscale, cos2, sin2]
    if has_gates:
        in_specs += [pl.BlockSpec((d, LANES), lambda i, j: (0, 0)), pl.BlockSpec((1, LANES), lambda i, j: (0, 0))]
        args += [wg, gb]
    last_bf = n_bf_tiles - 1
    out_specs = [pl.BlockSpec((tm, tn), lambda i, j: (i, jnp.minimum(j, last_bf)))]
    out_shape = [jax.ShapeDtypeStruct((m, n_bf_cols), BF16)]
    if has_f32:
        out_specs.append(pl.BlockSpec((tm, tn), lambda i, j: (i, jnp.maximum(j - n_bf_tiles, 0))))
        out_shape.append(jax.ShapeDtypeStruct((m, n - n_bf_cols), F32))
    if has_gates:
        out_specs.append(pl.BlockSpec((tm, LANES), lambda i, j: (i, 0)))
        out_shape.append(jax.ShapeDtypeStruct((m, LANES), F32))
    return pl.pallas_call(
        functools.partial(_proj_kernel, tuple(patterns), n_bf_tiles, has_f32, has_gates, tn),
        grid=(m // tm, nj),
        in_specs=in_specs,
        out_specs=out_specs,
        out_shape=out_shape,
        scratch_shapes=[pltpu.VMEM((tm, d), BF16)],
        compiler_params=_cparams(("parallel", "arbitrary")),
        name="norm_mod_proj",
    )(*args)


def _gqa_kernel(tq, q_ref, k_ref, v_ref, o_ref, qs_scr, m_scr, l_scr, acc_scr):
    kj = pl.program_id(2)

    @pl.when(kj == 0)
    def _():
        for g in range(GQA_GROUP):
            qs_scr[g * tq:(g + 1) * tq, :] = q_ref[:, g * HEAD:(g + 1) * HEAD]
        m_scr[...] = jnp.full(m_scr.shape, NEG_INF, F32)
        l_scr[...] = jnp.zeros(l_scr.shape, F32)
        acc_scr[...] = jnp.zeros(acc_scr.shape, F32)

    s = _dot_nt(qs_scr[...], k_ref[...])
    m_prev = m_scr[...]
    m_new = jnp.maximum(m_prev, jnp.max(s, axis=1, keepdims=True))
    alpha = jnp.exp(m_prev - m_new)
    p = jnp.exp(s - m_new)
    l_scr[...] = alpha * l_scr[...] + jnp.sum(p, axis=1, keepdims=True)
    acc_scr[...] = alpha * acc_scr[...] + _dot(p.astype(BF16), v_ref[...])
    m_scr[...] = m_new

    @pl.when(kj == pl.num_programs(2) - 1)
    def _():
        out = acc_scr[...] / l_scr[...]
        for g in range(GQA_GROUP):
            o_ref[:, g * HEAD:(g + 1) * HEAD] = out[g * tq:(g + 1) * tq, :].astype(o_ref.dtype)


def _gqa_attention(q_arr, kv_arr, n_kv, k_col, v_col, tq, tk):
    s_len = q_arr.shape[0]
    lk = kv_arr.shape[0]
    tq = min(tq, s_len)
    tk = min(tk, lk)
    gw = GQA_GROUP * HEAD
    return pl.pallas_call(
        functools.partial(_gqa_kernel, tq),
        grid=(n_kv, s_len // tq, lk // tk),
        in_specs=[
            pl.BlockSpec((tq, gw), lambda n, i, j: (i, n)),
            pl.BlockSpec((tk, HEAD), lambda n, i, j: (j, k_col + n)),
            pl.BlockSpec((tk, HEAD), lambda n, i, j: (j, v_col + n)),
        ],
        out_specs=pl.BlockSpec((tq, gw), lambda n, i, j: (i, n)),
        out_shape=jax.ShapeDtypeStruct((s_len, n_kv * gw), BF16),
        scratch_shapes=[
            pltpu.VMEM((GQA_GROUP * tq, HEAD), BF16),
            pltpu.VMEM((GQA_GROUP * tq, 1), F32),
            pltpu.VMEM((GQA_GROUP * tq, 1), F32),
            pltpu.VMEM((GQA_GROUP * tq, HEAD), F32),
        ],
        compiler_params=_cparams(("parallel", "parallel", "arbitrary")),
        name="gqa_flash",
    )(q_arr, kv_arr, kv_arr)


def _diff_kernel(tq, lam_init, q_ref, k_ref, v_ref, lq1_ref, lk1_ref, lq2_ref, lk2_ref, sg_ref, o_ref,
                 p_scr, m_scr, l_scr, acc_scr):
    kj = pl.program_id(2)

    @pl.when(kj == 0)
    def _():
        m_scr[...] = jnp.full(m_scr.shape, NEG_INF, F32)
        l_scr[...] = jnp.zeros(l_scr.shape, F32)
        acc_scr[...] = jnp.zeros(acc_scr.shape, F32)

    q = q_ref[...]
    k = k_ref[...]
    alphas = []
    for r in range(2):
        rows = slice(r * tq, (r + 1) * tq)
        s = _dot_nt(q[:, r * HEAD:(r + 1) * HEAD], k[:, r * HEAD:(r + 1) * HEAD])
        m_prev = m_scr[rows, :]
        m_new = jnp.maximum(m_prev, jnp.max(s, axis=1, keepdims=True))
        alpha = jnp.exp(m_prev - m_new)
        p = jnp.exp(s - m_new)
        l_scr[rows, :] = alpha * l_scr[rows, :] + jnp.sum(p, axis=1, keepdims=True)
        m_scr[rows, :] = m_new
        p_scr[rows, :] = p.astype(BF16)
        alphas.append(alpha)
    alpha = jnp.concatenate(alphas, axis=0)
    acc_scr[...] = alpha * acc_scr[...] + _dot(p_scr[...], v_ref[...])

    @pl.when(kj == pl.num_programs(2) - 1)
    def _():
        lam = (jnp.exp(jnp.sum(lq1_ref[...] * lk1_ref[...], axis=1, keepdims=True))
               - jnp.exp(jnp.sum(lq2_ref[...] * lk2_ref[...], axis=1, keepdims=True)) + lam_init)
        o = acc_scr[...] / l_scr[...]
        o = o[:tq, :] - lam * o[tq:, :]
        o = _rms(o) * sg_ref[...] * (1.0 - lam_init)
        o_ref[...] = o.astype(o_ref.dtype)


def _diff_attention(q_arr, kv_arr, n_heads, lam_init, lq1, lk1, lq2, lk2, sub_gain, tq, tk):
    s_len = q_arr.shape[0]
    lk = kv_arr.shape[0]
    tq = min(tq, s_len)
    tk = min(tk, lk)
    vec = pl.BlockSpec((1, HEAD), lambda h, i, j: (0, 0))
    return pl.pallas_call(
        functools.partial(_diff_kernel, tq, lam_init),
        grid=(n_heads, s_len // tq, lk // tk),
        in_specs=[
            pl.BlockSpec((tq, V_HEAD), lambda h, i, j: (i, h)),
            pl.BlockSpec((tk, V_HEAD), lambda h, i, j: (j, n_heads + h)),
            pl.BlockSpec((tk, V_HEAD), lambda h, i, j: (j, 2 * n_heads + h)),
            vec, vec, vec, vec,
            pl.BlockSpec((1, V_HEAD), lambda h, i, j: (0, 0)),
        ],
        out_specs=pl.BlockSpec((tq, V_HEAD), lambda h, i, j: (i, h)),
        out_shape=jax.ShapeDtypeStruct((s_len, n_heads * V_HEAD), BF16),
        scratch_shapes=[
            pltpu.VMEM((2 * tq, tk), BF16),
            pltpu.VMEM((2 * tq, 1), F32),
            pltpu.VMEM((2 * tq, 1), F32),
            pltpu.VMEM((2 * tq, V_HEAD), F32),
        ],
        compiler_params=_cparams(("parallel", "parallel", "arbitrary")),
        name="diff_flash",
    )(q_arr, kv_arr, kv_arr, lq1, lk1, lq2, lk2, sub_gain)


def _split3(x):
    hi = x.astype(BF16)
    r1 = x - hi.astype(F32)
    mid = r1.astype(BF16)
    lo = (r1 - mid.astype(F32)).astype(BF16)
    return hi, mid, lo


def _log_sigmoid(x):
    return jnp.minimum(x, 0.0) - jnp.log(1.0 + jnp.exp(-jnp.abs(x)))


def _mlstm_kernel(nh, chunk, qf_ref, kf_ref, vf_ref, gcf_ref, grf_ref, qb_ref, kb_ref, vb_ref, gcb_ref, grb_ref,
                  hf_ref, hb_ref, c_scr, n_scr, m_scr):
    t = pl.program_id(0)

    @pl.when(t == 0)
    def _():
        c_scr[...] = jnp.zeros(c_scr.shape, F32)
        n_scr[...] = jnp.zeros(n_scr.shape, F32)
        m_scr[...] = jnp.full(m_scr.shape, M_INIT, F32)

    row = lax.broadcasted_iota(jnp.int32, (chunk, chunk), 0)
    col = lax.broadcasted_iota(jnp.int32, (chunk, chunk), 1)
    lower = col <= row
    upper = col >= row
    lower_b = lower.astype(BF16)
    upper_b = upper.astype(BF16)

    for rev in (False, True):
        q_ref, k_ref, v_ref, gc_ref, gr_ref, h_ref = (
            (qb_ref, kb_ref, vb_ref, gcb_ref, grb_ref, hb_ref) if rev else
            (qf_ref, kf_ref, vf_ref, gcf_ref, grf_ref, hf_ref))
        mask = upper if rev else lower
        gcol = gc_ref[...]
        grow = gr_ref[...]
        cum_col = sum(_dot(upper_b if rev else lower_b, piece) for piece in _split3(_log_sigmoid(gcol)))
        cum_row = sum(_dot(piece, lower_b if rev else upper_b) for piece in _split3(_log_sigmoid(grow)))
        for hd in range(nh):
            idx = (1 if rev else 0) * nh + hd
            ki = (2 if rev else 0) * nh + hd
            kf = ki + nh
            qc = q_ref[:, hd * HEAD:(hd + 1) * HEAD]
            kc = k_ref[:, hd * HEAD:(hd + 1) * HEAD]
            vc = v_ref[:, hd * V_HEAD:(hd + 1) * V_HEAD]
            i_col = gcol[:, ki:ki + 1]
            i_row = grow[ki:ki + 1, :]
            b_col = cum_col[:, kf:kf + 1]
            b_row = cum_row[kf:kf + 1, :]
            c_mem = c_scr[idx]
            n_mem = n_scr[idx]
            m = m_scr[idx][:, :1]

            dlog = jnp.where(mask, b_col + (i_row - b_row), NEG_INF)
            inter = b_col + m
            m_out = jnp.maximum(inter, jnp.max(dlog, axis=1, keepdims=True))
            w_intra = jnp.exp(dlog - m_out)
            w_inter = jnp.exp(inter - m_out)
            qk = _dot_nt(qc, kc) * w_intra
            num = w_inter * _dot(qc, c_mem.astype(BF16)) + _dot(qk.astype(BF16), vc)
            den = (w_inter * jnp.sum(qc.astype(F32) * n_mem, axis=1, keepdims=True)
                   + jnp.sum(qk, axis=1, keepdims=True))
            h = num / jnp.maximum(jnp.abs(den), jnp.exp(-m_out))
            h_ref[:, hd * V_HEAD:(hd + 1) * V_HEAD] = h

            b_last = b_col[0:1, :] if rev else b_col[chunk - 1:chunk, :]
            g_col = b_last - b_col + i_col
            m_new = jnp.maximum(b_last + m, jnp.max(g_col, axis=0, keepdims=True))
            a_dec = jnp.exp(b_last + m - m_new)
            kw = kc.astype(F32) * jnp.exp(g_col - m_new)
            c_scr[idx] = a_dec * c_mem + _dot_tn(kw.astype(BF16), vc)
            n_scr[idx] = a_dec * n_mem + jnp.sum(kw, axis=0, keepdims=True)
            m_scr[idx] = jnp.broadcast_to(m_new, (1, LANES))


def _mlstm(qkv, gates_col, gates_row, nh, q_col, k_col, v_col, n_lat, n_ctx):
    t_len = qkv.shape[0]
    chunk = MLSTM_CHUNK
    nlat, nctx = n_lat // chunk, n_ctx // chunk
    nc = nlat + nctx

    def fwd(t):
        return jnp.where(t < nctx, nlat + t, t - nctx)

    def bwd(t):
        return jnp.where(t < nctx, nc - 1 - t, nlat - 1 - (t - nctx))

    qw, vw = nh * HEAD, nh * V_HEAD

    def specs(order):
        return [
            pl.BlockSpec((chunk, qw), lambda t: (order(t), q_col)),
            pl.BlockSpec((chunk, qw), lambda t: (order(t), k_col)),
            pl.BlockSpec((chunk, vw), lambda t: (order(t), v_col)),
            pl.BlockSpec((chunk, LANES), lambda t: (order(t), 0)),
            pl.BlockSpec((4 * nh, chunk), lambda t: (0, order(t))),
        ]

    return pl.pallas_call(
        functools.partial(_mlstm_kernel, nh, chunk),
        grid=(nc,),
        in_specs=specs(fwd) + specs(bwd),
        out_specs=[pl.BlockSpec((chunk, vw), lambda t: (fwd(t), 0)),
                   pl.BlockSpec((chunk, vw), lambda t: (bwd(t), 0))],
        out_shape=[jax.ShapeDtypeStruct((t_len, vw), F32)] * 2,
        scratch_shapes=[
            pltpu.VMEM((2 * nh, HEAD, V_HEAD), F32),
            pltpu.VMEM((2 * nh, 1, HEAD), F32),
            pltpu.VMEM((2 * nh, 1, LANES), F32),
        ],
        compiler_params=_cparams(("arbitrary",)),
        name="mlstm_scan",
    )(qkv, qkv, qkv, gates_col, gates_row, qkv, qkv, qkv, gates_col, gates_row)


def _out_kernel(merge, nh, *refs):
    if merge:
        a_ref, hf_ref, hb_ref, o_ref, hg_ref, w_ref, x_ref, g_ref, y_ref, mix_scr = refs
        aw = a_ref.shape[1]

        @pl.when(pl.program_id(1) == 0)
        def _():
            mix_scr[:, :aw] = a_ref[...]
            for hd in range(nh):
                sl = slice(hd * V_HEAD, (hd + 1) * V_HEAD)
                hm = _rms(hf_ref[:, sl] + hb_ref[:, sl]) * hg_ref[:, sl] * jax.nn.sigmoid(o_ref[:, sl])
                mix_scr[:, aw + hd * V_HEAD:aw + (hd + 1) * V_HEAD] = hm.astype(BF16)

        lhs = mix_scr[...]
    else:
        a_ref, w_ref, x_ref, g_ref, y_ref = refs
        lhs = a_ref[...]
    y_ref[...] = x_ref[...] + g_ref[...] * _dot(lhs, w_ref[...])


def _out_proj(x, gate, w, a, merge=None, row_off=0, tm=512, tn=1024):
    m, d = x.shape
    kdim, n = w.shape
    tm = min(tm, m)
    off = row_off // tm
    in_specs = [pl.BlockSpec((tm, a.shape[1]), lambda i, j: (i, 0))]
    args = [a]
    scratch = []
    nh = 0
    if merge is not None:
        hf, hb, o, hg, nh = merge
        hw = hf.shape[1]
        in_specs += [pl.BlockSpec((tm, hw), lambda i, j: (i + off, 0)),
                     pl.BlockSpec((tm, hw), lambda i, j: (i + off, 0)),
                     pl.BlockSpec((tm, hw), lambda i, j: (i, 0)),
                     pl.BlockSpec((1, hw), lambda i, j: (0, 0))]
        args += [hf, hb, o, hg]
        scratch = [pltpu.VMEM((tm, kdim), BF16)]
    in_specs += [pl.BlockSpec((kdim, tn), lambda i, j: (0, j)),
                 pl.BlockSpec((tm, tn), lambda i, j: (i, j)),
                 pl.BlockSpec((1, tn), lambda i, j: (0, j))]
    args += [w, x, gate]
    return pl.pallas_call(
        functools.partial(_out_kernel, merge is not None, nh),
        grid=(m // tm, n // tn),
        in_specs=in_specs,
        out_specs=pl.BlockSpec((tm, tn), lambda i, j: (i, j)),
        out_shape=jax.ShapeDtypeStruct((m, n), F32),
        scratch_shapes=scratch,
        compiler_params=_cparams(("parallel", "arbitrary")),
        name="out_proj_residual",
    )(*args)


def _ffn_kernel(x_ref, sh_ref, sc_ref, g_ref, wg_ref, wu_ref, wd_ref, y_ref, h_scr, acc_scr):
    f = pl.program_id(1)

    @pl.when(f == 0)
    def _():
        h = _rms(x_ref[...]) * (1.0 + sc_ref[...]) + sh_ref[...]
        h_scr[...] = h.astype(BF16)
        acc_scr[...] = jnp.zeros(acc_scr.shape, F32)

    h = h_scr[...]
    gt = _dot(h, wg_ref[...])
    up = _dot(h, wu_ref[...])
    act = (gt * jax.nn.sigmoid(gt)) * up
    acc_scr[...] += _dot(act.astype(BF16), wd_ref[...])

    @pl.when(f == pl.num_programs(1) - 1)
    def _():
        y_ref[...] = x_ref[...] + g_ref[...] * acc_scr[...]


def _ffn_dense(x, shift, scale, gate, wg, wu, wd, tm=512, tf=512):
    m, d = x.shape
    ff = wg.shape[1]
    tm = min(tm, m)
    vec = pl.BlockSpec((1, d), lambda i, f: (0, 0))
    return pl.pallas_call(
        _ffn_kernel,
        grid=(m // tm, ff // tf),
        in_specs=[
            pl.BlockSpec((tm, d), lambda i, f: (i, 0)), vec, vec, vec,
            pl.BlockSpec((d, tf), lambda i, f: (0, f)),
            pl.BlockSpec((d, tf), lambda i, f: (0, f)),
            pl.BlockSpec((tf, d), lambda i, f: (f, 0)),
        ],
        out_specs=pl.BlockSpec((tm, d), lambda i, f: (i, 0)),
        out_shape=jax.ShapeDtypeStruct((m, d), F32),
        scratch_shapes=[pltpu.VMEM((tm, d), BF16), pltpu.VMEM((tm, d), F32)],
        compiler_params=_cparams(("parallel", "arbitrary")),
        name="ffn_dense",
    )(x, shift, scale, gate, wg, wu, wd)


def _moe_kernel(n_exp, x_ref, sh_ref, sc_ref, g_ref, r_ref, wg_ref, wu_ref, wd_ref, y_ref, h_scr, acc_scr, gate_scr):
    e = pl.program_id(1)
    f = pl.program_id(2)
    first = jnp.logical_and(e == 0, f == 0)

    @pl.when(first)
    def _():
        h = _rms(x_ref[...]) * (1.0 + sc_ref[...]) + sh_ref[...]
        h_scr[...] = h.astype(BF16)
        acc_scr[...] = jnp.zeros(acc_scr.shape, F32)
        logits = jnp.dot(h, r_ref[...], preferred_element_type=F32, precision=lax.Precision.HIGHEST)
        lane = lax.broadcasted_iota(jnp.int32, logits.shape, 1)
        logits = jnp.where(lane < n_exp, logits, NEG_INF)
        v1 = jnp.max(logits, axis=1, keepdims=True)
        i1 = jnp.min(jnp.where(logits == v1, lane, LANES), axis=1, keepdims=True)
        rest = jnp.where(lane == i1, NEG_INF, logits)
        v2 = jnp.max(rest, axis=1, keepdims=True)
        i2 = jnp.min(jnp.where(rest == v2, lane, LANES), axis=1, keepdims=True)
        e2 = jnp.exp(v2 - v1)
        w1 = 1.0 / (1.0 + e2)
        w2 = e2 / (1.0 + e2)
        gate_scr[...] = jnp.where(lane == i1, w1, 0.0) + jnp.where(lane == i2, w2, 0.0)

    lane = lax.broadcasted_iota(jnp.int32, gate_scr.shape, 1)
    gcol = jnp.sum(jnp.where(lane == e, gate_scr[...], 0.0), axis=1, keepdims=True)
    h = h_scr[...]
    gt = _dot(h, wg_ref[...])
    up = _dot(h, wu_ref[...])
    act = (gt * jax.nn.sigmoid(gt)) * up * gcol
    acc_scr[...] += _dot(act.astype(BF16), wd_ref[...])

    last = jnp.logical_and(e == pl.num_programs(1) - 1, f == pl.num_programs(2) - 1)

    @pl.when(last)
    def _():
        y_ref[...] = x_ref[...] + g_ref[...] * acc_scr[...]


def _moe_dense(x, shift, scale, gate, router, wg, wu, wd, tm=512, tf=512):
    m, d = x.shape
    n_exp, _, ff = wg.shape
    tm = min(tm, m)
    vec = pl.BlockSpec((1, d), lambda i, e, f: (0, 0))
    router_pad = jnp.zeros((d, LANES), F32).at[:, :n_exp].set(router)
    return pl.pallas_call(
        functools.partial(_moe_kernel, n_exp),
        grid=(m // tm, n_exp, ff // tf),
        in_specs=[
            pl.BlockSpec((tm, d), lambda i, e, f: (i, 0)), vec, vec, vec,
            pl.BlockSpec((d, LANES), lambda i, e, f: (0, 0)),
            pl.BlockSpec((None, d, tf), lambda i, e, f: (e, 0, f)),
            pl.BlockSpec((None, d, tf), lambda i, e, f: (e, 0, f)),
            pl.BlockSpec((None, tf, d), lambda i, e, f: (e, f, 0)),
        ],
        out_specs=pl.BlockSpec((tm, d), lambda i, e, f: (i, 0)),
        out_shape=jax.ShapeDtypeStruct((m, d), F32),
        scratch_shapes=[pltpu.VMEM((tm, d), BF16), pltpu.VMEM((tm, d), F32), pltpu.VMEM((tm, LANES), F32)],
        compiler_params=_cparams(("parallel", "arbitrary", "arbitrary")),
        name="moe_dense",
    )(x, shift, scale, gate, router_pad, wg, wu, wd)


def _rope_tables(rows):
    row = jnp.broadcast_to(jnp.arange(rows)[:, None], (rows, GRID_W)).reshape(-1).astype(F32)
    col = jnp.broadcast_to(jnp.arange(GRID_W)[None, :], (rows, GRID_W)).reshape(-1).astype(F32)
    n_freq = HEAD // 4
    inv = ROPE_THETA ** (-jnp.arange(n_freq, dtype=F32) / n_freq)
    ang = jnp.concatenate([row[:, None] * inv, col[:, None] * inv], axis=-1)
    cos, sin = jnp.cos(ang), jnp.sin(ang)
    return jnp.concatenate([cos, cos], axis=-1), jnp.concatenate([-sin, sin], axis=-1)


def _halves(a):
    shp = a.shape
    a = a.reshape(shp[:-1] + (shp[-1] // HEAD, HEAD // 2, 2))
    return jnp.swapaxes(a, -1, -2).reshape(shp)


def kernel(x, c, ctx, c_ctx, ada_w, ada_b, ev_w_in, ev_gate_b, ev_q_gain, ev_k_gain, ev_h_gain, ev_w_out,
           ev_ff_gate, ev_ff_up, ev_ff_down, od_w_in, od_q_gain, od_k_gain, od_lam_q1, od_lam_k1, od_lam_q2,
           od_lam_k2, od_sub_gain, od_w_out, od_router, od_ex_gate, od_ex_up, od_ex_down):
    bsz, s_len, d = x.shape
    n_ctx = ctx.shape[1]
    assert bsz == 1 and ada_w.shape[0] == 2
    a_heads = d // (2 * HEAD)
    a_kv = a_heads // GQA_GROUP
    b_heads = d // (2 * V_HEAD)
    c_heads = d // V_HEAD
    a_q, a_kvw, b_qk, b_v = a_heads * HEAD, a_kv * HEAD, b_heads * HEAD, b_heads * V_HEAD
    score_scale = HEAD ** -0.5

    x_lat, x_ctx = x[0], ctx[0]
    cvecs = jnp.zeros((8, d), F32).at[0].set(c[0]).at[1].set(c_ctx)
    mods = _modulation(cvecs, ada_w, ada_b)

    def mod(layer, who):
        return [mods[layer, who, k * d:(k + 1) * d][None, :] for k in range(6)]

    cos_l, sin_l = _rope_tables(s_len // GRID_W)
    cos_c, sin_c = jnp.ones((n_ctx, HEAD), F32), jnp.zeros((n_ctx, HEAD), F32)

    w_in = ev_w_in[0]
    o0 = 0
    wqa = w_in[:, o0:o0 + a_q]; o0 += a_q
    wka = w_in[:, o0:o0 + a_kvw]; o0 += a_kvw
    wva = w_in[:, o0:o0 + a_kvw]; o0 += a_kvw
    wqb = w_in[:, o0:o0 + b_qk]; o0 += b_qk
    wkb = w_in[:, o0:o0 + b_qk]; o0 += b_qk
    wvb = w_in[:, o0:o0 + b_v]; o0 += b_v
    wob = w_in[:, o0:o0 + b_v]; o0 += b_v
    wgt = w_in[:, o0:]
    n_gates = wgt.shape[1]
    w0 = jnp.concatenate([_halves(wqa), wvb, wqb, wkb, _halves(wka), wva, wob], axis=1).astype(BF16)
    ones = lambda n: jnp.ones((n,), F32)
    cs0 = jnp.concatenate([
        jnp.tile(_halves(ev_q_gain[0]), a_heads) * score_scale, ones(b_v), ones(b_qk) * (HEAD ** -0.5), ones(b_qk),
        jnp.tile(_halves(ev_k_gain[0]), a_kv), ones(a_kvw), ones(b_v)])[None, :]
    modes0 = (["R"] * a_heads + ["P"] * (b_v // HEAD) + ["P"] * (2 * b_qk // HEAD) + ["R"] * a_kv
              + ["P"] * a_kv + ["P"] * (b_v // HEAD))
    nbf0 = a_q + b_v + 2 * b_qk + 2 * a_kvw
    wg0 = jnp.zeros((d, LANES), F32).at[:, :n_gates].set(wgt).astype(BF16)
    gb0 = jnp.zeros((1, LANES), F32).at[0, :n_gates].set(ev_gate_b[0])
    col_vb, col_qb, col_kb = a_q // b_v, (a_q + b_v) // b_qk, (a_q + b_v + b_qk) // b_qk
    col_ka = (a_q + b_v + 2 * b_qk) // HEAD
    col_va = col_ka + a_kv

    ml, mc = mod(0, 0), mod(0, 1)
    pl_b, pl_o, pl_g = _proj(x_lat, ml[0], ml[1], w0, cs0, cos_l, sin_l, modes0, nbf0, wg0, gb0)
    pc_b, pc_o, pc_g = _proj(x_ctx, mc[0], mc[1], w0, cs0, cos_c, sin_c, modes0, nbf0, wg0, gb0)
    p_all = jnp.concatenate([pl_b, pc_b], axis=0)
    g_all = jnp.concatenate([pl_g, pc_g], axis=0)
    g_row = g_all[:, :4 * b_heads].T

    att_l = _gqa_attention(pl_b, p_all, a_kv, col_ka, col_va, tq=512, tk=1280)
    att_c = _gqa_attention(pc_b, pc_b, a_kv, col_ka, col_va, tq=256, tk=256)
    hf, hb = _mlstm(p_all, g_all, g_row, b_heads, col_qb, col_kb, col_vb, s_len, n_ctx)

    hg = ev_h_gain[0].reshape(1, b_v)
    w_out0 = ev_w_out[0].astype(BF16)
    x_lat = _out_proj(x_lat, ml[2], w_out0, att_l, merge=(hf, hb, pl_o, hg, b_heads))
    x_ctx = _out_proj(x_ctx, mc[2], w_out0, att_c, merge=(hf, hb, pc_o, hg, b_heads), row_off=s_len)
    ffw = (ev_ff_gate[0].astype(BF16), ev_ff_up[0].astype(BF16), ev_ff_down[0].astype(BF16))
    x_lat = _ffn_dense(x_lat, ml[3], ml[4], ml[5], *ffw)
    x_ctx = _ffn_dense(x_ctx, mc[3], mc[4], mc[5], *ffw)

    c_qk = c_heads * 2 * HEAD
    w1 = od_w_in[0]
    w1 = jnp.concatenate([_halves(w1[:, :2 * c_qk]), w1[:, 2 * c_qk:]], axis=1).astype(BF16)
    cs1 = jnp.concatenate([jnp.tile(_halves(od_q_gain[0]), 2 * c_heads) * score_scale,
                           jnp.tile(_halves(od_k_gain[0]), 2 * c_heads), ones(c_heads * V_HEAD)])[None, :]
    modes1 = ["R"] * (4 * c_heads) + ["P"] * (c_heads * V_HEAD // HEAD)
    ml, mc = mod(1, 0), mod(1, 1)
    (ql,) = _proj(x_lat, ml[0], ml[1], w1, cs1, cos_l, sin_l, modes1, w1.shape[1])
    (qc,) = _proj(x_ctx, mc[0], mc[1], w1, cs1, cos_c, sin_c, modes1, w1.shape[1])
    kv_all = jnp.concatenate([ql, qc], axis=0)
    lam_init = 0.8 - 0.6 * math.exp(-0.3 * 1)
    vec = lambda a: a[0].reshape(1, -1)
    o_l = _diff_attention(ql, kv_all, c_heads, lam_init, vec(od_lam_q1), vec(od_lam_k1), vec(od_lam_q2),
                          vec(od_lam_k2), vec(od_sub_gain), tq=1024, tk=1280)
    x_lat = _out_proj(x_lat, ml[2], od_w_out[0].astype(BF16), o_l)
    x_lat = _moe_dense(x_lat, ml[3], ml[4], ml[5], od_router[0], od_ex_gate[0].astype(BF16),
                       od_ex_up[0].astype(BF16), od_ex_down[0].astype(BF16))
    return x_lat[None]
```

```python
import functools
import math

import jax
import jax.numpy as jnp
from jax import lax
from jax.experimental import pallas as pl
from jax.experimental.pallas import tpu as pltpu

F32 = jnp.float32
BF16 = jnp.bfloat16

GRID_W = 64
ROPE_THETA = 10000.0
EPS = 1e-6
M_INIT = -1e30
HEAD = 128
V_HEAD = 256
GQA_GROUP = 4
TOP_K = 2
MLSTM_CHUNK = 256
GQA_SUB_ROWS = 256
DIFF_SUB_ROWS = 128
SOFTMAX_ROW_GROUP = 64
LANES = 128
VMEM_LIMIT = 56 * 1024 * 1024
NEG_INF = float("-inf")


def _cparams(sem, flags=None):
    return pltpu.CompilerParams(dimension_semantics=sem, vmem_limit_bytes=VMEM_LIMIT, flags=flags)


ATTN_FLAGS = None


def _dot(a, b):
    return jnp.dot(a, b, preferred_element_type=F32)


def _dot_nt(a, b):
    return lax.dot_general(a, b, (((1,), (1,)), ((), ())), preferred_element_type=F32)


def _dot_tn(a, b):
    return lax.dot_general(a, b, (((0,), (0,)), ((), ())), preferred_element_type=F32)


def _rms(x):
    return x * lax.rsqrt(jnp.mean(x * x, axis=-1, keepdims=True) + EPS)


def _mod_kernel(c_ref, w_ref, b_ref, o_ref):
    c = c_ref[...]
    s = c * jax.nn.sigmoid(c)
    o_ref[...] = _dot(s.astype(BF16), w_ref[...].astype(BF16)) + b_ref[...]


def _modulation(cvecs, ada_w, ada_b):
    depth, d, n = ada_w.shape
    tn = 1024
    return pl.pallas_call(
        _mod_kernel,
        grid=(depth, n // tn),
        in_specs=[
            pl.BlockSpec((8, d), lambda l, j: (0, 0)),
            pl.BlockSpec((None, d, tn), lambda l, j: (l, 0, j)),
            pl.BlockSpec((None, 1, tn), lambda l, j: (l, 0, j)),
        ],
        out_specs=pl.BlockSpec((None, 8, tn), lambda l, j: (l, 0, j)),
        out_shape=jax.ShapeDtypeStruct((depth, 8, n), F32),
        compiler_params=_cparams(("parallel", "parallel")),
        name="adaln_mod",
    )(cvecs, ada_w, ada_b.reshape(depth, 1, n))


def _proj_kernel(patterns, n_bf_tiles, has_f32, has_gates, tn, *refs):
    it = iter(refs)
    x_ref, sh_ref, sc_ref, w_ref, cs_ref, cos_ref, sin_ref = (next(it) for _ in range(7))
    if has_gates:
        wg_ref, gb_ref = next(it), next(it)
    ob_ref = next(it)
    of_ref = next(it) if has_f32 else None
    og_ref = next(it) if has_gates else None
    h_scr = next(it)

    j = pl.program_id(1)

    @pl.when(j == 0)
    def _():
        h = _rms(x_ref[...]) * (1.0 + sc_ref[...]) + sh_ref[...]
        h_scr[...] = h.astype(BF16)
        if has_gates:
            og_ref[...] = _dot(h_scr[...], wg_ref[...]) + gb_ref[...]

    acc = _dot(h_scr[...], w_ref[...])
    cs = cs_ref[...]

    for lo, hi, pat in patterns:
        @pl.when(jnp.logical_and(j >= lo, j < hi))
        def _(pat=pat, lo=lo):
            to_f32 = has_f32 and lo >= n_bf_tiles
            out_ref = of_ref if to_f32 else ob_ref
            for hidx, mode in enumerate(pat):
                sl = slice(hidx * HEAD, (hidx + 1) * HEAD)
                y = acc[:, sl]
                if mode == "R":
                    y = _rms(y) * cs[:, sl]
                    y = y * cos_ref[...] + pltpu.roll(y, HEAD // 2, 1) * sin_ref[...]
                else:
                    y = y * cs[:, sl]
                out_ref[:, sl] = y.astype(out_ref.dtype)


def _proj(x, shift, scale, w, colscale, cos2, sin2, head_modes, n_bf_cols, wg=None, gb=None, tm=1024, tn=512):
    m, d = x.shape
    n = w.shape[1]
    tm = min(tm, m)
    assert m % tm == 0 and n % tn == 0 and n_bf_cols % tn == 0
    nj = n // tn
    n_bf_tiles = n_bf_cols // tn
    has_f32 = n_bf_cols < n
    has_gates = wg is not None
    hpt = tn // HEAD
    tiles = [tuple(head_modes[t * hpt:(t + 1) * hpt]) for t in range(nj)]
    patterns = []
    for t, pat in enumerate(tiles):
        if patterns and patterns[-1][2] == pat and not (has_f32 and t == n_bf_tiles):
            patterns[-1] = (patterns[-1][0], t + 1, pat)
        else:
            patterns.append((t, t + 1, pat))

    in_specs = [
        pl.BlockSpec((tm, d), lambda i, j: (i, 0)),
        pl.BlockSpec((1, d), lambda i, j: (0, 0)),
        pl.BlockSpec((1, d), lambda i, j: (0, 0)),
        pl.BlockSpec((d, tn), lambda i, j: (0, j)),
        pl.BlockSpec((1, tn), lambda i, j: (0, j)),
        pl.BlockSpec((tm, HEAD), lambda i, j: (i, 0)),
        pl.BlockSpec((tm, HEAD), lambda i, j: (i, 0)),
    ]
    args = [x, shift, scale, w, colscale, cos2, sin2]
    if has_gates:
        in_specs += [pl.BlockSpec((d, LANES), lambda i, j: (0, 0)), pl.BlockSpec((1, LANES), lambda i, j: (0, 0))]
        args += [wg, gb]
    last_bf = n_bf_tiles - 1
    out_specs = [pl.BlockSpec((tm, tn), lambda i, j: (i, jnp.minimum(j, last_bf)))]
    out_shape = [jax.ShapeDtypeStruct((m, n_bf_cols), BF16)]
    if has_f32:
        out_specs.append(pl.BlockSpec((tm, tn), lambda i, j: (i, jnp.maximum(j - n_bf_tiles, 0))))
        out_shape.append(jax.ShapeDtypeStruct((m, n - n_bf_cols), F32))
    if has_gates:
        out_specs.append(pl.BlockSpec((tm, LANES), lambda i, j: (i, 0)))
        out_shape.append(jax.ShapeDtypeStruct((m, LANES), F32))
    return pl.pallas_call(
        functools.partial(_proj_kernel, tuple(patterns), n_bf_tiles, has_f32, has_gates, tn),
        grid=(m // tm, nj),
        in_specs=in_specs,
        out_specs=out_specs,
        out_shape=out_shape,
        scratch_shapes=[pltpu.VMEM((tm, d), BF16)],
        compiler_params=_cparams(("parallel", "arbitrary")),
        name="norm_mod_proj",
    )(*args)


def _softmax_step(s, v, rows, m_scr, l_scr, acc_scr, s_buf, p_buf):
    sub, tk = s_buf.shape
    n_slab = tk // LANES
    s_buf[...] = s
    mx = s_buf[:, 0:LANES]
    for c in range(1, n_slab):
        mx = jnp.maximum(mx, s_buf[:, c * LANES:(c + 1) * LANES])
    m_prev = m_scr[rows, :]
    m_new = jnp.maximum(m_prev, jnp.max(mx, axis=1, keepdims=True))
    alpha = jnp.exp2(m_prev - m_new)
    m_scr[rows, :] = m_new
    rg = min(sub, SOFTMAX_ROW_GROUP)
    for g in range(sub // rg):
        gr = slice(g * rg, (g + 1) * rg)
        mb = jnp.broadcast_to(m_new[gr, :], (rg, LANES))
        lsum = jnp.zeros((rg, LANES), F32)
        for c in range(n_slab):
            cs = slice(c * LANES, (c + 1) * LANES)
            p = jnp.exp2(s_buf[gr, cs] - mb)
            lsum = lsum + p
            p_buf[gr, cs] = p.astype(BF16)
        lr = slice(rows.start + g * rg, rows.start + (g + 1) * rg)
        l_scr[lr, :] = alpha[gr, :] * l_scr[lr, :] + lsum
    acc_scr[rows, :] = alpha * acc_scr[rows, :] + _dot(p_buf[...], v)


def _gqa_kernel(tq, q_ref, kt_ref, v_ref, o_ref, m_scr, l_scr, acc_scr, s_scr, p_scr):
    kj = pl.program_id(2)

    @pl.when(kj == 0)
    def _():
        m_scr[...] = jnp.full(m_scr.shape, NEG_INF, F32)
        l_scr[...] = jnp.zeros(l_scr.shape, F32)
        acc_scr[...] = jnp.zeros(acc_scr.shape, F32)

    kt = kt_ref[...]
    v = v_ref[...]
    sub = s_scr.shape[1]
    nb = tq // sub
    for g in range(GQA_GROUP):
        for b in range(nb):
            s = _dot(q_ref[b * sub:(b + 1) * sub, g * HEAD:(g + 1) * HEAD], kt)
            _softmax_step(s, v, slice(g * tq + b * sub, g * tq + (b + 1) * sub), m_scr, l_scr, acc_scr,
                          s_scr.at[g * nb + b], p_scr.at[g * nb + b])

    @pl.when(kj == pl.num_programs(2) - 1)
    def _():
        out = acc_scr[...] / jnp.sum(l_scr[...], axis=1, keepdims=True)
        for g in range(GQA_GROUP):
            o_ref[:, g * HEAD:(g + 1) * HEAD] = out[g * tq:(g + 1) * tq, :].astype(o_ref.dtype)


def _gqa_attention(q_arr, kt_arr, kv_arr, n_kv, v_col, tq, tk):
    s_len = q_arr.shape[0]
    lk = kv_arr.shape[0]
    tq = min(tq, s_len)
    tk = min(tk, lk)
    assert s_len % tq == 0 and lk % tk == 0 and tk % LANES == 0
    gw = GQA_GROUP * HEAD
    sub = min(tq, GQA_SUB_ROWS)
    n_chain = GQA_GROUP * (tq // sub)
    return pl.pallas_call(
        functools.partial(_gqa_kernel, tq),
        grid=(n_kv, s_len // tq, lk // tk),
        in_specs=[
            pl.BlockSpec((tq, gw), lambda n, i, j: (i, n)),
            pl.BlockSpec((HEAD, tk), lambda n, i, j: (n, j)),
            pl.BlockSpec((tk, HEAD), lambda n, i, j: (j, v_col + n)),
        ],
        out_specs=pl.BlockSpec((tq, gw), lambda n, i, j: (i, n)),
        out_shape=jax.ShapeDtypeStruct((s_len, n_kv * gw), BF16),
        scratch_shapes=[
            pltpu.VMEM((GQA_GROUP * tq, 1), F32),
            pltpu.VMEM((GQA_GROUP * tq, LANES), F32),
            pltpu.VMEM((GQA_GROUP * tq, HEAD), F32),
            pltpu.VMEM((n_chain, sub, tk), F32),
            pltpu.VMEM((n_chain, sub, tk), BF16),
        ],
        compiler_params=_cparams(("parallel", "parallel", "arbitrary"), ATTN_FLAGS),
        name="gqa_flash",
    )(q_arr, kt_arr, kv_arr)


def _diff_kernel(tq, lam_init, q_ref, kt_ref, v_ref, lq1_ref, lk1_ref, lq2_ref, lk2_ref, sg_ref, o_ref,
                 m_scr, l_scr, acc_scr, s_scr, p_scr):
    kj = pl.program_id(2)

    @pl.when(kj == 0)
    def _():
        m_scr[...] = jnp.full(m_scr.shape, NEG_INF, F32)
        l_scr[...] = jnp.zeros(l_scr.shape, F32)
        acc_scr[...] = jnp.zeros(acc_scr.shape, F32)

    v = v_ref[...]
    sub = s_scr.shape[1]
    nb = tq // sub
    for r in range(2):
        kt = kt_ref[r * HEAD:(r + 1) * HEAD, :]
        for b in range(nb):
            s = _dot(q_ref[b * sub:(b + 1) * sub, r * HEAD:(r + 1) * HEAD], kt)
            _softmax_step(s, v, slice(r * tq + b * sub, r * tq + (b + 1) * sub), m_scr, l_scr, acc_scr,
                          s_scr.at[r * nb + b], p_scr.at[r * nb + b])

    @pl.when(kj == pl.num_programs(2) - 1)
    def _():
        lam = (jnp.exp(jnp.sum(lq1_ref[...] * lk1_ref[...], axis=1, keepdims=True))
               - jnp.exp(jnp.sum(lq2_ref[...] * lk2_ref[...], axis=1, keepdims=True)) + lam_init)
        o = acc_scr[...] / jnp.sum(l_scr[...], axis=1, keepdims=True)
        o = o[:tq, :] - lam * o[tq:, :]
        o = _rms(o) * sg_ref[...] * (1.0 - lam_init)
        o_ref[...] = o.astype(o_ref.dtype)


def _diff_attention(q_arr, kt_arr, kv_arr, n_heads, lam_init, lq1, lk1, lq2, lk2, sub_gain, tq, tk):
    s_len = q_arr.shape[0]
    lk = kv_arr.shape[0]
    tq = min(tq, s_len)
    tk = min(tk, lk)
    assert s_len % tq == 0 and lk % tk == 0 and tk % LANES == 0
    sub = min(tq, DIFF_SUB_ROWS)
    n_chain = 2 * (tq // sub)
    vec = pl.BlockSpec((1, HEAD), lambda h, i, j: (0, 0))
    return pl.pallas_call(
        functools.partial(_diff_kernel, tq, lam_init),
        grid=(n_heads, s_len // tq, lk // tk),
        in_specs=[
            pl.BlockSpec((tq, V_HEAD), lambda h, i, j: (i, h)),
            pl.BlockSpec((V_HEAD, tk), lambda h, i, j: (h, j)),
            pl.BlockSpec((tk, V_HEAD), lambda h, i, j: (j, 2 * n_heads + h)),
            vec, vec, vec, vec,
            pl.BlockSpec((1, V_HEAD), lambda h, i, j: (0, 0)),
        ],
        out_specs=pl.BlockSpec((tq, V_HEAD), lambda h, i, j: (i, h)),
        out_shape=jax.ShapeDtypeStruct((s_len, n_heads * V_HEAD), BF16),
        scratch_shapes=[
            pltpu.VMEM((2 * tq, 1), F32),
            pltpu.VMEM((2 * tq, LANES), F32),
            pltpu.VMEM((2 * tq, V_HEAD), F32),
            pltpu.VMEM((n_chain, sub, tk), F32),
            pltpu.VMEM((n_chain, sub, tk), BF16),
        ],
        compiler_params=_cparams(("parallel", "parallel", "arbitrary"), ATTN_FLAGS),
        name="diff_flash",
    )(q_arr, kt_arr, kv_arr, lq1, lk1, lq2, lk2, sub_gain)


def _split3(x):
    hi = x.astype(BF16)
    r1 = x - hi.astype(F32)
    mid = r1.astype(BF16)
    lo = (r1 - mid.astype(F32)).astype(BF16)
    return hi, mid, lo


def _log_sigmoid(x):
    return jnp.minimum(x, 0.0) - jnp.log(1.0 + jnp.exp(-jnp.abs(x)))


def _mlstm_kernel(nh, chunk, qf_ref, kf_ref, vf_ref, gcf_ref, grf_ref, qb_ref, kb_ref, vb_ref, gcb_ref, grb_ref,
                  hf_ref, hb_ref, c_scr, n_scr, m_scr):
    t = pl.program_id(0)

    @pl.when(t == 0)
    def _():
        c_scr[...] = jnp.zeros(c_scr.shape, F32)
        n_scr[...] = jnp.zeros(n_scr.shape, F32)
        m_scr[...] = jnp.full(m_scr.shape, M_INIT, F32)

    row = lax.broadcasted_iota(jnp.int32, (chunk, chunk), 0)
    col = lax.broadcasted_iota(jnp.int32, (chunk, chunk), 1)
    lower = col <= row
    upper = col >= row
    lower_b = lower.astype(BF16)
    upper_b = upper.astype(BF16)

    for rev in (False, True):
        q_ref, k_ref, v_ref, gc_ref, gr_ref, h_ref = (
            (qb_ref, kb_ref, vb_ref, gcb_ref, grb_ref, hb_ref) if rev else
            (qf_ref, kf_ref, vf_ref, gcf_ref, grf_ref, hf_ref))
        mask = upper if rev else lower
        gcol = gc_ref[...]
        grow = gr_ref[...]
        cum_col = sum(_dot(upper_b if rev else lower_b, piece) for piece in _split3(_log_sigmoid(gcol)))
        cum_row = sum(_dot(piece, lower_b if rev else upper_b) for piece in _split3(_log_sigmoid(grow)))
        for hd in range(nh):
            idx = (1 if rev else 0) * nh + hd
            ki = (2 if rev else 0) * nh + hd
            kf = ki + nh
            qc = q_ref[:, hd * HEAD:(hd + 1) * HEAD]
            kc = k_ref[:, hd * HEAD:(hd + 1) * HEAD]
            vc = v_ref[:, hd * V_HEAD:(hd + 1) * V_HEAD]
            i_col = gcol[:, ki:ki + 1]
            i_row = grow[ki:ki + 1, :]
            b_col = cum_col[:, kf:kf + 1]
            b_row = cum_row[kf:kf + 1, :]
            c_mem = c_scr[idx]
            n_mem = n_scr[idx]
            m = m_scr[idx][:, :1]

            dlog = jnp.where(mask, b_col + (i_row - b_row), NEG_INF)
            inter = b_col + m
            m_out = jnp.maximum(inter, jnp.max(dlog, axis=1, keepdims=True))
            w_intra = jnp.exp(dlog - m_out)
            w_inter = jnp.exp(inter - m_out)
            qk = _dot_nt(qc, kc) * w_intra
            num = w_inter * _dot(qc, c_mem.astype(BF16)) + _dot(qk.astype(BF16), vc)
            den = (w_inter * jnp.sum(qc.astype(F32) * n_mem, axis=1, keepdims=True)
                   + jnp.sum(qk, axis=1, keepdims=True))
            h = num / jnp.maximum(jnp.abs(den), jnp.exp(-m_out))
            h_ref[:, hd * V_HEAD:(hd + 1) * V_HEAD] = h

            b_last = b_col[0:1, :] if rev else b_col[chunk - 1:chunk, :]
            g_col = b_last - b_col + i_col
            m_new = jnp.maximum(b_last + m, jnp.max(g_col, axis=0, keepdims=True))
            a_dec = jnp.exp(b_last + m - m_new)
            kw = kc.astype(F32) * jnp.exp(g_col - m_new)
            c_scr[idx] = a_dec * c_mem + _dot_tn(kw.astype(BF16), vc)
            n_scr[idx] = a_dec * n_mem + jnp.sum(kw, axis=0, keepdims=True)
            m_scr[idx] = jnp.broadcast_to(m_new, (1, LANES))


def _mlstm(qkv, gates_col, gates_row, nh, q_col, k_col, v_col, n_lat, n_ctx):
    t_len = qkv.shape[0]
    chunk = MLSTM_CHUNK
    assert n_lat % chunk == 0 and n_ctx % chunk == 0 and t_len == n_lat + n_ctx
    nlat, nctx = n_lat // chunk, n_ctx // chunk
    nc = nlat + nctx

    def fwd(t):
        return jnp.where(t < nctx, nlat + t, t - nctx)

    def bwd(t):
        return jnp.where(t < nctx, nc - 1 - t, nlat - 1 - (t - nctx))

    qw, vw = nh * HEAD, nh * V_HEAD

    def specs(order):
        return [
            pl.BlockSpec((chunk, qw), lambda t: (order(t), q_col)),
            pl.BlockSpec((chunk, qw), lambda t: (order(t), k_col)),
            pl.BlockSpec((chunk, vw), lambda t: (order(t), v_col)),
            pl.BlockSpec((chunk, LANES), lambda t: (order(t), 0)),
            pl.BlockSpec((4 * nh, chunk), lambda t: (0, order(t))),
        ]

    return pl.pallas_call(
        functools.partial(_mlstm_kernel, nh, chunk),
        grid=(nc,),
        in_specs=specs(fwd) + specs(bwd),
        out_specs=[pl.BlockSpec((chunk, vw), lambda t: (fwd(t), 0)),
                   pl.BlockSpec((chunk, vw), lambda t: (bwd(t), 0))],
        out_shape=[jax.ShapeDtypeStruct((t_len, vw), F32)] * 2,
        scratch_shapes=[
            pltpu.VMEM((2 * nh, HEAD, V_HEAD), F32),
            pltpu.VMEM((2 * nh, 1, HEAD), F32),
            pltpu.VMEM((2 * nh, 1, LANES), F32),
        ],
        compiler_params=_cparams(("arbitrary",)),
        name="mlstm_scan",
    )(qkv, qkv, qkv, gates_col, gates_row, qkv, qkv, qkv, gates_col, gates_row)


def _out_kernel(merge, nh, *refs):
    if merge:
        a_ref, hf_ref, hb_ref, o_ref, hg_ref, w_ref, x_ref, g_ref, y_ref, mix_scr = refs
        aw = a_ref.shape[1]

        @pl.when(pl.program_id(1) == 0)
        def _():
            mix_scr[:, :aw] = a_ref[...]
            for hd in range(nh):
                sl = slice(hd * V_HEAD, (hd + 1) * V_HEAD)
                hm = _rms(hf_ref[:, sl] + hb_ref[:, sl]) * hg_ref[:, sl] * jax.nn.sigmoid(o_ref[:, sl])
                mix_scr[:, aw + hd * V_HEAD:aw + (hd + 1) * V_HEAD] = hm.astype(BF16)

        lhs = mix_scr[...]
    else:
        a_ref, w_ref, x_ref, g_ref, y_ref = refs
        lhs = a_ref[...]
    y_ref[...] = x_ref[...] + g_ref[...] * _dot(lhs, w_ref[...])


def _out_proj(x, gate, w, a, merge=None, row_off=0, tm=512, tn=1024):
    m, d = x.shape
    kdim, n = w.shape
    tm = min(tm, m)
    assert m % tm == 0 and n % tn == 0 and row_off % tm == 0
    off = row_off // tm
    in_specs = [pl.BlockSpec((tm, a.shape[1]), lambda i, j: (i, 0))]
    args = [a]
    scratch = []
    nh = 0
    if merge is not None:
        hf, hb, o, hg, nh = merge
        hw = hf.shape[1]
        in_specs += [pl.BlockSpec((tm, hw), lambda i, j: (i + off, 0)),
                     pl.BlockSpec((tm, hw), lambda i, j: (i + off, 0)),
                     pl.BlockSpec((tm, hw), lambda i, j: (i, 0)),
                     pl.BlockSpec((1, hw), lambda i, j: (0, 0))]
        args += [hf, hb, o, hg]
        scratch = [pltpu.VMEM((tm, kdim), BF16)]
    in_specs += [pl.BlockSpec((kdim, tn), lambda i, j: (0, j)),
                 pl.BlockSpec((tm, tn), lambda i, j: (i, j)),
                 pl.BlockSpec((1, tn), lambda i, j: (0, j))]
    args += [w, x, gate]
    return pl.pallas_call(
        functools.partial(_out_kernel, merge is not None, nh),
        grid=(m // tm, n // tn),
        in_specs=in_specs,
        out_specs=pl.BlockSpec((tm, tn), lambda i, j: (i, j)),
        out_shape=jax.ShapeDtypeStruct((m, n), F32),
        scratch_shapes=scratch,
        compiler_params=_cparams(("parallel", "arbitrary")),
        name="out_proj_residual",
    )(*args)


def _ffn_kernel(x_ref, sh_ref, sc_ref, g_ref, wg_ref, wu_ref, wd_ref, y_ref, h_scr, acc_scr):
    f = pl.program_id(1)

    @pl.when(f == 0)
    def _():
        h = _rms(x_ref[...]) * (1.0 + sc_ref[...]) + sh_ref[...]
        h_scr[...] = h.astype(BF16)
        acc_scr[...] = jnp.zeros(acc_scr.shape, F32)

    h = h_scr[...]
    gt = _dot(h, wg_ref[...])
    up = _dot(h, wu_ref[...])
    act = (gt * jax.nn.sigmoid(gt)) * up
    acc_scr[...] += _dot(act.astype(BF16), wd_ref[...])

    @pl.when(f == pl.num_programs(1) - 1)
    def _():
        y_ref[...] = x_ref[...] + g_ref[...] * acc_scr[...]


def _ffn_dense(x, shift, scale, gate, wg, wu, wd, tm=512, tf=512):
    m, d = x.shape
    ff = wg.shape[1]
    tm = min(tm, m)
    assert m % tm == 0 and ff % tf == 0
    vec = pl.BlockSpec((1, d), lambda i, f: (0, 0))
    return pl.pallas_call(
        _ffn_kernel,
        grid=(m // tm, ff // tf),
        in_specs=[
            pl.BlockSpec((tm, d), lambda i, f: (i, 0)), vec, vec, vec,
            pl.BlockSpec((d, tf), lambda i, f: (0, f)),
            pl.BlockSpec((d, tf), lambda i, f: (0, f)),
            pl.BlockSpec((tf, d), lambda i, f: (f, 0)),
        ],
        out_specs=pl.BlockSpec((tm, d), lambda i, f: (i, 0)),
        out_shape=jax.ShapeDtypeStruct((m, d), F32),
        scratch_shapes=[pltpu.VMEM((tm, d), BF16), pltpu.VMEM((tm, d), F32)],
        compiler_params=_cparams(("parallel", "arbitrary")),
        name="ffn_dense",
    )(x, shift, scale, gate, wg, wu, wd)


R_E1, R_E2, R_W1, R_W2, R_RANK1, R_RANK2 = range(6)


def _lane_pick(a, lane, k):
    return jnp.sum(jnp.where(lane == k, a, 0.0), axis=1, keepdims=True)


def _router_kernel(n_exp, x_ref, sh_ref, sc_ref, r_ref, info_ref, cnt_ref, carry_scr):
    @pl.when(pl.program_id(0) == 0)
    def _():
        carry_scr[...] = jnp.zeros(carry_scr.shape, F32)

    tm = x_ref.shape[0]
    h = _rms(x_ref[...]) * (1.0 + sc_ref[...]) + sh_ref[...]
    logits = jnp.dot(h, r_ref[...], preferred_element_type=F32, precision=lax.Precision.HIGHEST)
    lane = lax.broadcasted_iota(jnp.int32, logits.shape, 1)
    logits = jnp.where(lane < n_exp, logits, NEG_INF)
    v1 = jnp.max(logits, axis=1, keepdims=True)
    i1 = jnp.min(jnp.where(logits == v1, lane, LANES), axis=1, keepdims=True)
    rest = jnp.where(lane == i1, NEG_INF, logits)
    v2 = jnp.max(rest, axis=1, keepdims=True)
    i2 = jnp.min(jnp.where(rest == v2, lane, LANES), axis=1, keepdims=True)
    e2 = jnp.exp(v2 - v1)
    w1 = 1.0 / (1.0 + e2)
    w2 = e2 / (1.0 + e2)

    chosen = jnp.where(jnp.logical_or(lane == i1, lane == i2), 1.0, 0.0)
    r_i = lax.broadcasted_iota(jnp.int32, (tm, tm), 0)
    c_i = lax.broadcasted_iota(jnp.int32, (tm, tm), 1)
    before = (c_i < r_i).astype(BF16)
    rank = _dot(before, chosen.astype(BF16)) + carry_scr[...]
    carry_scr[...] += jnp.sum(chosen, axis=0, keepdims=True)
    cnt_ref[...] = carry_scr[...]
    rank1 = jnp.sum(jnp.where(lane == i1, rank, 0.0), axis=1, keepdims=True)
    rank2 = jnp.sum(jnp.where(lane == i2, rank, 0.0), axis=1, keepdims=True)
    info = jnp.zeros(logits.shape, F32)
    for k, val in ((R_E1, i1.astype(F32)), (R_E2, i2.astype(F32)), (R_W1, w1), (R_W2, w2),
                   (R_RANK1, rank1), (R_RANK2, rank2)):
        info = jnp.where(lane == k, val, info)
    info_ref[...] = info


def _router(x, shift, scale, router, tm=512):
    m, d = x.shape
    n_exp = router.shape[1]
    vec = pl.BlockSpec((1, d), lambda i: (0, 0))
    router_pad = jnp.zeros((d, LANES), F32).at[:, :n_exp].set(router)
    return pl.pallas_call(
        functools.partial(_router_kernel, n_exp),
        grid=(m // tm,),
        in_specs=[pl.BlockSpec((tm, d), lambda i: (i, 0)), vec, vec, pl.BlockSpec((d, LANES), lambda i: (0, 0))],
        out_specs=[pl.BlockSpec((tm, LANES), lambda i: (i, 0)), pl.BlockSpec((1, LANES), lambda i: (0, 0))],
        out_shape=[jax.ShapeDtypeStruct((m, LANES), F32), jax.ShapeDtypeStruct((1, LANES), F32)],
        scratch_shapes=[pltpu.VMEM((1, LANES), F32)],
        compiler_params=_cparams(("arbitrary",)),
        name="moe_router",
    )(x, shift, scale, router_pad)


def _row_copies(src_at, dst_at, n_rows, sem, start, valid=None):
    def body(r, carry):
        def go():
            cp = pltpu.make_async_copy(src_at(r), dst_at(r), sem)
            if start:
                cp.start()
            else:
                cp.wait()
        if valid is None:
            go()
        else:
            pl.when(valid(r))(go)
        return carry
    lax.fori_loop(0, n_rows, body, 0)


def _moe_ffn_kernel(te_ref, nu_ref, src_ref, dst_ref, x_hbm, sh_ref, sc_ref, wg_ref, wu_ref, wd_ref, y_hbm,
                    xbuf, hbuf, acc, sem):
    i = pl.program_id(0)
    f = pl.program_id(1)
    tm = xbuf.shape[0]
    active = i < nu_ref[0]

    @pl.when(jnp.logical_and(active, f == 0))
    def _():
        gather = (lambda r: x_hbm.at[pl.ds(src_ref[0, r], 1), :], lambda r: xbuf.at[pl.ds(r, 1), :])
        _row_copies(*gather, tm, sem.at[0], start=True)
        _row_copies(*gather, tm, sem.at[0], start=False)
        h = _rms(xbuf[...]) * (1.0 + sc_ref[...]) + sh_ref[...]
        hbuf[...] = h.astype(BF16)
        acc[...] = jnp.zeros(acc.shape, F32)

    @pl.when(active)
    def _():
        h = hbuf[...]
        gt = _dot(h, wg_ref[...])
        up = _dot(h, wu_ref[...])
        act = (gt * jax.nn.sigmoid(gt)) * up
        acc[...] += _dot(act.astype(BF16), wd_ref[...])

    @pl.when(jnp.logical_and(active, f == pl.num_programs(1) - 1))
    def _():
        scatter = (lambda r: acc.at[pl.ds(r, 1), :], lambda r: y_hbm.at[pl.ds(jnp.maximum(dst_ref[0, r], 0), 1), :])
        real = lambda r: dst_ref[0, r] >= 0
        _row_copies(*scatter, tm, sem.at[1], start=True, valid=real)
        _row_copies(*scatter, tm, sem.at[1], start=False, valid=real)


def _moe_ffn(x, shift, scale, tile_exp, n_used, src, dst, wg, wu, wd, n_out_rows, tm, tf=1024):
    m, d = x.shape
    n_exp, _, ff = wg.shape
    n_tiles = src.shape[0]
    tf = min(tf, ff)
    assert ff % tf == 0
    nf = ff // tf
    vec = pl.BlockSpec((1, d), lambda i, f, te, nu: (0, 0))
    idx = pl.BlockSpec((None, 1, tm), lambda i, f, te, nu: (i, 0, 0), memory_space=pltpu.SMEM)

    def f_eff(i, f, nu):
        return jnp.where(i < nu[0], f, nf - 1)

    grid_spec = pltpu.PrefetchScalarGridSpec(
        num_scalar_prefetch=2,
        grid=(n_tiles, nf),
        in_specs=[
            idx, idx,
            pl.BlockSpec(memory_space=pl.ANY),
            vec, vec,
            pl.BlockSpec((None, d, tf), lambda i, f, te, nu: (te[i], 0, f_eff(i, f, nu))),
            pl.BlockSpec((None, d, tf), lambda i, f, te, nu: (te[i], 0, f_eff(i, f, nu))),
            pl.BlockSpec((None, tf, d), lambda i, f, te, nu: (te[i], f_eff(i, f, nu), 0)),
        ],
        out_specs=pl.BlockSpec(memory_space=pl.ANY),
        scratch_shapes=[pltpu.VMEM((tm, d), F32), pltpu.VMEM((tm, d), BF16), pltpu.VMEM((tm, d), F32),
                        pltpu.SemaphoreType.DMA((2,))],
    )
    return pl.pallas_call(
        _moe_ffn_kernel,
        grid_spec=grid_spec,
        out_shape=jax.ShapeDtypeStruct((n_out_rows, d), F32),
        compiler_params=_cparams(("arbitrary", "arbitrary")),
        name="moe_grouped_ffn",
    )(tile_exp, n_used, src, dst, x, shift, scale, wg, wu, wd)


def _moe_combine_kernel(x_ref, g_ref, info_ref, y1_ref, y2_ref, o_ref):
    info = info_ref[...]
    lane = lax.broadcasted_iota(jnp.int32, info.shape, 1)
    w1 = _lane_pick(info, lane, R_W1)
    w2 = _lane_pick(info, lane, R_W2)
    o_ref[...] = x_ref[...] + g_ref[...] * (w1 * y1_ref[...] + w2 * y2_ref[...])


def _moe_combine(x, gate, info, y, tm=512):
    m, d = x.shape
    nb = m // tm
    return pl.pallas_call(
        _moe_combine_kernel,
        grid=(nb,),
        in_specs=[
            pl.BlockSpec((tm, d), lambda i: (i, 0)),
            pl.BlockSpec((1, d), lambda i: (0, 0)),
            pl.BlockSpec((tm, LANES), lambda i: (i, 0)),
            pl.BlockSpec((tm, d), lambda i: (i, 0)),
            pl.BlockSpec((tm, d), lambda i: (nb + i, 0)),
        ],
        out_specs=pl.BlockSpec((tm, d), lambda i: (i, 0)),
        out_shape=jax.ShapeDtypeStruct((m, d), F32),
        compiler_params=_cparams(("parallel",)),
        name="moe_combine",
    )(x, gate, info, y, y)


def _moe_top2(x, shift, scale, gate, router, wg, wu, wd, tm=512):
    m, d = x.shape
    n_exp = router.shape[1]
    info, counts = _router(x, shift, scale, router)
    e1, e2 = info[:, R_E1].astype(jnp.int32), info[:, R_E2].astype(jnp.int32)
    r1, r2 = info[:, R_RANK1].astype(jnp.int32), info[:, R_RANK2].astype(jnp.int32)
    cnt = counts[0, :n_exp].astype(jnp.int32)
    padded = (cnt + tm - 1) // tm * tm
    ends = jnp.cumsum(padded)
    offs = ends - padded
    pos1, pos2 = offs[e1] + r1, offs[e2] + r2
    n_tiles = TOP_K * m // tm + n_exp
    n_rows = n_tiles * tm
    tok = jnp.arange(m, dtype=jnp.int32)
    src = jnp.zeros((n_rows,), jnp.int32).at[pos1].set(tok).at[pos2].set(tok)
    dst = jnp.full((n_rows,), -1, jnp.int32).at[pos1].set(tok).at[pos2].set(m + tok)
    n_used = ends[-1] // tm
    tile_start = jnp.arange(n_tiles, dtype=jnp.int32) * tm
    tile_exp = jnp.searchsorted(ends, jnp.minimum(tile_start, ends[-1] - 1), side="right").astype(jnp.int32)
    y = _moe_ffn(x, shift, scale, jnp.minimum(tile_exp, n_exp - 1), n_used.reshape(1).astype(jnp.int32),
                 src.reshape(n_tiles, 1, tm), dst.reshape(n_tiles, 1, tm), wg, wu, wd, TOP_K * m, tm)
    return _moe_combine(x, gate, info, y)


def _rope_tables(rows):
    row = jnp.broadcast_to(jnp.arange(rows)[:, None], (rows, GRID_W)).reshape(-1).astype(F32)
    col = jnp.broadcast_to(jnp.arange(GRID_W)[None, :], (rows, GRID_W)).reshape(-1).astype(F32)
    n_freq = HEAD // 4
    inv = ROPE_THETA ** (-jnp.arange(n_freq, dtype=F32) / n_freq)
    ang = jnp.concatenate([row[:, None] * inv, col[:, None] * inv], axis=-1)
    cos, sin = jnp.cos(ang), jnp.sin(ang)
    return jnp.concatenate([cos, cos], axis=-1), jnp.concatenate([-sin, sin], axis=-1)


def _halves(a):
    shp = a.shape
    a = a.reshape(shp[:-1] + (shp[-1] // HEAD, HEAD // 2, 2))
    return jnp.swapaxes(a, -1, -2).reshape(shp)


def kernel(x, c, ctx, c_ctx, ada_w, ada_b, ev_w_in, ev_gate_b, ev_q_gain, ev_k_gain, ev_h_gain, ev_w_out,
           ev_ff_gate, ev_ff_up, ev_ff_down, od_w_in, od_q_gain, od_k_gain, od_lam_q1, od_lam_k1, od_lam_q2,
           od_lam_k2, od_sub_gain, od_w_out, od_router, od_ex_gate, od_ex_up, od_ex_down):
    bsz, s_len, d = x.shape
    n_ctx = ctx.shape[1]
    assert bsz == 1 and ada_w.shape[0] == 2
    a_heads = d // (2 * HEAD)
    a_kv = a_heads // GQA_GROUP
    b_heads = d // (2 * V_HEAD)
    c_heads = d // V_HEAD
    a_q, a_kvw, b_qk, b_v = a_heads * HEAD, a_kv * HEAD, b_heads * HEAD, b_heads * V_HEAD
    score_scale = HEAD ** -0.5 * math.log2(math.e)

    x_lat, x_ctx = x[0], ctx[0]
    cvecs = jnp.zeros((8, d), F32).at[0].set(c[0]).at[1].set(c_ctx)
    mods = _modulation(cvecs, ada_w, ada_b)

    def mod(layer, who):
        return [mods[layer, who, k * d:(k + 1) * d][None, :] for k in range(6)]

    cos_l, sin_l = _rope_tables(s_len // GRID_W)
    cos_c, sin_c = jnp.ones((n_ctx, HEAD), F32), jnp.zeros((n_ctx, HEAD), F32)

    w_in = ev_w_in[0]
    o0 = 0
    wqa = w_in[:, o0:o0 + a_q]; o0 += a_q
    wka = w_in[:, o0:o0 + a_kvw]; o0 += a_kvw
    wva = w_in[:, o0:o0 + a_kvw]; o0 += a_kvw
    wqb = w_in[:, o0:o0 + b_qk]; o0 += b_qk
    wkb = w_in[:, o0:o0 + b_qk]; o0 += b_qk
    wvb = w_in[:, o0:o0 + b_v]; o0 += b_v
    wob = w_in[:, o0:o0 + b_v]; o0 += b_v
    wgt = w_in[:, o0:]
    n_gates = wgt.shape[1]
    w0 = jnp.concatenate([_halves(wqa), wvb, wqb, wkb, _halves(wka), wva, wob], axis=1).astype(BF16)
    ones = lambda n: jnp.ones((n,), F32)
    cs0 = jnp.concatenate([
        jnp.tile(_halves(ev_q_gain[0]), a_heads) * score_scale, ones(b_v), ones(b_qk) * (HEAD ** -0.5), ones(b_qk),
        jnp.tile(_halves(ev_k_gain[0]), a_kv), ones(a_kvw), ones(b_v)])[None, :]
    modes0 = (["R"] * a_heads + ["P"] * (b_v // HEAD) + ["P"] * (2 * b_qk // HEAD) + ["R"] * a_kv
              + ["P"] * a_kv + ["P"] * (b_v // HEAD))
    nbf0 = a_q + b_v + 2 * b_qk + 2 * a_kvw
    wg0 = jnp.zeros((d, LANES), F32).at[:, :n_gates].set(wgt).astype(BF16)
    gb0 = jnp.zeros((1, LANES), F32).at[0, :n_gates].set(ev_gate_b[0])
    col_vb, col_qb, col_kb = a_q // b_v, (a_q + b_v) // b_qk, (a_q + b_v + b_qk) // b_qk
    col_ka = (a_q + b_v + 2 * b_qk) // HEAD
    col_va = col_ka + a_kv

    ml, mc = mod(0, 0), mod(0, 1)
    pl_b, pl_o, pl_g = _proj(x_lat, ml[0], ml[1], w0, cs0, cos_l, sin_l, modes0, nbf0, wg0, gb0)
    pc_b, pc_o, pc_g = _proj(x_ctx, mc[0], mc[1], w0, cs0, cos_c, sin_c, modes0, nbf0, wg0, gb0)
    p_all = jnp.concatenate([pl_b, pc_b], axis=0)
    g_all = jnp.concatenate([pl_g, pc_g], axis=0)
    g_row = g_all[:, :4 * b_heads].T

    ka_cols = slice(col_ka * HEAD, (col_ka + a_kv) * HEAD)
    att_l = _gqa_attention(pl_b, p_all[:, ka_cols].T, p_all, a_kv, col_va, tq=512, tk=1280)
    att_c = _gqa_attention(pc_b, pc_b[:, ka_cols].T, pc_b, a_kv, col_va, tq=256, tk=256)
    hf, hb = _mlstm(p_all, g_all, g_row, b_heads, col_qb, col_kb, col_vb, s_len, n_ctx)

    hg = ev_h_gain[0].reshape(1, b_v)
    w_out0 = ev_w_out[0].astype(BF16)
    x_lat = _out_proj(x_lat, ml[2], w_out0, att_l, merge=(hf, hb, pl_o, hg, b_heads))
    x_ctx = _out_proj(x_ctx, mc[2], w_out0, att_c, merge=(hf, hb, pc_o, hg, b_heads), row_off=s_len)
    ffw = (ev_ff_gate[0].astype(BF16), ev_ff_up[0].astype(BF16), ev_ff_down[0].astype(BF16))
    x_lat = _ffn_dense(x_lat, ml[3], ml[4], ml[5], *ffw)
    x_ctx = _ffn_dense(x_ctx, mc[3], mc[4], mc[5], *ffw)

    c_qk = c_heads * 2 * HEAD
    w1 = od_w_in[0]
    w1 = jnp.concatenate([_halves(w1[:, :2 * c_qk]), w1[:, 2 * c_qk:]], axis=1).astype(BF16)
    cs1 = jnp.concatenate([jnp.tile(_halves(od_q_gain[0]), 2 * c_heads) * score_scale,
                           jnp.tile(_halves(od_k_gain[0]), 2 * c_heads), ones(c_heads * V_HEAD)])[None, :]
    modes1 = ["R"] * (4 * c_heads) + ["P"] * (c_heads * V_HEAD // HEAD)
    ml, mc = mod(1, 0), mod(1, 1)
    (ql,) = _proj(x_lat, ml[0], ml[1], w1, cs1, cos_l, sin_l, modes1, w1.shape[1])
    (qc,) = _proj(x_ctx, mc[0], mc[1], w1, cs1, cos_c, sin_c, modes1, w1.shape[1])
    kv_all = jnp.concatenate([ql, qc], axis=0)
    lam_init = 0.8 - 0.6 * math.exp(-0.3 * 1)
    vec = lambda a: a[0].reshape(1, -1)
    kt_all = kv_all[:, c_qk:2 * c_qk].T
    o_l = _diff_attention(ql, kt_all, kv_all, c_heads, lam_init, vec(od_lam_q1), vec(od_lam_k1), vec(od_lam_q2),
                          vec(od_lam_k2), vec(od_sub_gain), tq=1024, tk=1280)
    x_lat = _out_proj(x_lat, ml[2], od_w_out[0].astype(BF16), o_l)
    x_lat = _moe_top2(x_lat, ml[3], ml[4], ml[5], od_router[0], od_ex_gate[0].astype(BF16),
                      od_ex_up[0].astype(BF16), od_ex_down[0].astype(BF16))
    return x_lat[None]
```

```python
import functools
import math

import jax
import jax.numpy as jnp
from jax import lax
from jax.experimental import pallas as pl
from jax.experimental.pallas import tpu as pltpu

F32 = jnp.float32
BF16 = jnp.bfloat16

GRID_W = 64
ROPE_THETA = 10000.0
EPS = 1e-6
M_INIT = -1e30
HEAD = 128
V_HEAD = 256
GQA_GROUP = 4
TOP_K = 2
MLSTM_CHUNK = 256
GQA_SUB_ROWS = 256
DIFF_SUB_ROWS = 256
PIPE_SKEW = 1
SOFTMAX_ROW_GROUP = 64
LANES = 128
VMEM_LIMIT = 56 * 1024 * 1024
NEG_INF = float("-inf")


def _cparams(sem, flags=None):
    return pltpu.CompilerParams(dimension_semantics=sem, vmem_limit_bytes=VMEM_LIMIT, flags=flags)


ATTN_FLAGS = None


def _dot(a, b):
    return jnp.dot(a, b, preferred_element_type=F32)


def _dot_nt(a, b):
    return lax.dot_general(a, b, (((1,), (1,)), ((), ())), preferred_element_type=F32)


def _dot_tn(a, b):
    return lax.dot_general(a, b, (((0,), (0,)), ((), ())), preferred_element_type=F32)


def _rms(x):
    return x * lax.rsqrt(jnp.mean(x * x, axis=-1, keepdims=True) + EPS)


def _mod_kernel(c_ref, w_ref, b_ref, o_ref):
    c = c_ref[...]
    s = c * jax.nn.sigmoid(c)
    o_ref[...] = _dot(s.astype(BF16), w_ref[...].astype(BF16)) + b_ref[...]


def _modulation(cvecs, ada_w, ada_b):
    depth, d, n = ada_w.shape
    tn = 1024
    return pl.pallas_call(
        _mod_kernel,
        grid=(depth, n // tn),
        in_specs=[
            pl.BlockSpec((8, d), lambda l, j: (0, 0)),
            pl.BlockSpec((None, d, tn), lambda l, j: (l, 0, j)),
            pl.BlockSpec((None, 1, tn), lambda l, j: (l, 0, j)),
        ],
        out_specs=pl.BlockSpec((None, 8, tn), lambda l, j: (l, 0, j)),
        out_shape=jax.ShapeDtypeStruct((depth, 8, n), F32),
        compiler_params=_cparams(("parallel", "parallel")),
        name="adaln_mod",
    )(cvecs, ada_w, ada_b.reshape(depth, 1, n))


def _proj_kernel(patterns, n_bf_tiles, has_f32, has_gates, tn, *refs):
    it = iter(refs)
    x_ref, sh_ref, sc_ref, w_ref, cs_ref, cos_ref, sin_ref = (next(it) for _ in range(7))
    if has_gates:
        wg_ref, gb_ref = next(it), next(it)
    ob_ref = next(it)
    of_ref = next(it) if has_f32 else None
    og_ref = next(it) if has_gates else None
    h_scr = next(it)

    j = pl.program_id(1)

    @pl.when(j == 0)
    def _():
        h = _rms(x_ref[...]) * (1.0 + sc_ref[...]) + sh_ref[...]
        h_scr[...] = h.astype(BF16)
        if has_gates:
            og_ref[...] = _dot(h_scr[...], wg_ref[...]) + gb_ref[...]

    acc = _dot(h_scr[...], w_ref[...])
    cs = cs_ref[...]

    for lo, hi, pat in patterns:
        @pl.when(jnp.logical_and(j >= lo, j < hi))
        def _(pat=pat, lo=lo):
            to_f32 = has_f32 and lo >= n_bf_tiles
            out_ref = of_ref if to_f32 else ob_ref
            for hidx, mode in enumerate(pat):
                sl = slice(hidx * HEAD, (hidx + 1) * HEAD)
                y = acc[:, sl]
                if mode == "R":
                    y = _rms(y) * cs[:, sl]
                    y = y * cos_ref[...] + pltpu.roll(y, HEAD // 2, 1) * sin_ref[...]
                else:
                    y = y * cs[:, sl]
                out_ref[:, sl] = y.astype(out_ref.dtype)


def _proj(x, shift, scale, w, colscale, cos2, sin2, head_modes, n_bf_cols, wg=None, gb=None, tm=1024, tn=512):
    m, d = x.shape
    n = w.shape[1]
    tm = min(tm, m)
    assert m % tm == 0 and n % tn == 0 and n_bf_cols % tn == 0
    nj = n // tn
    n_bf_tiles = n_bf_cols // tn
    has_f32 = n_bf_cols < n
    has_gates = wg is not None
    hpt = tn // HEAD
    tiles = [tuple(head_modes[t * hpt:(t + 1) * hpt]) for t in range(nj)]
    patterns = []
    for t, pat in enumerate(tiles):
        if patterns and patterns[-1][2] == pat and not (has_f32 and t == n_bf_tiles):
            patterns[-1] = (patterns[-1][0], t + 1, pat)
        else:
            patterns.append((t, t + 1, pat))

    in_specs = [
        pl.BlockSpec((tm, d), lambda i, j: (i, 0)),
        pl.BlockSpec((1, d), lambda i, j: (0, 0)),
        pl.BlockSpec((1, d), lambda i, j: (0, 0)),
        pl.BlockSpec((d, tn), lambda i, j: (0, j)),
        pl.BlockSpec((1, tn), lambda i, j: (0, j)),
        pl.BlockSpec((tm, HEAD), lambda i, j: (i, 0)),
        pl.BlockSpec((tm, HEAD), lambda i, j: (i, 0)),
    ]
    args = [x, shift, scale, w, colscale, cos2, sin2]
    if has_gates:
        in_specs += [pl.BlockSpec((d, LANES), lambda i, j: (0, 0)), pl.BlockSpec((1, LANES), lambda i, j: (0, 0))]
        args += [wg, gb]
    last_bf = n_bf_tiles - 1
    out_specs = [pl.BlockSpec((tm, tn), lambda i, j: (i, jnp.minimum(j, last_bf)))]
    out_shape = [jax.ShapeDtypeStruct((m, n_bf_cols), BF16)]
    if has_f32:
        out_specs.append(pl.BlockSpec((tm, tn), lambda i, j: (i, jnp.maximum(j - n_bf_tiles, 0))))
        out_shape.append(jax.ShapeDtypeStruct((m, n - n_bf_cols), F32))
    if has_gates:
        out_specs.append(pl.BlockSpec((tm, LANES), lambda i, j: (i, 0)))
        out_shape.append(jax.ShapeDtypeStruct((m, LANES), F32))
    return pl.pallas_call(
        functools.partial(_proj_kernel, tuple(patterns), n_bf_tiles, has_f32, has_gates, tn),
        grid=(m // tm, nj),
        in_specs=in_specs,
        out_specs=out_specs,
        out_shape=out_shape,
        scratch_shapes=[pltpu.VMEM((tm, d), BF16)],
        compiler_params=_cparams(("parallel", "arbitrary")),
        name="norm_mod_proj",
    )(*args)


def _softmax_pipeline(score_fns, v, row_slices, m_scr, l_scr, acc_scr, s_scr, p_scr):
    n = len(score_fns)
    alphas = {}
    k = PIPE_SKEW
    for t in range(n + 2 * k):
        if t < n:
            s_scr[t] = score_fns[t]()
        c = t - k
        if 0 <= c < n:
            alphas[c] = _softmax_probs(row_slices[c], m_scr, l_scr, s_scr.at[c], p_scr.at[c])
        c = t - 2 * k
        if 0 <= c < n:
            rows = row_slices[c]
            acc_scr[rows, :] = alphas.pop(c) * acc_scr[rows, :] + _dot(p_scr[c], v)


def _softmax_probs(rows, m_scr, l_scr, s_buf, p_buf):
    sub, tk = s_buf.shape
    n_slab = tk // LANES
    mx = s_buf[:, 0:LANES]
    for c in range(1, n_slab):
        mx = jnp.maximum(mx, s_buf[:, c * LANES:(c + 1) * LANES])
    m_prev = m_scr[rows, :]
    m_new = jnp.maximum(m_prev, jnp.max(mx, axis=1, keepdims=True))
    alpha = jnp.exp2(m_prev - m_new)
    m_scr[rows, :] = m_new
    rg = min(sub, SOFTMAX_ROW_GROUP)
    for g in range(sub // rg):
        gr = slice(g * rg, (g + 1) * rg)
        mb = jnp.broadcast_to(m_new[gr, :], (rg, LANES))
        lsum = jnp.zeros((rg, LANES), F32)
        for c in range(n_slab):
            cs = slice(c * LANES, (c + 1) * LANES)
            p = jnp.exp2(s_buf[gr, cs] - mb)
            if l_scr is not None:
                lsum = lsum + p
            p_buf[gr, cs] = p.astype(BF16)
        if l_scr is not None:
            lr = slice(rows.start + g * rg, rows.start + (g + 1) * rg)
            l_scr[lr, :] = alpha[gr, :] * l_scr[lr, :] + lsum
    return alpha


def _gqa_kernel(tq, q_ref, kt_ref, v_ref, o_ref, m_scr, acc_scr, s_scr, p_scr):
    kj = pl.program_id(2)

    @pl.when(kj == 0)
    def _():
        m_scr[...] = jnp.full(m_scr.shape, NEG_INF, F32)
        acc_scr[...] = jnp.zeros(acc_scr.shape, F32)

    kt = kt_ref[...]
    v = jnp.concatenate([v_ref[...], jnp.ones(v_ref.shape, BF16)], axis=1)
    sub = s_scr.shape[1]
    nb = tq // sub
    fns, rows = [], []
    for g in range(GQA_GROUP):
        for b in range(nb):
            fns.append(lambda g=g, b=b: _dot(q_ref[b * sub:(b + 1) * sub, g * HEAD:(g + 1) * HEAD], kt))
            rows.append(slice(g * tq + b * sub, g * tq + (b + 1) * sub))
    _softmax_pipeline(fns, v, rows, m_scr, None, acc_scr, s_scr, p_scr)

    @pl.when(kj == pl.num_programs(2) - 1)
    def _():
        out = acc_scr[:, :HEAD] / acc_scr[:, HEAD:HEAD + 1]
        for g in range(GQA_GROUP):
            o_ref[:, g * HEAD:(g + 1) * HEAD] = out[g * tq:(g + 1) * tq, :].astype(o_ref.dtype)


def _gqa_attention(q_arr, kt_arr, kv_arr, n_kv, v_col, tq, tk):
    s_len = q_arr.shape[0]
    lk = kv_arr.shape[0]
    tq = min(tq, s_len)
    tk = min(tk, lk)
    assert s_len % tq == 0 and lk % tk == 0 and tk % LANES == 0
    gw = GQA_GROUP * HEAD
    sub = min(tq, GQA_SUB_ROWS)
    n_chain = GQA_GROUP * (tq // sub)
    return pl.pallas_call(
        functools.partial(_gqa_kernel, tq),
        grid=(n_kv, s_len // tq, lk // tk),
        in_specs=[
            pl.BlockSpec((tq, gw), lambda n, i, j: (i, n)),
            pl.BlockSpec((HEAD, tk), lambda n, i, j: (n, j)),
            pl.BlockSpec((tk, HEAD), lambda n, i, j: (j, v_col + n)),
        ],
        out_specs=pl.BlockSpec((tq, gw), lambda n, i, j: (i, n)),
        out_shape=jax.ShapeDtypeStruct((s_len, n_kv * gw), BF16),
        scratch_shapes=[
            pltpu.VMEM((GQA_GROUP * tq, 1), F32),
            pltpu.VMEM((GQA_GROUP * tq, 2 * HEAD), F32),
            pltpu.VMEM((n_chain, sub, tk), F32),
            pltpu.VMEM((n_chain, sub, tk), BF16),
        ],
        compiler_params=_cparams(("parallel", "parallel", "arbitrary"), ATTN_FLAGS),
        name="gqa_flash",
    )(q_arr, kt_arr, kv_arr)


def _diff_kernel(tq, lam_init, q_ref, kt_ref, v_ref, lq1_ref, lk1_ref, lq2_ref, lk2_ref, sg_ref, o_ref,
                 m_scr, l_scr, acc_scr, s_scr, p_scr):
    kj = pl.program_id(2)

    @pl.when(kj == 0)
    def _():
        m_scr[...] = jnp.full(m_scr.shape, NEG_INF, F32)
        l_scr[...] = jnp.zeros(l_scr.shape, F32)
        acc_scr[...] = jnp.zeros(acc_scr.shape, F32)

    v = v_ref[...]
    sub = s_scr.shape[1]
    nb = tq // sub
    fns, rows = [], []
    for b in range(nb):
        for r in range(2):
            fns.append(lambda r=r, b=b: _dot(q_ref[b * sub:(b + 1) * sub, r * HEAD:(r + 1) * HEAD],
                                             kt_ref[r * HEAD:(r + 1) * HEAD, :]))
            rows.append(slice(r * tq + b * sub, r * tq + (b + 1) * sub))
    _softmax_pipeline(fns, v, rows, m_scr, l_scr, acc_scr, s_scr, p_scr)

    @pl.when(kj == pl.num_programs(2) - 1)
    def _():
        lam = (jnp.exp(jnp.sum(lq1_ref[...] * lk1_ref[...], axis=1, keepdims=True))
               - jnp.exp(jnp.sum(lq2_ref[...] * lk2_ref[...], axis=1, keepdims=True)) + lam_init)
        o = acc_scr[...] / jnp.sum(l_scr[...], axis=1, keepdims=True)
        o = o[:tq, :] - lam * o[tq:, :]
        o = _rms(o) * sg_ref[...] * (1.0 - lam_init)
        o_ref[...] = o.astype(o_ref.dtype)


def _diff_attention(q_arr, kt_arr, kv_arr, n_heads, lam_init, lq1, lk1, lq2, lk2, sub_gain, tq, tk):
    s_len = q_arr.shape[0]
    lk = kv_arr.shape[0]
    tq = min(tq, s_len)
    tk = min(tk, lk)
    assert s_len % tq == 0 and lk % tk == 0 and tk % LANES == 0
    sub = min(tq, DIFF_SUB_ROWS)
    n_chain = 2 * (tq // sub)
    vec = pl.BlockSpec((1, HEAD), lambda h, i, j: (0, 0))
    return pl.pallas_call(
        functools.partial(_diff_kernel, tq, lam_init),
        grid=(n_heads, s_len // tq, lk // tk),
        in_specs=[
            pl.BlockSpec((tq, V_HEAD), lambda h, i, j: (i, h)),
            pl.BlockSpec((V_HEAD, tk), lambda h, i, j: (h, j)),
            pl.BlockSpec((tk, V_HEAD), lambda h, i, j: (j, 2 * n_heads + h)),
            vec, vec, vec, vec,
            pl.BlockSpec((1, V_HEAD), lambda h, i, j: (0, 0)),
        ],
        out_specs=pl.BlockSpec((tq, V_HEAD), lambda h, i, j: (i, h)),
        out_shape=jax.ShapeDtypeStruct((s_len, n_heads * V_HEAD), BF16),
        scratch_shapes=[
            pltpu.VMEM((2 * tq, 1), F32),
            pltpu.VMEM((2 * tq, LANES), F32),
            pltpu.VMEM((2 * tq, V_HEAD), F32),
            pltpu.VMEM((n_chain, sub, tk), F32),
            pltpu.VMEM((n_chain, sub, tk), BF16),
        ],
        compiler_params=_cparams(("parallel", "parallel", "arbitrary"), ATTN_FLAGS),
        name="diff_flash",
    )(q_arr, kt_arr, kv_arr, lq1, lk1, lq2, lk2, sub_gain)


def _split3(x):
    hi = x.astype(BF16)
    r1 = x - hi.astype(F32)
    mid = r1.astype(BF16)
    lo = (r1 - mid.astype(F32)).astype(BF16)
    return hi, mid, lo


def _log_sigmoid(x):
    return jnp.minimum(x, 0.0) - jnp.log(1.0 + jnp.exp(-jnp.abs(x)))


def _mlstm_kernel(nh, chunk, qf_ref, kf_ref, vf_ref, gcf_ref, grf_ref, qb_ref, kb_ref, vb_ref, gcb_ref, grb_ref,
                  hf_ref, hb_ref, c_scr, n_scr, m_scr):
    t = pl.program_id(0)

    @pl.when(t == 0)
    def _():
        c_scr[...] = jnp.zeros(c_scr.shape, F32)
        n_scr[...] = jnp.zeros(n_scr.shape, F32)
        m_scr[...] = jnp.full(m_scr.shape, M_INIT, F32)

    row = lax.broadcasted_iota(jnp.int32, (chunk, chunk), 0)
    col = lax.broadcasted_iota(jnp.int32, (chunk, chunk), 1)
    lower = col <= row
    upper = col >= row
    lower_b = lower.astype(BF16)
    upper_b = upper.astype(BF16)

    for rev in (False, True):
        q_ref, k_ref, v_ref, gc_ref, gr_ref, h_ref = (
            (qb_ref, kb_ref, vb_ref, gcb_ref, grb_ref, hb_ref) if rev else
            (qf_ref, kf_ref, vf_ref, gcf_ref, grf_ref, hf_ref))
        mask = upper if rev else lower
        gcol = gc_ref[...]
        grow = gr_ref[...]
        cum_col = sum(_dot(upper_b if rev else lower_b, piece) for piece in _split3(_log_sigmoid(gcol)))
        cum_row = sum(_dot(piece, lower_b if rev else upper_b) for piece in _split3(_log_sigmoid(grow)))
        for hd in range(nh):
            idx = (1 if rev else 0) * nh + hd
            ki = (2 if rev else 0) * nh + hd
            kf = ki + nh
            qc = q_ref[:, hd * HEAD:(hd + 1) * HEAD]
            kc = k_ref[:, hd * HEAD:(hd + 1) * HEAD]
            vc = v_ref[:, hd * V_HEAD:(hd + 1) * V_HEAD]
            i_col = gcol[:, ki:ki + 1]
            i_row = grow[ki:ki + 1, :]
            b_col = cum_col[:, kf:kf + 1]
            b_row = cum_row[kf:kf + 1, :]
            c_mem = c_scr[idx]
            n_mem = n_scr[idx]
            m = m_scr[idx][:, :1]

            dlog = jnp.where(mask, b_col + (i_row - b_row), NEG_INF)
            inter = b_col + m
            m_out = jnp.maximum(inter, jnp.max(dlog, axis=1, keepdims=True))
            w_intra = jnp.exp(dlog - m_out)
            w_inter = jnp.exp(inter - m_out)
            qk = _dot_nt(qc, kc) * w_intra
            num = w_inter * _dot(qc, c_mem.astype(BF16)) + _dot(qk.astype(BF16), vc)
            den = (w_inter * jnp.sum(qc.astype(F32) * n_mem, axis=1, keepdims=True)
                   + jnp.sum(qk, axis=1, keepdims=True))
            h = num / jnp.maximum(jnp.abs(den), jnp.exp(-m_out))
            h_ref[:, hd * V_HEAD:(hd + 1) * V_HEAD] = h

            b_last = b_col[0:1, :] if rev else b_col[chunk - 1:chunk, :]
            g_col = b_last - b_col + i_col
            m_new = jnp.maximum(b_last + m, jnp.max(g_col, axis=0, keepdims=True))
            a_dec = jnp.exp(b_last + m - m_new)
            kw = kc.astype(F32) * jnp.exp(g_col - m_new)
            c_scr[idx] = a_dec * c_mem + _dot_tn(kw.astype(BF16), vc)
            n_scr[idx] = a_dec * n_mem + jnp.sum(kw, axis=0, keepdims=True)
            m_scr[idx] = jnp.broadcast_to(m_new, (1, LANES))


def _mlstm(qkv, gates_col, gates_row, nh, q_col, k_col, v_col, n_lat, n_ctx):
    t_len = qkv.shape[0]
    chunk = MLSTM_CHUNK
    assert n_lat % chunk == 0 and n_ctx % chunk == 0 and t_len == n_lat + n_ctx
    nlat, nctx = n_lat // chunk, n_ctx // chunk
    nc = nlat + nctx

    def fwd(t):
        return jnp.where(t < nctx, nlat + t, t - nctx)

    def bwd(t):
        return jnp.where(t < nctx, nc - 1 - t, nlat - 1 - (t - nctx))

    qw, vw = nh * HEAD, nh * V_HEAD

    def specs(order):
        return [
            pl.BlockSpec((chunk, qw), lambda t: (order(t), q_col)),
            pl.BlockSpec((chunk, qw), lambda t: (order(t), k_col)),
            pl.BlockSpec((chunk, vw), lambda t: (order(t), v_col)),
            pl.BlockSpec((chunk, LANES), lambda t: (order(t), 0)),
            pl.BlockSpec((4 * nh, chunk), lambda t: (0, order(t))),
        ]

    return pl.pallas_call(
        functools.partial(_mlstm_kernel, nh, chunk),
        grid=(nc,),
        in_specs=specs(fwd) + specs(bwd),
        out_specs=[pl.BlockSpec((chunk, vw), lambda t: (fwd(t), 0)),
                   pl.BlockSpec((chunk, vw), lambda t: (bwd(t), 0))],
        out_shape=[jax.ShapeDtypeStruct((t_len, vw), F32)] * 2,
        scratch_shapes=[
            pltpu.VMEM((2 * nh, HEAD, V_HEAD), F32),
            pltpu.VMEM((2 * nh, 1, HEAD), F32),
            pltpu.VMEM((2 * nh, 1, LANES), F32),
        ],
        compiler_params=_cparams(("arbitrary",)),
        name="mlstm_scan",
    )(qkv, qkv, qkv, gates_col, gates_row, qkv, qkv, qkv, gates_col, gates_row)


def _out_kernel(merge, nh, *refs):
    if merge:
        a_ref, hf_ref, hb_ref, o_ref, hg_ref, w_ref, x_ref, g_ref, y_ref, mix_scr = refs
        aw = a_ref.shape[1]

        @pl.when(pl.program_id(1) == 0)
        def _():
            mix_scr[:, :aw] = a_ref[...]
            for hd in range(nh):
                sl = slice(hd * V_HEAD, (hd + 1) * V_HEAD)
                hm = _rms(hf_ref[:, sl] + hb_ref[:, sl]) * hg_ref[:, sl] * jax.nn.sigmoid(o_ref[:, sl])
                mix_scr[:, aw + hd * V_HEAD:aw + (hd + 1) * V_HEAD] = hm.astype(BF16)

        lhs = mix_scr[...]
    else:
        a_ref, w_ref, x_ref, g_ref, y_ref = refs
        lhs = a_ref[...]
    y_ref[...] = x_ref[...] + g_ref[...] * _dot(lhs, w_ref[...])


def _out_proj(x, gate, w, a, merge=None, row_off=0, tm=512, tn=1024):
    m, d = x.shape
    kdim, n = w.shape
    tm = min(tm, m)
    assert m % tm == 0 and n % tn == 0 and row_off % tm == 0
    off = row_off // tm
    in_specs = [pl.BlockSpec((tm, a.shape[1]), lambda i, j: (i, 0))]
    args = [a]
    scratch = []
    nh = 0
    if merge is not None:
        hf, hb, o, hg, nh = merge
        hw = hf.shape[1]
        in_specs += [pl.BlockSpec((tm, hw), lambda i, j: (i + off, 0)),
                     pl.BlockSpec((tm, hw), lambda i, j: (i + off, 0)),
                     pl.BlockSpec((tm, hw), lambda i, j: (i, 0)),
                     pl.BlockSpec((1, hw), lambda i, j: (0, 0))]
        args += [hf, hb, o, hg]
        scratch = [pltpu.VMEM((tm, kdim), BF16)]
    in_specs += [pl.BlockSpec((kdim, tn), lambda i, j: (0, j)),
                 pl.BlockSpec((tm, tn), lambda i, j: (i, j)),
                 pl.BlockSpec((1, tn), lambda i, j: (0, j))]
    args += [w, x, gate]
    return pl.pallas_call(
        functools.partial(_out_kernel, merge is not None, nh),
        grid=(m // tm, n // tn),
        in_specs=in_specs,
        out_specs=pl.BlockSpec((tm, tn), lambda i, j: (i, j)),
        out_shape=jax.ShapeDtypeStruct((m, n), F32),
        scratch_shapes=scratch,
        compiler_params=_cparams(("parallel", "arbitrary")),
        name="out_proj_residual",
    )(*args)


def _ffn_kernel(x_ref, sh_ref, sc_ref, g_ref, wg_ref, wu_ref, wd_ref, y_ref, h_scr, acc_scr):
    f = pl.program_id(1)

    @pl.when(f == 0)
    def _():
        h = _rms(x_ref[...]) * (1.0 + sc_ref[...]) + sh_ref[...]
        h_scr[...] = h.astype(BF16)
        acc_scr[...] = jnp.zeros(acc_scr.shape, F32)

    h = h_scr[...]
    gt = _dot(h, wg_ref[...])
    up = _dot(h, wu_ref[...])
    act = (gt * jax.nn.sigmoid(gt)) * up
    acc_scr[...] += _dot(act.astype(BF16), wd_ref[...])

    @pl.when(f == pl.num_programs(1) - 1)
    def _():
        y_ref[...] = x_ref[...] + g_ref[...] * acc_scr[...]


def _ffn_dense(x, shift, scale, gate, wg, wu, wd, tm=512, tf=512):
    m, d = x.shape
    ff = wg.shape[1]
    tm = min(tm, m)
    assert m % tm == 0 and ff % tf == 0
    vec = pl.BlockSpec((1, d), lambda i, f: (0, 0))
    return pl.pallas_call(
        _ffn_kernel,
        grid=(m // tm, ff // tf),
        in_specs=[
            pl.BlockSpec((tm, d), lambda i, f: (i, 0)), vec, vec, vec,
            pl.BlockSpec((d, tf), lambda i, f: (0, f)),
            pl.BlockSpec((d, tf), lambda i, f: (0, f)),
            pl.BlockSpec((tf, d), lambda i, f: (f, 0)),
        ],
        out_specs=pl.BlockSpec((tm, d), lambda i, f: (i, 0)),
        out_shape=jax.ShapeDtypeStruct((m, d), F32),
        scratch_shapes=[pltpu.VMEM((tm, d), BF16), pltpu.VMEM((tm, d), F32)],
        compiler_params=_cparams(("parallel", "arbitrary")),
        name="ffn_dense",
    )(x, shift, scale, gate, wg, wu, wd)


R_E1, R_E2, R_W1, R_W2, R_RANK1, R_RANK2 = range(6)


def _lane_pick(a, lane, k):
    return jnp.sum(jnp.where(lane == k, a, 0.0), axis=1, keepdims=True)


def _router_kernel(n_exp, x_ref, sh_ref, sc_ref, r_ref, info_ref, cnt_ref, carry_scr):
    @pl.when(pl.program_id(0) == 0)
    def _():
        carry_scr[...] = jnp.zeros(carry_scr.shape, F32)

    tm = x_ref.shape[0]
    h = _rms(x_ref[...]) * (1.0 + sc_ref[...]) + sh_ref[...]
    logits = jnp.dot(h, r_ref[...], preferred_element_type=F32, precision=lax.Precision.HIGHEST)
    lane = lax.broadcasted_iota(jnp.int32, logits.shape, 1)
    logits = jnp.where(lane < n_exp, logits, NEG_INF)
    v1 = jnp.max(logits, axis=1, keepdims=True)
    i1 = jnp.min(jnp.where(logits == v1, lane, LANES), axis=1, keepdims=True)
    rest = jnp.where(lane == i1, NEG_INF, logits)
    v2 = jnp.max(rest, axis=1, keepdims=True)
    i2 = jnp.min(jnp.where(rest == v2, lane, LANES), axis=1, keepdims=True)
    e2 = jnp.exp(v2 - v1)
    w1 = 1.0 / (1.0 + e2)
    w2 = e2 / (1.0 + e2)

    chosen = jnp.where(jnp.logical_or(lane == i1, lane == i2), 1.0, 0.0)
    r_i = lax.broadcasted_iota(jnp.int32, (tm, tm), 0)
    c_i = lax.broadcasted_iota(jnp.int32, (tm, tm), 1)
    before = (c_i < r_i).astype(BF16)
    rank = _dot(before, chosen.astype(BF16)) + carry_scr[...]
    carry_scr[...] += jnp.sum(chosen, axis=0, keepdims=True)
    cnt_ref[...] = carry_scr[...]
    rank1 = jnp.sum(jnp.where(lane == i1, rank, 0.0), axis=1, keepdims=True)
    rank2 = jnp.sum(jnp.where(lane == i2, rank, 0.0), axis=1, keepdims=True)
    info = jnp.zeros(logits.shape, F32)
    for k, val in ((R_E1, i1.astype(F32)), (R_E2, i2.astype(F32)), (R_W1, w1), (R_W2, w2),
                   (R_RANK1, rank1), (R_RANK2, rank2)):
        info = jnp.where(lane == k, val, info)
    info_ref[...] = info


def _router(x, shift, scale, router, tm=512):
    m, d = x.shape
    n_exp = router.shape[1]
    vec = pl.BlockSpec((1, d), lambda i: (0, 0))
    router_pad = jnp.zeros((d, LANES), F32).at[:, :n_exp].set(router)
    return pl.pallas_call(
        functools.partial(_router_kernel, n_exp),
        grid=(m // tm,),
        in_specs=[pl.BlockSpec((tm, d), lambda i: (i, 0)), vec, vec, pl.BlockSpec((d, LANES), lambda i: (0, 0))],
        out_specs=[pl.BlockSpec((tm, LANES), lambda i: (i, 0)), pl.BlockSpec((1, LANES), lambda i: (0, 0))],
        out_shape=[jax.ShapeDtypeStruct((m, LANES), F32), jax.ShapeDtypeStruct((1, LANES), F32)],
        scratch_shapes=[pltpu.VMEM((1, LANES), F32)],
        compiler_params=_cparams(("arbitrary",)),
        name="moe_router",
    )(x, shift, scale, router_pad)


def _row_copies(src_at, dst_at, n_rows, sem, start, valid=None):
    def body(r, carry):
        def go():
            cp = pltpu.make_async_copy(src_at(r), dst_at(r), sem)
            if start:
                cp.start()
            else:
                cp.wait()
        if valid is None:
            go()
        else:
            pl.when(valid(r))(go)
        return carry
    lax.fori_loop(0, n_rows, body, 0)


def _moe_ffn_kernel(te_ref, nu_ref, src_ref, dst_ref, x_hbm, sh_ref, sc_ref, wg_ref, wu_ref, wd_ref, y_hbm,
                    xbuf, hbuf, acc, sem):
    i = pl.program_id(0)
    f = pl.program_id(1)
    tm = xbuf.shape[0]
    active = i < nu_ref[0]

    @pl.when(jnp.logical_and(active, f == 0))
    def _():
        gather = (lambda r: x_hbm.at[pl.ds(src_ref[0, r], 1), :], lambda r: xbuf.at[pl.ds(r, 1), :])
        _row_copies(*gather, tm, sem.at[0], start=True)
        _row_copies(*gather, tm, sem.at[0], start=False)
        h = _rms(xbuf[...]) * (1.0 + sc_ref[...]) + sh_ref[...]
        hbuf[...] = h.astype(BF16)
        acc[...] = jnp.zeros(acc.shape, F32)

    @pl.when(active)
    def _():
        h = hbuf[...]
        gt = _dot(h, wg_ref[...])
        up = _dot(h, wu_ref[...])
        act = (gt * jax.nn.sigmoid(gt)) * up
        acc[...] += _dot(act.astype(BF16), wd_ref[...])

    @pl.when(jnp.logical_and(active, f == pl.num_programs(1) - 1))
    def _():
        scatter = (lambda r: acc.at[pl.ds(r, 1), :], lambda r: y_hbm.at[pl.ds(jnp.maximum(dst_ref[0, r], 0), 1), :])
        real = lambda r: dst_ref[0, r] >= 0
        _row_copies(*scatter, tm, sem.at[1], start=True, valid=real)
        _row_copies(*scatter, tm, sem.at[1], start=False, valid=real)


def _moe_ffn(x, shift, scale, tile_exp, n_used, src, dst, wg, wu, wd, n_out_rows, tm, tf=1024):
    m, d = x.shape
    n_exp, _, ff = wg.shape
    n_tiles = src.shape[0]
    tf = min(tf, ff)
    assert ff % tf == 0
    nf = ff // tf
    vec = pl.BlockSpec((1, d), lambda i, f, te, nu: (0, 0))
    idx = pl.BlockSpec((None, 1, tm), lambda i, f, te, nu: (i, 0, 0), memory_space=pltpu.SMEM)

    def f_eff(i, f, nu):
        return jnp.where(i < nu[0], f, nf - 1)

    grid_spec = pltpu.PrefetchScalarGridSpec(
        num_scalar_prefetch=2,
        grid=(n_tiles, nf),
        in_specs=[
            idx, idx,
            pl.BlockSpec(memory_space=pl.ANY),
            vec, vec,
            pl.BlockSpec((None, d, tf), lambda i, f, te, nu: (te[i], 0, f_eff(i, f, nu))),
            pl.BlockSpec((None, d, tf), lambda i, f, te, nu: (te[i], 0, f_eff(i, f, nu))),
            pl.BlockSpec((None, tf, d), lambda i, f, te, nu: (te[i], f_eff(i, f, nu), 0)),
        ],
        out_specs=pl.BlockSpec(memory_space=pl.ANY),
        scratch_shapes=[pltpu.VMEM((tm, d), F32), pltpu.VMEM((tm, d), BF16), pltpu.VMEM((tm, d), F32),
                        pltpu.SemaphoreType.DMA((2,))],
    )
    return pl.pallas_call(
        _moe_ffn_kernel,
        grid_spec=grid_spec,
        out_shape=jax.ShapeDtypeStruct((n_out_rows, d), F32),
        compiler_params=_cparams(("arbitrary", "arbitrary")),
        name="moe_grouped_ffn",
    )(tile_exp, n_used, src, dst, x, shift, scale, wg, wu, wd)


def _moe_combine_kernel(x_ref, g_ref, info_ref, y1_ref, y2_ref, o_ref):
    info = info_ref[...]
    lane = lax.broadcasted_iota(jnp.int32, info.shape, 1)
    w1 = _lane_pick(info, lane, R_W1)
    w2 = _lane_pick(info, lane, R_W2)
    o_ref[...] = x_ref[...] + g_ref[...] * (w1 * y1_ref[...] + w2 * y2_ref[...])


def _moe_combine(x, gate, info, y, tm=512):
    m, d = x.shape
    nb = m // tm
    return pl.pallas_call(
        _moe_combine_kernel,
        grid=(nb,),
        in_specs=[
            pl.BlockSpec((tm, d), lambda i: (i, 0)),
            pl.BlockSpec((1, d), lambda i: (0, 0)),
            pl.BlockSpec((tm, LANES), lambda i: (i, 0)),
            pl.BlockSpec((tm, d), lambda i: (i, 0)),
            pl.BlockSpec((tm, d), lambda i: (nb + i, 0)),
        ],
        out_specs=pl.BlockSpec((tm, d), lambda i: (i, 0)),
        out_shape=jax.ShapeDtypeStruct((m, d), F32),
        compiler_params=_cparams(("parallel",)),
        name="moe_combine",
    )(x, gate, info, y, y)


def _moe_top2(x, shift, scale, gate, router, wg, wu, wd, tm=512):
    m, d = x.shape
    n_exp = router.shape[1]
    info, counts = _router(x, shift, scale, router)
    e1, e2 = info[:, R_E1].astype(jnp.int32), info[:, R_E2].astype(jnp.int32)
    r1, r2 = info[:, R_RANK1].astype(jnp.int32), info[:, R_RANK2].astype(jnp.int32)
    cnt = counts[0, :n_exp].astype(jnp.int32)
    padded = (cnt + tm - 1) // tm * tm
    ends = jnp.cumsum(padded)
    offs = ends - padded
    pos1, pos2 = offs[e1] + r1, offs[e2] + r2
    n_tiles = TOP_K * m // tm + n_exp
    n_rows = n_tiles * tm
    dst = jnp.full((n_rows,), -1, jnp.int32).at[jnp.concatenate([pos1, pos2])].set(
        jnp.arange(TOP_K * m, dtype=jnp.int32), unique_indices=True)
    src = jnp.where(dst >= 0, dst % m, 0)
    n_used = ends[-1] // tm
    tile_start = jnp.arange(n_tiles, dtype=jnp.int32) * tm
    tile_exp = jnp.searchsorted(ends, jnp.minimum(tile_start, ends[-1] - 1), side="right").astype(jnp.int32)
    y = _moe_ffn(x, shift, scale, jnp.minimum(tile_exp, n_exp - 1), n_used.reshape(1).astype(jnp.int32),
                 src.reshape(n_tiles, 1, tm), dst.reshape(n_tiles, 1, tm), wg, wu, wd, TOP_K * m, tm)
    return _moe_combine(x, gate, info, y)


def _rope_tables(rows):
    row = jnp.broadcast_to(jnp.arange(rows)[:, None], (rows, GRID_W)).reshape(-1).astype(F32)
    col = jnp.broadcast_to(jnp.arange(GRID_W)[None, :], (rows, GRID_W)).reshape(-1).astype(F32)
    n_freq = HEAD // 4
    inv = ROPE_THETA ** (-jnp.arange(n_freq, dtype=F32) / n_freq)
    ang = jnp.concatenate([row[:, None] * inv, col[:, None] * inv], axis=-1)
    cos, sin = jnp.cos(ang), jnp.sin(ang)
    return jnp.concatenate([cos, cos], axis=-1), jnp.concatenate([-sin, sin], axis=-1)


def _halves(a):
    shp = a.shape
    a = a.reshape(shp[:-1] + (shp[-1] // HEAD, HEAD // 2, 2))
    return jnp.swapaxes(a, -1, -2).reshape(shp)


def kernel(x, c, ctx, c_ctx, ada_w, ada_b, ev_w_in, ev_gate_b, ev_q_gain, ev_k_gain, ev_h_gain, ev_w_out,
           ev_ff_gate, ev_ff_up, ev_ff_down, od_w_in, od_q_gain, od_k_gain, od_lam_q1, od_lam_k1, od_lam_q2,
           od_lam_k2, od_sub_gain, od_w_out, od_router, od_ex_gate, od_ex_up, od_ex_down):
    bsz, s_len, d = x.shape
    n_ctx = ctx.shape[1]
    assert bsz == 1 and ada_w.shape[0] == 2
    a_heads = d // (2 * HEAD)
    a_kv = a_heads // GQA_GROUP
    b_heads = d // (2 * V_HEAD)
    c_heads = d // V_HEAD
    a_q, a_kvw, b_qk, b_v = a_heads * HEAD, a_kv * HEAD, b_heads * HEAD, b_heads * V_HEAD
    score_scale = HEAD ** -0.5 * math.log2(math.e)

    x_lat, x_ctx = x[0], ctx[0]
    cvecs = jnp.zeros((8, d), F32).at[0].set(c[0]).at[1].set(c_ctx)
    mods = _modulation(cvecs, ada_w, ada_b)

    def mod(layer, who):
        return [mods[layer, who, k * d:(k + 1) * d][None, :] for k in range(6)]

    cos_l, sin_l = _rope_tables(s_len // GRID_W)
    cos_c, sin_c = jnp.ones((n_ctx, HEAD), F32), jnp.zeros((n_ctx, HEAD), F32)

    w_in = ev_w_in[0]
    o0 = 0
    wqa = w_in[:, o0:o0 + a_q]; o0 += a_q
    wka = w_in[:, o0:o0 + a_kvw]; o0 += a_kvw
    wva = w_in[:, o0:o0 + a_kvw]; o0 += a_kvw
    wqb = w_in[:, o0:o0 + b_qk]; o0 += b_qk
    wkb = w_in[:, o0:o0 + b_qk]; o0 += b_qk
    wvb = w_in[:, o0:o0 + b_v]; o0 += b_v
    wob = w_in[:, o0:o0 + b_v]; o0 += b_v
    wgt = w_in[:, o0:]
    n_gates = wgt.shape[1]
    w0 = jnp.concatenate([_halves(wqa), wvb, wqb, wkb, _halves(wka), wva, wob], axis=1).astype(BF16)
    ones = lambda n: jnp.ones((n,), F32)
    cs0 = jnp.concatenate([
        jnp.tile(_halves(ev_q_gain[0]), a_heads) * score_scale, ones(b_v), ones(b_qk) * (HEAD ** -0.5), ones(b_qk),
        jnp.tile(_halves(ev_k_gain[0]), a_kv), ones(a_kvw), ones(b_v)])[None, :]
    modes0 = (["R"] * a_heads + ["P"] * (b_v // HEAD) + ["P"] * (2 * b_qk // HEAD) + ["R"] * a_kv
              + ["P"] * a_kv + ["P"] * (b_v // HEAD))
    nbf0 = a_q + b_v + 2 * b_qk + 2 * a_kvw
    wg0 = jnp.zeros((d, LANES), F32).at[:, :n_gates].set(wgt).astype(BF16)
    gb0 = jnp.zeros((1, LANES), F32).at[0, :n_gates].set(ev_gate_b[0])
    col_vb, col_qb, col_kb = a_q // b_v, (a_q + b_v) // b_qk, (a_q + b_v + b_qk) // b_qk
    col_ka = (a_q + b_v + 2 * b_qk) // HEAD
    col_va = col_ka + a_kv

    ml, mc = mod(0, 0), mod(0, 1)
    pl_b, pl_o, pl_g = _proj(x_lat, ml[0], ml[1], w0, cs0, cos_l, sin_l, modes0, nbf0, wg0, gb0)
    pc_b, pc_o, pc_g = _proj(x_ctx, mc[0], mc[1], w0, cs0, cos_c, sin_c, modes0, nbf0, wg0, gb0)
    p_all = jnp.concatenate([pl_b, pc_b], axis=0)
    g_all = jnp.concatenate([pl_g, pc_g], axis=0)
    g_row = g_all[:, :4 * b_heads].T

    ka_cols = slice(col_ka * HEAD, (col_ka + a_kv) * HEAD)
    att_l = _gqa_attention(pl_b, p_all[:, ka_cols].T, p_all, a_kv, col_va, tq=1024, tk=1280)
    att_c = _gqa_attention(pc_b, pc_b[:, ka_cols].T, pc_b, a_kv, col_va, tq=256, tk=256)
    hf, hb = _mlstm(p_all, g_all, g_row, b_heads, col_qb, col_kb, col_vb, s_len, n_ctx)

    hg = ev_h_gain[0].reshape(1, b_v)
    w_out0 = ev_w_out[0].astype(BF16)
    x_lat = _out_proj(x_lat, ml[2], w_out0, att_l, merge=(hf, hb, pl_o, hg, b_heads))
    x_ctx = _out_proj(x_ctx, mc[2], w_out0, att_c, merge=(hf, hb, pc_o, hg, b_heads), row_off=s_len)
    ffw = (ev_ff_gate[0].astype(BF16), ev_ff_up[0].astype(BF16), ev_ff_down[0].astype(BF16))
    x_lat = _ffn_dense(x_lat, ml[3], ml[4], ml[5], *ffw)
    x_ctx = _ffn_dense(x_ctx, mc[3], mc[4], mc[5], *ffw)

    c_qk = c_heads * 2 * HEAD
    w1 = od_w_in[0]
    w1 = jnp.concatenate([_halves(w1[:, :2 * c_qk]), w1[:, 2 * c_qk:]], axis=1).astype(BF16)
    cs1 = jnp.concatenate([jnp.tile(_halves(od_q_gain[0]), 2 * c_heads) * score_scale,
                           jnp.tile(_halves(od_k_gain[0]), 2 * c_heads), ones(c_heads * V_HEAD)])[None, :]
    modes1 = ["R"] * (4 * c_heads) + ["P"] * (c_heads * V_HEAD // HEAD)
    ml, mc = mod(1, 0), mod(1, 1)
    (ql,) = _proj(x_lat, ml[0], ml[1], w1, cs1, cos_l, sin_l, modes1, w1.shape[1])
    (qc,) = _proj(x_ctx, mc[0], mc[1], w1, cs1, cos_c, sin_c, modes1, w1.shape[1])
    kv_all = jnp.concatenate([ql, qc], axis=0)
    lam_init = 0.8 - 0.6 * math.exp(-0.3 * 1)
    vec = lambda a: a[0].reshape(1, -1)
    kt_all = kv_all[:, c_qk:2 * c_qk].T
    o_l = _diff_attention(ql, kt_all, kv_all, c_heads, lam_init, vec(od_lam_q1), vec(od_lam_k1), vec(od_lam_q2),
                          vec(od_lam_k2), vec(od_sub_gain), tq=2048, tk=1280)
    x_lat = _out_proj(x_lat, ml[2], od_w_out[0].astype(BF16), o_l)
    x_lat = _moe_top2(x_lat, ml[3], ml[4], ml[5], od_router[0], od_ex_gate[0].astype(BF16),
                      od_ex_up[0].astype(BF16), od_ex_down[0].astype(BF16))
    return x_lat[None]
```

```python
import functools
import math

import jax
import jax.numpy as jnp
from jax import lax
from jax.experimental import pallas as pl
from jax.experimental.pallas import tpu as pltpu

F32 = jnp.float32
BF16 = jnp.bfloat16

GRID_W = 64
ROPE_THETA = 10000.0
EPS = 1e-6
M_INIT = -1e30
HEAD = 128
V_HEAD = 256
GQA_GROUP = 4
TOP_K = 2
MLSTM_CHUNK = 256
GQA_SUB_ROWS = 256
DIFF_SUB_ROWS = 256
PIPE_SKEW = 1
SOFTMAX_ROW_GROUP = 64
LANES = 128
VMEM_LIMIT = 56 * 1024 * 1024
NEG_INF = float("-inf")


def _cparams(sem, flags=None):
    return pltpu.CompilerParams(dimension_semantics=sem, vmem_limit_bytes=VMEM_LIMIT, flags=flags)


ATTN_FLAGS = None


def _dot(a, b):
    return jnp.dot(a, b, preferred_element_type=F32)


def _dot_nt(a, b):
    return lax.dot_general(a, b, (((1,), (1,)), ((), ())), preferred_element_type=F32)


def _dot_tn(a, b):
    return lax.dot_general(a, b, (((0,), (0,)), ((), ())), preferred_element_type=F32)


def _rms(x):
    return x * lax.rsqrt(jnp.mean(x * x, axis=-1, keepdims=True) + EPS)


def _mod_kernel(c_ref, w_ref, b_ref, o_ref):
    c = c_ref[...]
    s = c * jax.nn.sigmoid(c)
    o_ref[...] = _dot(s.astype(BF16), w_ref[...].astype(BF16)) + b_ref[...]


def _modulation(cvecs, ada_w, ada_b):
    depth, d, n = ada_w.shape
    tn = 1024
    return pl.pallas_call(
        _mod_kernel,
        grid=(depth, n // tn),
        in_specs=[
            pl.BlockSpec((8, d), lambda l, j: (0, 0)),
            pl.BlockSpec((None, d, tn), lambda l, j: (l, 0, j)),
            pl.BlockSpec((None, 1, tn), lambda l, j: (l, 0, j)),
        ],
        out_specs=pl.BlockSpec((None, 8, tn), lambda l, j: (l, 0, j)),
        out_shape=jax.ShapeDtypeStruct((depth, 8, n), F32),
        compiler_params=_cparams(("parallel", "parallel")),
        name="adaln_mod",
    )(cvecs, ada_w, ada_b.reshape(depth, 1, n))


def _proj_kernel(patterns, n_bf_tiles, has_f32, has_gates, tn, *refs):
    it = iter(refs)
    x_ref, sh_ref, sc_ref, w_ref, cs_ref, cos_ref, sin_ref = (next(it) for _ in range(7))
    if has_gates:
        wg_ref, gb_ref = next(it), next(it)
    ob_ref = next(it)
    of_ref = next(it) if has_f32 else None
    og_ref = next(it) if has_gates else None
    h_scr = next(it)

    j = pl.program_id(1)

    @pl.when(j == 0)
    def _():
        h = _rms(x_ref[...]) * (1.0 + sc_ref[...]) + sh_ref[...]
        h_scr[...] = h.astype(BF16)
        if has_gates:
            og_ref[...] = _dot(h_scr[...], wg_ref[...]) + gb_ref[...]

    acc = _dot(h_scr[...], w_ref[...])
    cs = cs_ref[...]

    for lo, hi, pat in patterns:
        @pl.when(jnp.logical_and(j >= lo, j < hi))
        def _(pat=pat, lo=lo):
            to_f32 = has_f32 and lo >= n_bf_tiles
            out_ref = of_ref if to_f32 else ob_ref
            for hidx, mode in enumerate(pat):
                sl = slice(hidx * HEAD, (hidx + 1) * HEAD)
                y = acc[:, sl]
                if mode == "R":
                    y = _rms(y) * cs[:, sl]
                    y = y * cos_ref[...] + pltpu.roll(y, HEAD // 2, 1) * sin_ref[...]
                else:
                    y = y * cs[:, sl]
                out_ref[:, sl] = y.astype(out_ref.dtype)


def _proj(x, shift, scale, w, colscale, cos2, sin2, head_modes, n_bf_cols, wg=None, gb=None, tm=1024, tn=512):
    m, d = x.shape
    n = w.shape[1]
    tm = min(tm, m)
    assert m % tm == 0 and n % tn == 0 and n_bf_cols % tn == 0
    nj = n // tn
    n_bf_tiles = n_bf_cols // tn
    has_f32 = n_bf_cols < n
    has_gates = wg is not None
    hpt = tn // HEAD
    tiles = [tuple(head_modes[t * hpt:(t + 1) * hpt]) for t in range(nj)]
    patterns = []
    for t, pat in enumerate(tiles):
        if patterns and patterns[-1][2] == pat and not (has_f32 and t == n_bf_tiles):
            patterns[-1] = (patterns[-1][0], t + 1, pat)
        else:
            patterns.append((t, t + 1, pat))

    in_specs = [
        pl.BlockSpec((tm, d), lambda i, j: (i, 0)),
        pl.BlockSpec((1, d), lambda i, j: (0, 0)),
        pl.BlockSpec((1, d), lambda i, j: (0, 0)),
        pl.BlockSpec((d, tn), lambda i, j: (0, j)),
        pl.BlockSpec((1, tn), lambda i, j: (0, j)),
        pl.BlockSpec((tm, HEAD), lambda i, j: (i, 0)),
        pl.BlockSpec((tm, HEAD), lambda i, j: (i, 0)),
    ]
    args = [x, shift, scale, w, colscale, cos2, sin2]
    if has_gates:
        in_specs += [pl.BlockSpec((d, LANES), lambda i, j: (0, 0)), pl.BlockSpec((1, LANES), lambda i, j: (0, 0))]
        args += [wg, gb]
    last_bf = n_bf_tiles - 1
    out_specs = [pl.BlockSpec((tm, tn), lambda i, j: (i, jnp.minimum(j, last_bf)))]
    out_shape = [jax.ShapeDtypeStruct((m, n_bf_cols), BF16)]
    if has_f32:
        out_specs.append(pl.BlockSpec((tm, tn), lambda i, j: (i, jnp.maximum(j - n_bf_tiles, 0))))
        out_shape.append(jax.ShapeDtypeStruct((m, n - n_bf_cols), F32))
    if has_gates:
        out_specs.append(pl.BlockSpec((tm, LANES), lambda i, j: (i, 0)))
        out_shape.append(jax.ShapeDtypeStruct((m, LANES), F32))
    return pl.pallas_call(
        functools.partial(_proj_kernel, tuple(patterns), n_bf_tiles, has_f32, has_gates, tn),
        grid=(m // tm, nj),
        in_specs=in_specs,
        out_specs=out_specs,
        out_shape=out_shape,
        scratch_shapes=[pltpu.VMEM((tm, d), BF16)],
        compiler_params=_cparams(("parallel", "arbitrary")),
        name="norm_mod_proj",
    )(*args)


def _softmax_pipeline(score_fns, v, row_slices, m_scr, l_scr, acc_scr, s_scr, p_scr):
    n = len(score_fns)
    alphas = {}
    k = PIPE_SKEW
    for t in range(n + 2 * k):
        if t < n:
            s_scr[t] = score_fns[t]()
        c = t - k
        if 0 <= c < n:
            alphas[c] = _softmax_probs(row_slices[c], m_scr, l_scr, s_scr.at[c], p_scr.at[c])
        c = t - 2 * k
        if 0 <= c < n:
            rows = row_slices[c]
            acc_scr[rows, :] = alphas.pop(c) * acc_scr[rows, :] + _dot(p_scr[c], v)


def _softmax_probs(rows, m_scr, l_scr, s_buf, p_buf):
    sub, tk = s_buf.shape
    n_slab = tk // LANES
    mx = s_buf[:, 0:LANES]
    for c in range(1, n_slab):
        mx = jnp.maximum(mx, s_buf[:, c * LANES:(c + 1) * LANES])
    m_prev = m_scr[rows, :]
    m_new = jnp.maximum(m_prev, jnp.max(mx, axis=1, keepdims=True))
    alpha = jnp.exp2(m_prev - m_new)
    m_scr[rows, :] = m_new
    rg = min(sub, SOFTMAX_ROW_GROUP)
    for g in range(sub // rg):
        gr = slice(g * rg, (g + 1) * rg)
        mb = jnp.broadcast_to(m_new[gr, :], (rg, LANES))
        lsum = jnp.zeros((rg, LANES), F32)
        for c in range(n_slab):
            cs = slice(c * LANES, (c + 1) * LANES)
            p = jnp.exp2(s_buf[gr, cs] - mb)
            if l_scr is not None:
                lsum = lsum + p
            p_buf[gr, cs] = p.astype(BF16)
        if l_scr is not None:
            lr = slice(rows.start + g * rg, rows.start + (g + 1) * rg)
            l_scr[lr, :] = alpha[gr, :] * l_scr[lr, :] + lsum
    return alpha


def _gqa_kernel(tq, q_ref, kt_ref, v_ref, o_ref, m_scr, acc_scr, s_scr, p_scr):
    kj = pl.program_id(2)

    @pl.when(kj == 0)
    def _():
        m_scr[...] = jnp.full(m_scr.shape, NEG_INF, F32)
        acc_scr[...] = jnp.zeros(acc_scr.shape, F32)

    kt = kt_ref[...]
    v = jnp.concatenate([v_ref[...], jnp.ones(v_ref.shape, BF16)], axis=1)
    sub = s_scr.shape[1]
    nb = tq // sub
    fns, rows = [], []
    for g in range(GQA_GROUP):
        for b in range(nb):
            fns.append(lambda g=g, b=b: _dot(q_ref[b * sub:(b + 1) * sub, g * HEAD:(g + 1) * HEAD], kt))
            rows.append(slice(g * tq + b * sub, g * tq + (b + 1) * sub))
    _softmax_pipeline(fns, v, rows, m_scr, None, acc_scr, s_scr, p_scr)

    @pl.when(kj == pl.num_programs(2) - 1)
    def _():
        out = acc_scr[:, :HEAD] / acc_scr[:, HEAD:HEAD + 1]
        for g in range(GQA_GROUP):
            o_ref[:, g * HEAD:(g + 1) * HEAD] = out[g * tq:(g + 1) * tq, :].astype(o_ref.dtype)


def _gqa_attention(q_arr, kt_arr, kv_arr, n_kv, v_col, tq, tk):
    s_len = q_arr.shape[0]
    lk = kv_arr.shape[0]
    tq = min(tq, s_len)
    tk = min(tk, lk)
    assert s_len % tq == 0 and lk % tk == 0 and tk % LANES == 0
    gw = GQA_GROUP * HEAD
    sub = min(tq, GQA_SUB_ROWS)
    n_chain = GQA_GROUP * (tq // sub)
    return pl.pallas_call(
        functools.partial(_gqa_kernel, tq),
        grid=(n_kv, s_len // tq, lk // tk),
        in_specs=[
            pl.BlockSpec((tq, gw), lambda n, i, j: (i, n)),
            pl.BlockSpec((HEAD, tk), lambda n, i, j: (n, j)),
            pl.BlockSpec((tk, HEAD), lambda n, i, j: (j, v_col + n)),
        ],
        out_specs=pl.BlockSpec((tq, gw), lambda n, i, j: (i, n)),
        out_shape=jax.ShapeDtypeStruct((s_len, n_kv * gw), BF16),
        scratch_shapes=[
            pltpu.VMEM((GQA_GROUP * tq, 1), F32),
            pltpu.VMEM((GQA_GROUP * tq, 2 * HEAD), F32),
            pltpu.VMEM((n_chain, sub, tk), F32),
            pltpu.VMEM((n_chain, sub, tk), BF16),
        ],
        compiler_params=_cparams(("parallel", "parallel", "arbitrary"), ATTN_FLAGS),
        name="gqa_flash",
    )(q_arr, kt_arr, kv_arr)


def _diff_kernel(tq, lam_init, q_ref, kt_ref, v_ref, lq1_ref, lk1_ref, lq2_ref, lk2_ref, sg_ref, o_ref,
                 m_scr, l_scr, acc_scr, s_scr, p_scr):
    kj = pl.program_id(2)

    @pl.when(kj == 0)
    def _():
        m_scr[...] = jnp.full(m_scr.shape, NEG_INF, F32)
        l_scr[...] = jnp.zeros(l_scr.shape, F32)
        acc_scr[...] = jnp.zeros(acc_scr.shape, F32)

    v = v_ref[...]
    sub = s_scr.shape[1]
    nb = tq // sub
    fns, rows = [], []
    for b in range(nb):
        for r in range(2):
            fns.append(lambda r=r, b=b: _dot(q_ref[b * sub:(b + 1) * sub, r * HEAD:(r + 1) * HEAD],
                                             kt_ref[r * HEAD:(r + 1) * HEAD, :]))
            rows.append(slice(r * tq + b * sub, r * tq + (b + 1) * sub))
    _softmax_pipeline(fns, v, rows, m_scr, l_scr, acc_scr, s_scr, p_scr)

    @pl.when(kj == pl.num_programs(2) - 1)
    def _():
        lam = (jnp.exp(jnp.sum(lq1_ref[...] * lk1_ref[...], axis=1, keepdims=True))
               - jnp.exp(jnp.sum(lq2_ref[...] * lk2_ref[...], axis=1, keepdims=True)) + lam_init)
        o = acc_scr[...] / jnp.sum(l_scr[...], axis=1, keepdims=True)
        o = o[:tq, :] - lam * o[tq:, :]
        o = _rms(o) * sg_ref[...] * (1.0 - lam_init)
        o_ref[...] = o.astype(o_ref.dtype)


def _diff_attention(q_arr, kt_arr, kv_arr, n_heads, lam_init, lq1, lk1, lq2, lk2, sub_gain, tq, tk):
    s_len = q_arr.shape[0]
    lk = kv_arr.shape[0]
    tq = min(tq, s_len)
    tk = min(tk, lk)
    assert s_len % tq == 0 and lk % tk == 0 and tk % LANES == 0
    sub = min(tq, DIFF_SUB_ROWS)
    n_chain = 2 * (tq // sub)
    vec = pl.BlockSpec((1, HEAD), lambda h, i, j: (0, 0))
    return pl.pallas_call(
        functools.partial(_diff_kernel, tq, lam_init),
        grid=(n_heads, s_len // tq, lk // tk),
        in_specs=[
            pl.BlockSpec((tq, V_HEAD), lambda h, i, j: (i, h)),
            pl.BlockSpec((V_HEAD, tk), lambda h, i, j: (h, j)),
            pl.BlockSpec((tk, V_HEAD), lambda h, i, j: (j, 2 * n_heads + h)),
            vec, vec, vec, vec,
            pl.BlockSpec((1, V_HEAD), lambda h, i, j: (0, 0)),
        ],
        out_specs=pl.BlockSpec((tq, V_HEAD), lambda h, i, j: (i, h)),
        out_shape=jax.ShapeDtypeStruct((s_len, n_heads * V_HEAD), BF16),
        scratch_shapes=[
            pltpu.VMEM((2 * tq, 1), F32),
            pltpu.VMEM((2 * tq, LANES), F32),
            pltpu.VMEM((2 * tq, V_HEAD), F32),
            pltpu.VMEM((n_chain, sub, tk), F32),
            pltpu.VMEM((n_chain, sub, tk), BF16),
        ],
        compiler_params=_cparams(("parallel", "parallel", "arbitrary"), ATTN_FLAGS),
        name="diff_flash",
    )(q_arr, kt_arr, kv_arr, lq1, lk1, lq2, lk2, sub_gain)


def _split3(x):
    hi = x.astype(BF16)
    r1 = x - hi.astype(F32)
    mid = r1.astype(BF16)
    lo = (r1 - mid.astype(F32)).astype(BF16)
    return hi, mid, lo


def _log_sigmoid(x):
    return jnp.minimum(x, 0.0) - jnp.log(1.0 + jnp.exp(-jnp.abs(x)))


def _mlstm_kernel(nh, chunk, qf_ref, kf_ref, vf_ref, gcf_ref, grf_ref, qb_ref, kb_ref, vb_ref, gcb_ref, grb_ref,
                  hf_ref, hb_ref, c_scr, n_scr, m_scr):
    t = pl.program_id(0)

    @pl.when(t == 0)
    def _():
        c_scr[...] = jnp.zeros(c_scr.shape, F32)
        n_scr[...] = jnp.zeros(n_scr.shape, F32)
        m_scr[...] = jnp.full(m_scr.shape, M_INIT, F32)

    row = lax.broadcasted_iota(jnp.int32, (chunk, chunk), 0)
    col = lax.broadcasted_iota(jnp.int32, (chunk, chunk), 1)
    lower = col <= row
    upper = col >= row
    lower_b = lower.astype(BF16)
    upper_b = upper.astype(BF16)

    for rev in (False, True):
        q_ref, k_ref, v_ref, gc_ref, gr_ref, h_ref = (
            (qb_ref, kb_ref, vb_ref, gcb_ref, grb_ref, hb_ref) if rev else
            (qf_ref, kf_ref, vf_ref, gcf_ref, grf_ref, hf_ref))
        mask = upper if rev else lower
        gcol = gc_ref[...]
        grow = gr_ref[...]
        cum_col = sum(_dot(upper_b if rev else lower_b, piece) for piece in _split3(_log_sigmoid(gcol)))
        cum_row = sum(_dot(piece, lower_b if rev else upper_b) for piece in _split3(_log_sigmoid(grow)))
        for hd in range(nh):
            idx = (1 if rev else 0) * nh + hd
            ki = (2 if rev else 0) * nh + hd
            kf = ki + nh
            qc = q_ref[:, hd * HEAD:(hd + 1) * HEAD]
            kc = k_ref[:, hd * HEAD:(hd + 1) * HEAD]
            vc = v_ref[:, hd * V_HEAD:(hd + 1) * V_HEAD]
            i_col = gcol[:, ki:ki + 1]
            i_row = grow[ki:ki + 1, :]
            b_col = cum_col[:, kf:kf + 1]
            b_row = cum_row[kf:kf + 1, :]
            c_mem = c_scr[idx]
            n_mem = n_scr[idx]
            m = m_scr[idx][:, :1]

            dlog = jnp.where(mask, b_col + (i_row - b_row), NEG_INF)
            inter = b_col + m
            m_out = jnp.maximum(inter, jnp.max(dlog, axis=1, keepdims=True))
            w_intra = jnp.exp(dlog - m_out)
            w_inter = jnp.exp(inter - m_out)
            qk = _dot_nt(qc, kc) * w_intra
            num = w_inter * _dot(qc, c_mem.astype(BF16)) + _dot(qk.astype(BF16), vc)
            den = (w_inter * jnp.sum(qc.astype(F32) * n_mem, axis=1, keepdims=True)
                   + jnp.sum(qk, axis=1, keepdims=True))
            h = num / jnp.maximum(jnp.abs(den), jnp.exp(-m_out))
            h_ref[:, hd * V_HEAD:(hd + 1) * V_HEAD] = h

            b_last = b_col[0:1, :] if rev else b_col[chunk - 1:chunk, :]
            g_col = b_last - b_col + i_col
            m_new = jnp.maximum(b_last + m, jnp.max(g_col, axis=0, keepdims=True))
            a_dec = jnp.exp(b_last + m - m_new)
            kw = kc.astype(F32) * jnp.exp(g_col - m_new)
            c_scr[idx] = a_dec * c_mem + _dot_tn(kw.astype(BF16), vc)
            n_scr[idx] = a_dec * n_mem + jnp.sum(kw, axis=0, keepdims=True)
            m_scr[idx] = jnp.broadcast_to(m_new, (1, LANES))


def _mlstm(qkv, gates_col, gates_row, nh, q_col, k_col, v_col, n_lat, n_ctx):
    t_len = qkv.shape[0]
    chunk = MLSTM_CHUNK
    assert n_lat % chunk == 0 and n_ctx % chunk == 0 and t_len == n_lat + n_ctx
    nlat, nctx = n_lat // chunk, n_ctx // chunk
    nc = nlat + nctx

    def fwd(t):
        return jnp.where(t < nctx, nlat + t, t - nctx)

    def bwd(t):
        return jnp.where(t < nctx, nc - 1 - t, nlat - 1 - (t - nctx))

    qw, vw = nh * HEAD, nh * V_HEAD

    def specs(order):
        return [
            pl.BlockSpec((chunk, qw), lambda t: (order(t), q_col)),
            pl.BlockSpec((chunk, qw), lambda t: (order(t), k_col)),
            pl.BlockSpec((chunk, vw), lambda t: (order(t), v_col)),
            pl.BlockSpec((chunk, LANES), lambda t: (order(t), 0)),
            pl.BlockSpec((4 * nh, chunk), lambda t: (0, order(t))),
        ]

    return pl.pallas_call(
        functools.partial(_mlstm_kernel, nh, chunk),
        grid=(nc,),
        in_specs=specs(fwd) + specs(bwd),
        out_specs=[pl.BlockSpec((chunk, vw), lambda t: (fwd(t), 0)),
                   pl.BlockSpec((chunk, vw), lambda t: (bwd(t), 0))],
        out_shape=[jax.ShapeDtypeStruct((t_len, vw), F32)] * 2,
        scratch_shapes=[
            pltpu.VMEM((2 * nh, HEAD, V_HEAD), F32),
            pltpu.VMEM((2 * nh, 1, HEAD), F32),
            pltpu.VMEM((2 * nh, 1, LANES), F32),
        ],
        compiler_params=_cparams(("arbitrary",)),
        name="mlstm_scan",
    )(qkv, qkv, qkv, gates_col, gates_row, qkv, qkv, qkv, gates_col, gates_row)


def _out_kernel(merge, nh, *refs):
    if merge:
        a_ref, hf_ref, hb_ref, o_ref, hg_ref, w_ref, x_ref, g_ref, y_ref, mix_scr = refs
        aw = a_ref.shape[1]

        @pl.when(pl.program_id(1) == 0)
        def _():
            mix_scr[:, :aw] = a_ref[...]
            for hd in range(nh):
                sl = slice(hd * V_HEAD, (hd + 1) * V_HEAD)
                hm = _rms(hf_ref[:, sl] + hb_ref[:, sl]) * hg_ref[:, sl] * jax.nn.sigmoid(o_ref[:, sl])
                mix_scr[:, aw + hd * V_HEAD:aw + (hd + 1) * V_HEAD] = hm.astype(BF16)

        lhs = mix_scr[...]
    else:
        a_ref, w_ref, x_ref, g_ref, y_ref = refs
        lhs = a_ref[...]
    y_ref[...] = x_ref[...] + g_ref[...] * _dot(lhs, w_ref[...])


def _out_proj(x, gate, w, a, merge=None, row_off=0, tm=512, tn=1024):
    m, d = x.shape
    kdim, n = w.shape
    tm = min(tm, m)
    assert m % tm == 0 and n % tn == 0 and row_off % tm == 0
    off = row_off // tm
    in_specs = [pl.BlockSpec((tm, a.shape[1]), lambda i, j: (i, 0))]
    args = [a]
    scratch = []
    nh = 0
    if merge is not None:
        hf, hb, o, hg, nh = merge
        hw = hf.shape[1]
        in_specs += [pl.BlockSpec((tm, hw), lambda i, j: (i + off, 0)),
                     pl.BlockSpec((tm, hw), lambda i, j: (i + off, 0)),
                     pl.BlockSpec((tm, hw), lambda i, j: (i, 0)),
                     pl.BlockSpec((1, hw), lambda i, j: (0, 0))]
        args += [hf, hb, o, hg]
        scratch = [pltpu.VMEM((tm, kdim), BF16)]
    in_specs += [pl.BlockSpec((kdim, tn), lambda i, j: (0, j)),
                 pl.BlockSpec((tm, tn), lambda i, j: (i, j)),
                 pl.BlockSpec((1, tn), lambda i, j: (0, j))]
    args += [w, x, gate]
    return pl.pallas_call(
        functools.partial(_out_kernel, merge is not None, nh),
        grid=(m // tm, n // tn),
        in_specs=in_specs,
        out_specs=pl.BlockSpec((tm, tn), lambda i, j: (i, j)),
        out_shape=jax.ShapeDtypeStruct((m, n), F32),
        scratch_shapes=scratch,
        compiler_params=_cparams(("parallel", "arbitrary")),
        name="out_proj_residual",
    )(*args)


def _ffn_kernel(x_ref, sh_ref, sc_ref, g_ref, wg_ref, wu_ref, wd_ref, y_ref, h_scr, acc_scr):
    f = pl.program_id(1)

    @pl.when(f == 0)
    def _():
        h = _rms(x_ref[...]) * (1.0 + sc_ref[...]) + sh_ref[...]
        h_scr[...] = h.astype(BF16)
        acc_scr[...] = jnp.zeros(acc_scr.shape, F32)

    h = h_scr[...]
    gt = _dot(h, wg_ref[...])
    up = _dot(h, wu_ref[...])
    act = (gt * jax.nn.sigmoid(gt)) * up
    acc_scr[...] += _dot(act.astype(BF16), wd_ref[...])

    @pl.when(f == pl.num_programs(1) - 1)
    def _():
        y_ref[...] = x_ref[...] + g_ref[...] * acc_scr[...]


def _ffn_dense(x, shift, scale, gate, wg, wu, wd, tm=512, tf=512):
    m, d = x.shape
    ff = wg.shape[1]
    tm = min(tm, m)
    assert m % tm == 0 and ff % tf == 0
    vec = pl.BlockSpec((1, d), lambda i, f: (0, 0))
    return pl.pallas_call(
        _ffn_kernel,
        grid=(m // tm, ff // tf),
        in_specs=[
            pl.BlockSpec((tm, d), lambda i, f: (i, 0)), vec, vec, vec,
            pl.BlockSpec((d, tf), lambda i, f: (0, f)),
            pl.BlockSpec((d, tf), lambda i, f: (0, f)),
            pl.BlockSpec((tf, d), lambda i, f: (f, 0)),
        ],
        out_specs=pl.BlockSpec((tm, d), lambda i, f: (i, 0)),
        out_shape=jax.ShapeDtypeStruct((m, d), F32),
        scratch_shapes=[pltpu.VMEM((tm, d), BF16), pltpu.VMEM((tm, d), F32)],
        compiler_params=_cparams(("parallel", "arbitrary")),
        name="ffn_dense",
    )(x, shift, scale, gate, wg, wu, wd)


R_E1, R_E2, R_W1, R_W2, R_RANK1, R_RANK2 = range(6)


def _lane_pick(a, lane, k):
    return jnp.sum(jnp.where(lane == k, a, 0.0), axis=1, keepdims=True)


def _router_kernel(n_exp, x_ref, sh_ref, sc_ref, r_ref, info_ref, cnt_ref, carry_scr):
    @pl.when(pl.program_id(0) == 0)
    def _():
        carry_scr[...] = jnp.zeros(carry_scr.shape, F32)

    tm = x_ref.shape[0]
    h = _rms(x_ref[...]) * (1.0 + sc_ref[...]) + sh_ref[...]
    logits = jnp.dot(h, r_ref[...], preferred_element_type=F32, precision=lax.Precision.HIGHEST)
    lane = lax.broadcasted_iota(jnp.int32, logits.shape, 1)
    logits = jnp.where(lane < n_exp, logits, NEG_INF)
    v1 = jnp.max(logits, axis=1, keepdims=True)
    i1 = jnp.min(jnp.where(logits == v1, lane, LANES), axis=1, keepdims=True)
    rest = jnp.where(lane == i1, NEG_INF, logits)
    v2 = jnp.max(rest, axis=1, keepdims=True)
    i2 = jnp.min(jnp.where(rest == v2, lane, LANES), axis=1, keepdims=True)
    e2 = jnp.exp(v2 - v1)
    w1 = 1.0 / (1.0 + e2)
    w2 = e2 / (1.0 + e2)

    chosen = jnp.where(jnp.logical_or(lane == i1, lane == i2), 1.0, 0.0)
    r_i = lax.broadcasted_iota(jnp.int32, (tm, tm), 0)
    c_i = lax.broadcasted_iota(jnp.int32, (tm, tm), 1)
    before = (c_i < r_i).astype(BF16)
    rank = _dot(before, chosen.astype(BF16)) + carry_scr[...]
    carry_scr[...] += jnp.sum(chosen, axis=0, keepdims=True)
    cnt_ref[...] = carry_scr[...]
    rank1 = jnp.sum(jnp.where(lane == i1, rank, 0.0), axis=1, keepdims=True)
    rank2 = jnp.sum(jnp.where(lane == i2, rank, 0.0), axis=1, keepdims=True)
    info = jnp.zeros(logits.shape, F32)
    for k, val in ((R_E1, i1.astype(F32)), (R_E2, i2.astype(F32)), (R_W1, w1), (R_W2, w2),
                   (R_RANK1, rank1), (R_RANK2, rank2)):
        info = jnp.where(lane == k, val, info)
    info_ref[...] = info


def _router(x, shift, scale, router, tm=512):
    m, d = x.shape
    n_exp = router.shape[1]
    vec = pl.BlockSpec((1, d), lambda i: (0, 0))
    router_pad = jnp.zeros((d, LANES), F32).at[:, :n_exp].set(router)
    return pl.pallas_call(
        functools.partial(_router_kernel, n_exp),
        grid=(m // tm,),
        in_specs=[pl.BlockSpec((tm, d), lambda i: (i, 0)), vec, vec, pl.BlockSpec((d, LANES), lambda i: (0, 0))],
        out_specs=[pl.BlockSpec((tm, LANES), lambda i: (i, 0)), pl.BlockSpec((1, LANES), lambda i: (0, 0))],
        out_shape=[jax.ShapeDtypeStruct((m, LANES), F32), jax.ShapeDtypeStruct((1, LANES), F32)],
        scratch_shapes=[pltpu.VMEM((1, LANES), F32)],
        compiler_params=_cparams(("arbitrary",)),
        name="moe_router",
    )(x, shift, scale, router_pad)


def _row_copies(src_at, dst_at, n_rows, sem, start):
    def body(r, carry):
        cp = pltpu.make_async_copy(src_at(r), dst_at(r), sem)
        if start:
            cp.start()
        else:
            cp.wait()
        return carry
    lax.fori_loop(0, n_rows, body, 0)


def _moe_ffn_kernel(te_ref, nu_ref, nv_ref, src_ref, nsrc_ref, dst_ref, x_hbm, sh_ref, sc_ref, wg_ref, wu_ref, wd_ref,
                    y_hbm, xbuf, hbuf, acc, stage, sem):
    i = pl.program_id(0)
    f = pl.program_id(1)
    tm = xbuf.shape[0]
    n_used = nu_ref[0]
    active = i < n_used
    g_sem, s_sem = sem.at[0], sem.at[1]
    row0 = lambda ref: (lambda r: ref.at[pl.ds(0, 1), :])

    def gather(idx_ref, start):
        _row_copies(lambda r: x_hbm.at[pl.ds(idx_ref[0, r], 1), :] if start else x_hbm.at[pl.ds(0, 1), :],
                    lambda r: xbuf.at[pl.ds(r, 1), :], tm, g_sem, start)

    @pl.when(jnp.logical_and(active, f == 0))
    def _():
        pl.when(i == 0)(lambda: gather(src_ref, True))
        gather(src_ref, False)
        h = _rms(xbuf[...]) * (1.0 + sc_ref[...]) + sh_ref[...]
        hbuf[...] = h.astype(BF16)
        acc[...] = jnp.zeros(acc.shape, F32)
        pl.when(i + 1 < n_used)(lambda: gather(nsrc_ref, True))

    @pl.when(active)
    def _():
        h = hbuf[...]
        gt = _dot(h, wg_ref[...])
        up = _dot(h, wu_ref[...])
        act = (gt * jax.nn.sigmoid(gt)) * up
        acc[...] += _dot(act.astype(BF16), wd_ref[...])

    @pl.when(jnp.logical_and(active, f == pl.num_programs(1) - 1))
    def _():
        wait_scatter = lambda n: _row_copies(row0(stage), row0(y_hbm), n, s_sem, False)
        pl.when(i > 0)(lambda: wait_scatter(nv_ref[jnp.maximum(i - 1, 0)]))
        stage[...] = acc[...]
        _row_copies(lambda r: stage.at[pl.ds(r, 1), :], lambda r: y_hbm.at[pl.ds(dst_ref[0, r], 1), :],
                    nv_ref[i], s_sem, True)
        pl.when(i == n_used - 1)(lambda: wait_scatter(nv_ref[i]))


def _moe_ffn(x, shift, scale, tile_exp, n_used, n_valid, src, dst, wg, wu, wd, n_out_rows, tm, tf=1024):
    m, d = x.shape
    n_exp, _, ff = wg.shape
    n_tiles = src.shape[0]
    tf = min(tf, ff)
    assert ff % tf == 0
    nf = ff // tf
    vec = pl.BlockSpec((1, d), lambda i, f, te, nu, nv: (0, 0))
    idx = pl.BlockSpec((None, 1, tm), lambda i, f, te, nu, nv: (i, 0, 0), memory_space=pltpu.SMEM)
    idx_next = pl.BlockSpec((None, 1, tm), lambda i, f, te, nu, nv: (jnp.minimum(i + 1, n_tiles - 1), 0, 0),
                            memory_space=pltpu.SMEM)

    def f_eff(i, f, nu):
        return jnp.where(i < nu[0], f, nf - 1)

    grid_spec = pltpu.PrefetchScalarGridSpec(
        num_scalar_prefetch=3,
        grid=(n_tiles, nf),
        in_specs=[
            idx, idx_next, idx,
            pl.BlockSpec(memory_space=pl.ANY),
            vec, vec,
            pl.BlockSpec((None, d, tf), lambda i, f, te, nu, nv: (te[i], 0, f_eff(i, f, nu))),
            pl.BlockSpec((None, d, tf), lambda i, f, te, nu, nv: (te[i], 0, f_eff(i, f, nu))),
            pl.BlockSpec((None, tf, d), lambda i, f, te, nu, nv: (te[i], f_eff(i, f, nu), 0)),
        ],
        out_specs=pl.BlockSpec(memory_space=pl.ANY),
        scratch_shapes=[pltpu.VMEM((tm, d), F32), pltpu.VMEM((tm, d), BF16), pltpu.VMEM((tm, d), F32),
                        pltpu.VMEM((tm, d), F32), pltpu.SemaphoreType.DMA((2,))],
    )
    return pl.pallas_call(
        _moe_ffn_kernel,
        grid_spec=grid_spec,
        out_shape=jax.ShapeDtypeStruct((n_out_rows, d), F32),
        compiler_params=_cparams(("arbitrary", "arbitrary")),
        name="moe_grouped_ffn",
    )(tile_exp, n_used, n_valid, src, src, dst, x, shift, scale, wg, wu, wd)


def _moe_combine_kernel(x_ref, g_ref, info_ref, y1_ref, y2_ref, o_ref):
    info = info_ref[...]
    lane = lax.broadcasted_iota(jnp.int32, info.shape, 1)
    w1 = _lane_pick(info, lane, R_W1)
    w2 = _lane_pick(info, lane, R_W2)
    o_ref[...] = x_ref[...] + g_ref[...] * (w1 * y1_ref[...] + w2 * y2_ref[...])


def _moe_combine(x, gate, info, y, tm=512):
    m, d = x.shape
    nb = m // tm
    return pl.pallas_call(
        _moe_combine_kernel,
        grid=(nb,),
        in_specs=[
            pl.BlockSpec((tm, d), lambda i: (i, 0)),
            pl.BlockSpec((1, d), lambda i: (0, 0)),
            pl.BlockSpec((tm, LANES), lambda i: (i, 0)),
            pl.BlockSpec((tm, d), lambda i: (i, 0)),
            pl.BlockSpec((tm, d), lambda i: (nb + i, 0)),
        ],
        out_specs=pl.BlockSpec((tm, d), lambda i: (i, 0)),
        out_shape=jax.ShapeDtypeStruct((m, d), F32),
        compiler_params=_cparams(("parallel",)),
        name="moe_combine",
    )(x, gate, info, y, y)


def _moe_top2(x, shift, scale, gate, router, wg, wu, wd, tm=512):
    m, d = x.shape
    n_exp = router.shape[1]
    info, counts = _router(x, shift, scale, router)
    e1, e2 = info[:, R_E1].astype(jnp.int32), info[:, R_E2].astype(jnp.int32)
    r1, r2 = info[:, R_RANK1].astype(jnp.int32), info[:, R_RANK2].astype(jnp.int32)
    cnt = counts[0, :n_exp].astype(jnp.int32)
    padded = (cnt + tm - 1) // tm * tm
    ends = jnp.cumsum(padded)
    offs = ends - padded
    pos1, pos2 = offs[e1] + r1, offs[e2] + r2
    n_tiles = TOP_K * m // tm + n_exp
    n_rows = n_tiles * tm
    dst = jnp.full((n_rows,), -1, jnp.int32).at[jnp.concatenate([pos1, pos2])].set(
        jnp.arange(TOP_K * m, dtype=jnp.int32), unique_indices=True)
    src = jnp.where(dst >= 0, dst % m, 0)
    n_used = ends[-1] // tm
    tile_start = jnp.arange(n_tiles, dtype=jnp.int32) * tm
    tile_exp = jnp.searchsorted(ends, jnp.minimum(tile_start, ends[-1] - 1), side="right").astype(jnp.int32)
    tile_exp = jnp.minimum(tile_exp, n_exp - 1)
    n_valid = jnp.where(tile_start < ends[-1], jnp.clip(offs[tile_exp] + cnt[tile_exp] - tile_start, 0, tm), 0)
    y = _moe_ffn(x, shift, scale, tile_exp, n_used.reshape(1).astype(jnp.int32), n_valid.astype(jnp.int32),
                 src.reshape(n_tiles, 1, tm), dst.reshape(n_tiles, 1, tm), wg, wu, wd, TOP_K * m, tm)
    return _moe_combine(x, gate, info, y)


def _rope_tables(rows):
    row = jnp.broadcast_to(jnp.arange(rows)[:, None], (rows, GRID_W)).reshape(-1).astype(F32)
    col = jnp.broadcast_to(jnp.arange(GRID_W)[None, :], (rows, GRID_W)).reshape(-1).astype(F32)
    n_freq = HEAD // 4
    inv = ROPE_THETA ** (-jnp.arange(n_freq, dtype=F32) / n_freq)
    ang = jnp.concatenate([row[:, None] * inv, col[:, None] * inv], axis=-1)
    cos, sin = jnp.cos(ang), jnp.sin(ang)
    return jnp.concatenate([cos, cos], axis=-1), jnp.concatenate([-sin, sin], axis=-1)


def _halves(a):
    shp = a.shape
    a = a.reshape(shp[:-1] + (shp[-1] // HEAD, HEAD // 2, 2))
    return jnp.swapaxes(a, -1, -2).reshape(shp)


def kernel(x, c, ctx, c_ctx, ada_w, ada_b, ev_w_in, ev_gate_b, ev_q_gain, ev_k_gain, ev_h_gain, ev_w_out,
           ev_ff_gate, ev_ff_up, ev_ff_down, od_w_in, od_q_gain, od_k_gain, od_lam_q1, od_lam_k1, od_lam_q2,
           od_lam_k2, od_sub_gain, od_w_out, od_router, od_ex_gate, od_ex_up, od_ex_down):
    bsz, s_len, d = x.shape
    n_ctx = ctx.shape[1]
    assert bsz == 1 and ada_w.shape[0] == 2
    a_heads = d // (2 * HEAD)
    a_kv = a_heads // GQA_GROUP
    b_heads = d // (2 * V_HEAD)
    c_heads = d // V_HEAD
    a_q, a_kvw, b_qk, b_v = a_heads * HEAD, a_kv * HEAD, b_heads * HEAD, b_heads * V_HEAD
    score_scale = HEAD ** -0.5 * math.log2(math.e)

    x_lat, x_ctx = x[0], ctx[0]
    cvecs = jnp.zeros((8, d), F32).at[0].set(c[0]).at[1].set(c_ctx)
    mods = _modulation(cvecs, ada_w, ada_b)

    def mod(layer, who):
        return [mods[layer, who, k * d:(k + 1) * d][None, :] for k in range(6)]

    cos_l, sin_l = _rope_tables(s_len // GRID_W)
    cos_c, sin_c = jnp.ones((n_ctx, HEAD), F32), jnp.zeros((n_ctx, HEAD), F32)

    w_in = ev_w_in[0]
    o0 = 0
    wqa = w_in[:, o0:o0 + a_q]; o0 += a_q
    wka = w_in[:, o0:o0 + a_kvw]; o0 += a_kvw
    wva = w_in[:, o0:o0 + a_kvw]; o0 += a_kvw
    wqb = w_in[:, o0:o0 + b_qk]; o0 += b_qk
    wkb = w_in[:, o0:o0 + b_qk]; o0 += b_qk
    wvb = w_in[:, o0:o0 + b_v]; o0 += b_v
    wob = w_in[:, o0:o0 + b_v]; o0 += b_v
    wgt = w_in[:, o0:]
    n_gates = wgt.shape[1]
    w0 = jnp.concatenate([_halves(wqa), wvb, wqb, wkb, _halves(wka), wva, wob], axis=1).astype(BF16)
    ones = lambda n: jnp.ones((n,), F32)
    cs0 = jnp.concatenate([
        jnp.tile(_halves(ev_q_gain[0]), a_heads) * score_scale, ones(b_v), ones(b_qk) * (HEAD ** -0.5), ones(b_qk),
        jnp.tile(_halves(ev_k_gain[0]), a_kv), ones(a_kvw), ones(b_v)])[None, :]
    modes0 = (["R"] * a_heads + ["P"] * (b_v // HEAD) + ["P"] * (2 * b_qk // HEAD) + ["R"] * a_kv
              + ["P"] * a_kv + ["P"] * (b_v // HEAD))
    nbf0 = a_q + b_v + 2 * b_qk + 2 * a_kvw
    wg0 = jnp.zeros((d, LANES), F32).at[:, :n_gates].set(wgt).astype(BF16)
    gb0 = jnp.zeros((1, LANES), F32).at[0, :n_gates].set(ev_gate_b[0])
    col_vb, col_qb, col_kb = a_q // b_v, (a_q + b_v) // b_qk, (a_q + b_v + b_qk) // b_qk
    col_ka = (a_q + b_v + 2 * b_qk) // HEAD
    col_va = col_ka + a_kv

    ml, mc = mod(0, 0), mod(0, 1)
    pl_b, pl_o, pl_g = _proj(x_lat, ml[0], ml[1], w0, cs0, cos_l, sin_l, modes0, nbf0, wg0, gb0)
    pc_b, pc_o, pc_g = _proj(x_ctx, mc[0], mc[1], w0, cs0, cos_c, sin_c, modes0, nbf0, wg0, gb0)
    p_all = jnp.concatenate([pl_b, pc_b], axis=0)
    g_all = jnp.concatenate([pl_g, pc_g], axis=0)
    g_row = g_all[:, :4 * b_heads].T

    ka_cols = slice(col_ka * HEAD, (col_ka + a_kv) * HEAD)
    att_l = _gqa_attention(pl_b, p_all[:, ka_cols].T, p_all, a_kv, col_va, tq=1024, tk=1280)
    att_c = _gqa_attention(pc_b, pc_b[:, ka_cols].T, pc_b, a_kv, col_va, tq=256, tk=256)
    hf, hb = _mlstm(p_all, g_all, g_row, b_heads, col_qb, col_kb, col_vb, s_len, n_ctx)

    hg = ev_h_gain[0].reshape(1, b_v)
    w_out0 = ev_w_out[0].astype(BF16)
    x_lat = _out_proj(x_lat, ml[2], w_out0, att_l, merge=(hf, hb, pl_o, hg, b_heads))
    x_ctx = _out_proj(x_ctx, mc[2], w_out0, att_c, merge=(hf, hb, pc_o, hg, b_heads), row_off=s_len)
    ffw = (ev_ff_gate[0].astype(BF16), ev_ff_up[0].astype(BF16), ev_ff_down[0].astype(BF16))
    x_lat = _ffn_dense(x_lat, ml[3], ml[4], ml[5], *ffw)
    x_ctx = _ffn_dense(x_ctx, mc[3], mc[4], mc[5], *ffw)

    c_qk = c_heads * 2 * HEAD
    w1 = od_w_in[0]
    w1 = jnp.concatenate([_halves(w1[:, :2 * c_qk]), w1[:, 2 * c_qk:]], axis=1).astype(BF16)
    cs1 = jnp.concatenate([jnp.tile(_halves(od_q_gain[0]), 2 * c_heads) * score_scale,
                           jnp.tile(_halves(od_k_gain[0]), 2 * c_heads), ones(c_heads * V_HEAD)])[None, :]
    modes1 = ["R"] * (4 * c_heads) + ["P"] * (c_heads * V_HEAD // HEAD)
    ml, mc = mod(1, 0), mod(1, 1)
    (ql,) = _proj(x_lat, ml[0], ml[1], w1, cs1, cos_l, sin_l, modes1, w1.shape[1])
    (qc,) = _proj(x_ctx, mc[0], mc[1], w1, cs1, cos_c, sin_c, modes1, w1.shape[1])
    kv_all = jnp.concatenate([ql, qc], axis=0)
    lam_init = 0.8 - 0.6 * math.exp(-0.3 * 1)
    vec = lambda a: a[0].reshape(1, -1)
    kt_all = kv_all[:, c_qk:2 * c_qk].T
    o_l = _diff_attention(ql, kt_all, kv_all, c_heads, lam_init, vec(od_lam_q1), vec(od_lam_k1), vec(od_lam_q2),
                          vec(od_lam_k2), vec(od_sub_gain), tq=2048, tk=1280)
    x_lat = _out_proj(x_lat, ml[2], od_w_out[0].astype(BF16), o_l)
    x_lat = _moe_top2(x_lat, ml[3], ml[4], ml[5], od_router[0], od_ex_gate[0].astype(BF16),
                      od_ex_up[0].astype(BF16), od_ex_down[0].astype(BF16))
    return x_lat[None]
```

```python
import functools
import math

import jax
import jax.numpy as jnp
from jax import lax
from jax.experimental import pallas as pl
from jax.experimental.pallas import tpu as pltpu

F32 = jnp.float32
BF16 = jnp.bfloat16

GRID_W = 64
ROPE_THETA = 10000.0
EPS = 1e-6
M_INIT = -1e30
HEAD = 128
V_HEAD = 256
GQA_GROUP = 4
TOP_K = 2
MLSTM_CHUNK = 256
GQA_SUB_ROWS = 256
DIFF_SUB_ROWS = 256
PIPE_SKEW = 1
SOFTMAX_ROW_GROUP = 64
LANES = 128
VMEM_LIMIT = 56 * 1024 * 1024
NEG_INF = float("-inf")


def _cparams(sem, flags=None):
    return pltpu.CompilerParams(dimension_semantics=sem, vmem_limit_bytes=VMEM_LIMIT, flags=flags)


ATTN_FLAGS = None


def _dot(a, b):
    return jnp.dot(a, b, preferred_element_type=F32)


def _dot_nt(a, b):
    return lax.dot_general(a, b, (((1,), (1,)), ((), ())), preferred_element_type=F32)


def _dot_tn(a, b):
    return lax.dot_general(a, b, (((0,), (0,)), ((), ())), preferred_element_type=F32)


def _rms(x):
    return x * lax.rsqrt(jnp.mean(x * x, axis=-1, keepdims=True) + EPS)


def _mod_kernel(c_ref, w_ref, b_ref, o_ref):
    c = c_ref[...]
    s = c * jax.nn.sigmoid(c)
    o_ref[...] = _dot(s.astype(BF16), w_ref[...].astype(BF16)) + b_ref[...]


def _modulation(cvecs, ada_w, ada_b):
    depth, d, n = ada_w.shape
    tn = 1024
    return pl.pallas_call(
        _mod_kernel,
        grid=(depth, n // tn),
        in_specs=[
            pl.BlockSpec((8, d), lambda l, j: (0, 0)),
            pl.BlockSpec((None, d, tn), lambda l, j: (l, 0, j)),
            pl.BlockSpec((None, 1, tn), lambda l, j: (l, 0, j)),
        ],
        out_specs=pl.BlockSpec((None, 8, tn), lambda l, j: (l, 0, j)),
        out_shape=jax.ShapeDtypeStruct((depth, 8, n), F32),
        compiler_params=_cparams(("parallel", "parallel")),
        name="adaln_mod",
    )(cvecs, ada_w, ada_b.reshape(depth, 1, n))


def _proj_kernel(patterns, n_bf_tiles, has_f32, has_gates, tn, *refs):
    it = iter(refs)
    x_ref, sh_ref, sc_ref, w_ref, cs_ref, cos_ref, sin_ref = (next(it) for _ in range(7))
    if has_gates:
        wg_ref, gb_ref = next(it), next(it)
    ob_ref = next(it)
    of_ref = next(it) if has_f32 else None
    og_ref = next(it) if has_gates else None
    h_scr = next(it)

    j = pl.program_id(1)

    @pl.when(j == 0)
    def _():
        h = _rms(x_ref[...]) * (1.0 + sc_ref[...]) + sh_ref[...]
        h_scr[...] = h.astype(BF16)
        if has_gates:
            og_ref[...] = _dot(h_scr[...], wg_ref[...]) + gb_ref[...]

    acc = _dot(h_scr[...], w_ref[...])
    cs = cs_ref[...]

    for lo, hi, pat in patterns:
        @pl.when(jnp.logical_and(j >= lo, j < hi))
        def _(pat=pat, lo=lo):
            to_f32 = has_f32 and lo >= n_bf_tiles
            out_ref = of_ref if to_f32 else ob_ref
            for hidx, mode in enumerate(pat):
                sl = slice(hidx * HEAD, (hidx + 1) * HEAD)
                y = acc[:, sl]
                if mode == "R":
                    y = _rms(y) * cs[:, sl]
                    y = y * cos_ref[...] + pltpu.roll(y, HEAD // 2, 1) * sin_ref[...]
                else:
                    y = y * cs[:, sl]
                out_ref[:, sl] = y.astype(out_ref.dtype)


def _proj(x, shift, scale, w, colscale, cos2, sin2, head_modes, n_bf_cols, wg=None, gb=None, tm=1024, tn=512):
    m, d = x.shape
    n = w.shape[1]
    tm = min(tm, m)
    assert m % tm == 0 and n % tn == 0 and n_bf_cols % tn == 0
    nj = n // tn
    n_bf_tiles = n_bf_cols // tn
    has_f32 = n_bf_cols < n
    has_gates = wg is not None
    hpt = tn // HEAD
    tiles = [tuple(head_modes[t * hpt:(t + 1) * hpt]) for t in range(nj)]
    patterns = []
    for t, pat in enumerate(tiles):
        if patterns and patterns[-1][2] == pat and not (has_f32 and t == n_bf_tiles):
            patterns[-1] = (patterns[-1][0], t + 1, pat)
        else:
            patterns.append((t, t + 1, pat))

    in_specs = [
        pl.BlockSpec((tm, d), lambda i, j: (i, 0)),
        pl.BlockSpec((1, d), lambda i, j: (0, 0)),
        pl.BlockSpec((1, d), lambda i, j: (0, 0)),
        pl.BlockSpec((d, tn), lambda i, j: (0, j)),
        pl.BlockSpec((1, tn), lambda i, j: (0, j)),
        pl.BlockSpec((tm, HEAD), lambda i, j: (i, 0)),
        pl.BlockSpec((tm, HEAD), lambda i, j: (i, 0)),
    ]
    args = [x, shift, scale, w, colscale, cos2, sin2]
    if has_gates:
        in_specs += [pl.BlockSpec((d, LANES), lambda i, j: (0, 0)), pl.BlockSpec((1, LANES), lambda i, j: (0, 0))]
        args += [wg, gb]
    last_bf = n_bf_tiles - 1
    out_specs = [pl.BlockSpec((tm, tn), lambda i, j: (i, jnp.minimum(j, last_bf)))]
    out_shape = [jax.ShapeDtypeStruct((m, n_bf_cols), BF16)]
    if has_f32:
        out_specs.append(pl.BlockSpec((tm, tn), lambda i, j: (i, jnp.maximum(j - n_bf_tiles, 0))))
        out_shape.append(jax.ShapeDtypeStruct((m, n - n_bf_cols), F32))
    if has_gates:
        out_specs.append(pl.BlockSpec((tm, LANES), lambda i, j: (i, 0)))
        out_shape.append(jax.ShapeDtypeStruct((m, LANES), F32))
    return pl.pallas_call(
        functools.partial(_proj_kernel, tuple(patterns), n_bf_tiles, has_f32, has_gates, tn),
        grid=(m // tm, nj),
        in_specs=in_specs,
        out_specs=out_specs,
        out_shape=out_shape,
        scratch_shapes=[pltpu.VMEM((tm, d), BF16)],
        compiler_params=_cparams(("parallel", "arbitrary")),
        name="norm_mod_proj",
    )(*args)


def _softmax_pipeline(score_fns, v, row_slices, m_scr, l_scr, acc_scr, s_scr, p_scr):
    n = len(score_fns)
    alphas = {}
    k = PIPE_SKEW
    for t in range(n + 2 * k):
        if t < n:
            s_scr[t] = score_fns[t]()
        c = t - k
        if 0 <= c < n:
            alphas[c] = _softmax_probs(row_slices[c], m_scr, l_scr, s_scr.at[c], p_scr.at[c])
        c = t - 2 * k
        if 0 <= c < n:
            rows = row_slices[c]
            acc_scr[rows, :] = alphas.pop(c) * acc_scr[rows, :] + _dot(p_scr[c], v)


def _softmax_probs(rows, m_scr, l_scr, s_buf, p_buf):
    sub, tk = s_buf.shape
    n_slab = tk // LANES
    mx = s_buf[:, 0:LANES]
    for c in range(1, n_slab):
        mx = jnp.maximum(mx, s_buf[:, c * LANES:(c + 1) * LANES])
    m_prev = m_scr[rows, :]
    m_new = jnp.maximum(m_prev, jnp.max(mx, axis=1, keepdims=True))
    alpha = jnp.exp2(m_prev - m_new)
    m_scr[rows, :] = m_new
    rg = min(sub, SOFTMAX_ROW_GROUP)
    for g in range(sub // rg):
        gr = slice(g * rg, (g + 1) * rg)
        mb = jnp.broadcast_to(m_new[gr, :], (rg, LANES))
        lsum = jnp.zeros((rg, LANES), F32)
        for c in range(n_slab):
            cs = slice(c * LANES, (c + 1) * LANES)
            p = jnp.exp2(s_buf[gr, cs] - mb)
            if l_scr is not None:
                lsum = lsum + p
            p_buf[gr, cs] = p.astype(BF16)
        if l_scr is not None:
            lr = slice(rows.start + g * rg, rows.start + (g + 1) * rg)
            l_scr[lr, :] = alpha[gr, :] * l_scr[lr, :] + lsum
    return alpha


def _gqa_kernel(tq, q_ref, kt_ref, v_ref, o_ref, m_scr, acc_scr, s_scr, p_scr):
    kj = pl.program_id(2)

    @pl.when(kj == 0)
    def _():
        m_scr[...] = jnp.full(m_scr.shape, NEG_INF, F32)
        acc_scr[...] = jnp.zeros(acc_scr.shape, F32)

    kt = kt_ref[...]
    v = jnp.concatenate([v_ref[...], jnp.ones(v_ref.shape, BF16)], axis=1)
    sub = s_scr.shape[1]
    nb = tq // sub
    fns, rows = [], []
    for g in range(GQA_GROUP):
        for b in range(nb):
            fns.append(lambda g=g, b=b: _dot(q_ref[b * sub:(b + 1) * sub, g * HEAD:(g + 1) * HEAD], kt))
            rows.append(slice(g * tq + b * sub, g * tq + (b + 1) * sub))
    _softmax_pipeline(fns, v, rows, m_scr, None, acc_scr, s_scr, p_scr)

    @pl.when(kj == pl.num_programs(2) - 1)
    def _():
        out = acc_scr[:, :HEAD] / acc_scr[:, HEAD:HEAD + 1]
        for g in range(GQA_GROUP):
            o_ref[:, g * HEAD:(g + 1) * HEAD] = out[g * tq:(g + 1) * tq, :].astype(o_ref.dtype)


def _gqa_attention(q_arr, kt_arr, kv_arr, n_kv, v_col, tq, tk):
    s_len = q_arr.shape[0]
    lk = kv_arr.shape[0]
    tq = min(tq, s_len)
    tk = min(tk, lk)
    assert s_len % tq == 0 and lk % tk == 0 and tk % LANES == 0
    gw = GQA_GROUP * HEAD
    sub = min(tq, GQA_SUB_ROWS)
    n_chain = GQA_GROUP * (tq // sub)
    return pl.pallas_call(
        functools.partial(_gqa_kernel, tq),
        grid=(n_kv, s_len // tq, lk // tk),
        in_specs=[
            pl.BlockSpec((tq, gw), lambda n, i, j: (i, n)),
            pl.BlockSpec((HEAD, tk), lambda n, i, j: (n, j)),
            pl.BlockSpec((tk, HEAD), lambda n, i, j: (j, v_col + n)),
        ],
        out_specs=pl.BlockSpec((tq, gw), lambda n, i, j: (i, n)),
        out_shape=jax.ShapeDtypeStruct((s_len, n_kv * gw), BF16),
        scratch_shapes=[
            pltpu.VMEM((GQA_GROUP * tq, 1), F32),
            pltpu.VMEM((GQA_GROUP * tq, 2 * HEAD), F32),
            pltpu.VMEM((n_chain, sub, tk), F32),
            pltpu.VMEM((n_chain, sub, tk), BF16),
        ],
        compiler_params=_cparams(("parallel", "parallel", "arbitrary"), ATTN_FLAGS),
        name="gqa_flash",
    )(q_arr, kt_arr, kv_arr)


def _diff_kernel(tq, lam_init, q_ref, kt_ref, v_ref, lq1_ref, lk1_ref, lq2_ref, lk2_ref, sg_ref, o_ref,
                 m_scr, l_scr, acc_scr, s_scr, p_scr):
    kj = pl.program_id(2)

    @pl.when(kj == 0)
    def _():
        m_scr[...] = jnp.full(m_scr.shape, NEG_INF, F32)
        l_scr[...] = jnp.zeros(l_scr.shape, F32)
        acc_scr[...] = jnp.zeros(acc_scr.shape, F32)

    v = v_ref[...]
    sub = s_scr.shape[1]
    nb = tq // sub
    fns, rows = [], []
    for b in range(nb):
        for r in range(2):
            fns.append(lambda r=r, b=b: _dot(q_ref[b * sub:(b + 1) * sub, r * HEAD:(r + 1) * HEAD],
                                             kt_ref[r * HEAD:(r + 1) * HEAD, :]))
            rows.append(slice(r * tq + b * sub, r * tq + (b + 1) * sub))
    _softmax_pipeline(fns, v, rows, m_scr, l_scr, acc_scr, s_scr, p_scr)

    @pl.when(kj == pl.num_programs(2) - 1)
    def _():
        lam = (jnp.exp(jnp.sum(lq1_ref[...] * lk1_ref[...], axis=1, keepdims=True))
               - jnp.exp(jnp.sum(lq2_ref[...] * lk2_ref[...], axis=1, keepdims=True)) + lam_init)
        o = acc_scr[...] / jnp.sum(l_scr[...], axis=1, keepdims=True)
        o = o[:tq, :] - lam * o[tq:, :]
        o = _rms(o) * sg_ref[...] * (1.0 - lam_init)
        o_ref[...] = o.astype(o_ref.dtype)


def _diff_attention(q_arr, kt_arr, kv_arr, n_heads, lam_init, lq1, lk1, lq2, lk2, sub_gain, tq, tk):
    s_len = q_arr.shape[0]
    lk = kv_arr.shape[0]
    tq = min(tq, s_len)
    tk = min(tk, lk)
    assert s_len % tq == 0 and lk % tk == 0 and tk % LANES == 0
    sub = min(tq, DIFF_SUB_ROWS)
    n_chain = 2 * (tq // sub)
    vec = pl.BlockSpec((1, HEAD), lambda h, i, j: (0, 0))
    return pl.pallas_call(
        functools.partial(_diff_kernel, tq, lam_init),
        grid=(n_heads, s_len // tq, lk // tk),
        in_specs=[
            pl.BlockSpec((tq, V_HEAD), lambda h, i, j: (i, h)),
            pl.BlockSpec((V_HEAD, tk), lambda h, i, j: (h, j)),
            pl.BlockSpec((tk, V_HEAD), lambda h, i, j: (j, 2 * n_heads + h)),
            vec, vec, vec, vec,
            pl.BlockSpec((1, V_HEAD), lambda h, i, j: (0, 0)),
        ],
        out_specs=pl.BlockSpec((tq, V_HEAD), lambda h, i, j: (i, h)),
        out_shape=jax.ShapeDtypeStruct((s_len, n_heads * V_HEAD), BF16),
        scratch_shapes=[
            pltpu.VMEM((2 * tq, 1), F32),
            pltpu.VMEM((2 * tq, LANES), F32),
            pltpu.VMEM((2 * tq, V_HEAD), F32),
            pltpu.VMEM((n_chain, sub, tk), F32),
            pltpu.VMEM((n_chain, sub, tk), BF16),
        ],
        compiler_params=_cparams(("parallel", "parallel", "arbitrary"), ATTN_FLAGS),
        name="diff_flash",
    )(q_arr, kt_arr, kv_arr, lq1, lk1, lq2, lk2, sub_gain)


def _split3(x):
    hi = x.astype(BF16)
    r1 = x - hi.astype(F32)
    mid = r1.astype(BF16)
    lo = (r1 - mid.astype(F32)).astype(BF16)
    return hi, mid, lo


def _log_sigmoid(x):
    return jnp.minimum(x, 0.0) - jnp.log(1.0 + jnp.exp(-jnp.abs(x)))


def _mlstm_kernel(nh, chunk, qf_ref, kf_ref, vf_ref, gcf_ref, grf_ref, qb_ref, kb_ref, vb_ref, gcb_ref, grb_ref,
                  hf_ref, hb_ref, c_scr, n_scr, m_scr):
    t = pl.program_id(0)

    @pl.when(t == 0)
    def _():
        c_scr[...] = jnp.zeros(c_scr.shape, F32)
        n_scr[...] = jnp.zeros(n_scr.shape, F32)
        m_scr[...] = jnp.full(m_scr.shape, M_INIT, F32)

    row = lax.broadcasted_iota(jnp.int32, (chunk, chunk), 0)
    col = lax.broadcasted_iota(jnp.int32, (chunk, chunk), 1)
    lower = col <= row
    upper = col >= row
    lower_b = lower.astype(BF16)
    upper_b = upper.astype(BF16)

    for rev in (False, True):
        q_ref, k_ref, v_ref, gc_ref, gr_ref, h_ref = (
            (qb_ref, kb_ref, vb_ref, gcb_ref, grb_ref, hb_ref) if rev else
            (qf_ref, kf_ref, vf_ref, gcf_ref, grf_ref, hf_ref))
        mask = upper if rev else lower
        gcol = gc_ref[...]
        grow = gr_ref[...]
        cum_col = sum(_dot(upper_b if rev else lower_b, piece) for piece in _split3(_log_sigmoid(gcol)))
        cum_row = sum(_dot(piece, lower_b if rev else upper_b) for piece in _split3(_log_sigmoid(grow)))
        for hd in range(nh):
            idx = (1 if rev else 0) * nh + hd
            ki = (2 if rev else 0) * nh + hd
            kf = ki + nh
            qc = q_ref[:, hd * HEAD:(hd + 1) * HEAD]
            kc = k_ref[:, hd * HEAD:(hd + 1) * HEAD]
            vc = v_ref[:, hd * V_HEAD:(hd + 1) * V_HEAD]
            i_col = gcol[:, ki:ki + 1]
            i_row = grow[ki:ki + 1, :]
            b_col = cum_col[:, kf:kf + 1]
            b_row = cum_row[kf:kf + 1, :]
            c_mem = c_scr[idx]
            n_mem = n_scr[idx]
            m = m_scr[idx][:, :1]

            dlog = jnp.where(mask, b_col + (i_row - b_row), NEG_INF)
            inter = b_col + m
            m_out = jnp.maximum(inter, jnp.max(dlog, axis=1, keepdims=True))
            w_intra = jnp.exp(dlog - m_out)
            w_inter = jnp.exp(inter - m_out)
            qk = _dot_nt(qc, kc) * w_intra
            num = w_inter * _dot(qc, c_mem.astype(BF16)) + _dot(qk.astype(BF16), vc)
            den = (w_inter * jnp.sum(qc.astype(F32) * n_mem, axis=1, keepdims=True)
                   + jnp.sum(qk, axis=1, keepdims=True))
            h = num / jnp.maximum(jnp.abs(den), jnp.exp(-m_out))
            h_ref[:, hd * V_HEAD:(hd + 1) * V_HEAD] = h

            b_last = b_col[0:1, :] if rev else b_col[chunk - 1:chunk, :]
            g_col = b_last - b_col + i_col
            m_new = jnp.maximum(b_last + m, jnp.max(g_col, axis=0, keepdims=True))
            a_dec = jnp.exp(b_last + m - m_new)
            kw = kc.astype(F32) * jnp.exp(g_col - m_new)
            c_scr[idx] = a_dec * c_mem + _dot_tn(kw.astype(BF16), vc)
            n_scr[idx] = a_dec * n_mem + jnp.sum(kw, axis=0, keepdims=True)
            m_scr[idx] = jnp.broadcast_to(m_new, (1, LANES))


def _mlstm(qkv, gates_col, gates_row, nh, q_col, k_col, v_col, n_lat, n_ctx):
    t_len = qkv.shape[0]
    chunk = MLSTM_CHUNK
    assert n_lat % chunk == 0 and n_ctx % chunk == 0 and t_len == n_lat + n_ctx
    nlat, nctx = n_lat // chunk, n_ctx // chunk
    nc = nlat + nctx

    def fwd(t):
        return jnp.where(t < nctx, nlat + t, t - nctx)

    def bwd(t):
        return jnp.where(t < nctx, nc - 1 - t, nlat - 1 - (t - nctx))

    qw, vw = nh * HEAD, nh * V_HEAD

    def specs(order):
        return [
            pl.BlockSpec((chunk, qw), lambda t: (order(t), q_col)),
            pl.BlockSpec((chunk, qw), lambda t: (order(t), k_col)),
            pl.BlockSpec((chunk, vw), lambda t: (order(t), v_col)),
            pl.BlockSpec((chunk, LANES), lambda t: (order(t), 0)),
            pl.BlockSpec((4 * nh, chunk), lambda t: (0, order(t))),
        ]

    return pl.pallas_call(
        functools.partial(_mlstm_kernel, nh, chunk),
        grid=(nc,),
        in_specs=specs(fwd) + specs(bwd),
        out_specs=[pl.BlockSpec((chunk, vw), lambda t: (fwd(t), 0)),
                   pl.BlockSpec((chunk, vw), lambda t: (bwd(t), 0))],
        out_shape=[jax.ShapeDtypeStruct((t_len, vw), F32)] * 2,
        scratch_shapes=[
            pltpu.VMEM((2 * nh, HEAD, V_HEAD), F32),
            pltpu.VMEM((2 * nh, 1, HEAD), F32),
            pltpu.VMEM((2 * nh, 1, LANES), F32),
        ],
        compiler_params=_cparams(("arbitrary",)),
        name="mlstm_scan",
    )(qkv, qkv, qkv, gates_col, gates_row, qkv, qkv, qkv, gates_col, gates_row)


def _out_kernel(merge, nh, *refs):
    if merge:
        a_ref, hf_ref, hb_ref, o_ref, hg_ref, w_ref, x_ref, g_ref, y_ref, mix_scr = refs
        aw = a_ref.shape[1]

        @pl.when(pl.program_id(1) == 0)
        def _():
            mix_scr[:, :aw] = a_ref[...]
            for hd in range(nh):
                sl = slice(hd * V_HEAD, (hd + 1) * V_HEAD)
                hm = _rms(hf_ref[:, sl] + hb_ref[:, sl]) * hg_ref[:, sl] * jax.nn.sigmoid(o_ref[:, sl])
                mix_scr[:, aw + hd * V_HEAD:aw + (hd + 1) * V_HEAD] = hm.astype(BF16)

        lhs = mix_scr[...]
    else:
        a_ref, w_ref, x_ref, g_ref, y_ref = refs
        lhs = a_ref[...]
    y_ref[...] = x_ref[...] + g_ref[...] * _dot(lhs, w_ref[...])


def _out_proj(x, gate, w, a, merge=None, row_off=0, tm=512, tn=1024):
    m, d = x.shape
    kdim, n = w.shape
    tm = min(tm, m)
    assert m % tm == 0 and n % tn == 0 and row_off % tm == 0
    off = row_off // tm
    in_specs = [pl.BlockSpec((tm, a.shape[1]), lambda i, j: (i, 0))]
    args = [a]
    scratch = []
    nh = 0
    if merge is not None:
        hf, hb, o, hg, nh = merge
        hw = hf.shape[1]
        in_specs += [pl.BlockSpec((tm, hw), lambda i, j: (i + off, 0)),
                     pl.BlockSpec((tm, hw), lambda i, j: (i + off, 0)),
                     pl.BlockSpec((tm, hw), lambda i, j: (i, 0)),
                     pl.BlockSpec((1, hw), lambda i, j: (0, 0))]
        args += [hf, hb, o, hg]
        scratch = [pltpu.VMEM((tm, kdim), BF16)]
    in_specs += [pl.BlockSpec((kdim, tn), lambda i, j: (0, j)),
                 pl.BlockSpec((tm, tn), lambda i, j: (i, j)),
                 pl.BlockSpec((1, tn), lambda i, j: (0, j))]
    args += [w, x, gate]
    return pl.pallas_call(
        functools.partial(_out_kernel, merge is not None, nh),
        grid=(m // tm, n // tn),
        in_specs=in_specs,
        out_specs=pl.BlockSpec((tm, tn), lambda i, j: (i, j)),
        out_shape=jax.ShapeDtypeStruct((m, n), F32),
        scratch_shapes=scratch,
        compiler_params=_cparams(("parallel", "arbitrary")),
        name="out_proj_residual",
    )(*args)


def _ffn_kernel(x_ref, sh_ref, sc_ref, g_ref, wg_ref, wu_ref, wd_ref, y_ref, h_scr, acc_scr):
    f = pl.program_id(1)

    @pl.when(f == 0)
    def _():
        h = _rms(x_ref[...]) * (1.0 + sc_ref[...]) + sh_ref[...]
        h_scr[...] = h.astype(BF16)
        acc_scr[...] = jnp.zeros(acc_scr.shape, F32)

    h = h_scr[...]
    gt = _dot(h, wg_ref[...])
    up = _dot(h, wu_ref[...])
    act = (gt * jax.nn.sigmoid(gt)) * up
    acc_scr[...] += _dot(act.astype(BF16), wd_ref[...])

    @pl.when(f == pl.num_programs(1) - 1)
    def _():
        y_ref[...] = x_ref[...] + g_ref[...] * acc_scr[...]


def _ffn_dense(x, shift, scale, gate, wg, wu, wd, tm=512, tf=512):
    m, d = x.shape
    ff = wg.shape[1]
    tm = min(tm, m)
    assert m % tm == 0 and ff % tf == 0
    vec = pl.BlockSpec((1, d), lambda i, f: (0, 0))
    return pl.pallas_call(
        _ffn_kernel,
        grid=(m // tm, ff // tf),
        in_specs=[
            pl.BlockSpec((tm, d), lambda i, f: (i, 0)), vec, vec, vec,
            pl.BlockSpec((d, tf), lambda i, f: (0, f)),
            pl.BlockSpec((d, tf), lambda i, f: (0, f)),
            pl.BlockSpec((tf, d), lambda i, f: (f, 0)),
        ],
        out_specs=pl.BlockSpec((tm, d), lambda i, f: (i, 0)),
        out_shape=jax.ShapeDtypeStruct((m, d), F32),
        scratch_shapes=[pltpu.VMEM((tm, d), BF16), pltpu.VMEM((tm, d), F32)],
        compiler_params=_cparams(("parallel", "arbitrary")),
        name="ffn_dense",
    )(x, shift, scale, gate, wg, wu, wd)


R_E1, R_E2, R_W1, R_W2, R_RANK1, R_RANK2 = range(6)


def _lane_pick(a, lane, k):
    return jnp.sum(jnp.where(lane == k, a, 0.0), axis=1, keepdims=True)


def _router_kernel(n_exp, x_ref, sh_ref, sc_ref, r_ref, info_ref, cnt_ref, carry_scr):
    @pl.when(pl.program_id(0) == 0)
    def _():
        carry_scr[...] = jnp.zeros(carry_scr.shape, F32)

    tm = x_ref.shape[0]
    h = _rms(x_ref[...]) * (1.0 + sc_ref[...]) + sh_ref[...]
    logits = jnp.dot(h, r_ref[...], preferred_element_type=F32, precision=lax.Precision.HIGHEST)
    lane = lax.broadcasted_iota(jnp.int32, logits.shape, 1)
    logits = jnp.where(lane < n_exp, logits, NEG_INF)
    v1 = jnp.max(logits, axis=1, keepdims=True)
    i1 = jnp.min(jnp.where(logits == v1, lane, LANES), axis=1, keepdims=True)
    rest = jnp.where(lane == i1, NEG_INF, logits)
    v2 = jnp.max(rest, axis=1, keepdims=True)
    i2 = jnp.min(jnp.where(rest == v2, lane, LANES), axis=1, keepdims=True)
    e2 = jnp.exp(v2 - v1)
    w1 = 1.0 / (1.0 + e2)
    w2 = e2 / (1.0 + e2)

    chosen = jnp.where(jnp.logical_or(lane == i1, lane == i2), 1.0, 0.0)
    r_i = lax.broadcasted_iota(jnp.int32, (tm, tm), 0)
    c_i = lax.broadcasted_iota(jnp.int32, (tm, tm), 1)
    before = (c_i < r_i).astype(BF16)
    rank = _dot(before, chosen.astype(BF16)) + carry_scr[...]
    carry_scr[...] += jnp.sum(chosen, axis=0, keepdims=True)
    cnt_ref[...] = carry_scr[...]
    rank1 = jnp.sum(jnp.where(lane == i1, rank, 0.0), axis=1, keepdims=True)
    rank2 = jnp.sum(jnp.where(lane == i2, rank, 0.0), axis=1, keepdims=True)
    info = jnp.zeros(logits.shape, F32)
    for k, val in ((R_E1, i1.astype(F32)), (R_E2, i2.astype(F32)), (R_W1, w1), (R_W2, w2),
                   (R_RANK1, rank1), (R_RANK2, rank2)):
        info = jnp.where(lane == k, val, info)
    info_ref[...] = info


def _router(x, shift, scale, router, tm=512):
    m, d = x.shape
    n_exp = router.shape[1]
    vec = pl.BlockSpec((1, d), lambda i: (0, 0))
    router_pad = jnp.zeros((d, LANES), F32).at[:, :n_exp].set(router)
    return pl.pallas_call(
        functools.partial(_router_kernel, n_exp),
        grid=(m // tm,),
        in_specs=[pl.BlockSpec((tm, d), lambda i: (i, 0)), vec, vec, pl.BlockSpec((d, LANES), lambda i: (0, 0))],
        out_specs=[pl.BlockSpec((tm, LANES), lambda i: (i, 0)), pl.BlockSpec((1, LANES), lambda i: (0, 0))],
        out_shape=[jax.ShapeDtypeStruct((m, LANES), F32), jax.ShapeDtypeStruct((1, LANES), F32)],
        scratch_shapes=[pltpu.VMEM((1, LANES), F32)],
        compiler_params=_cparams(("arbitrary",)),
        name="moe_router",
    )(x, shift, scale, router_pad)


def _row_copies(src_at, dst_at, n_rows, sem, start):
    def body(r, carry):
        cp = pltpu.make_async_copy(src_at(r), dst_at(r), sem)
        if start:
            cp.start()
        else:
            cp.wait()
        return carry
    lax.fori_loop(0, n_rows, body, 0)


def _moe_ffn_kernel(te_ref, nu_ref, nv_ref, src_ref, nsrc_ref, dst_ref, x_hbm, sh_ref, sc_ref, wg_ref, wu_ref, wd_ref,
                    y_hbm, xbuf, hbuf, acc, stage, sem):
    i = pl.program_id(0)
    f = pl.program_id(1)
    tm = xbuf.shape[0]
    n_used = nu_ref[0]
    active = i < n_used
    g_sem, s_sem = sem.at[0], sem.at[1]
    row0 = lambda ref: (lambda r: ref.at[pl.ds(0, 1), :])

    def gather(idx_ref, start):
        _row_copies(lambda r: x_hbm.at[pl.ds(idx_ref[0, r], 1), :] if start else x_hbm.at[pl.ds(0, 1), :],
                    lambda r: xbuf.at[pl.ds(r, 1), :], tm, g_sem, start)

    @pl.when(jnp.logical_and(active, f == 0))
    def _():
        pl.when(i == 0)(lambda: gather(src_ref, True))
        gather(src_ref, False)
        h = _rms(xbuf[...]) * (1.0 + sc_ref[...]) + sh_ref[...]
        hbuf[...] = h.astype(BF16)
        acc[...] = jnp.zeros(acc.shape, F32)
        pl.when(i + 1 < n_used)(lambda: gather(nsrc_ref, True))

    @pl.when(active)
    def _():
        h = hbuf[...]
        gt = _dot(h, wg_ref[...])
        up = _dot(h, wu_ref[...])
        act = (gt * jax.nn.sigmoid(gt)) * up
        acc[...] += _dot(act.astype(BF16), wd_ref[...])

    @pl.when(jnp.logical_and(active, f == pl.num_programs(1) - 1))
    def _():
        wait_scatter = lambda n: _row_copies(row0(stage), row0(y_hbm), n, s_sem, False)
        pl.when(i > 0)(lambda: wait_scatter(nv_ref[jnp.maximum(i - 1, 0)]))
        stage[...] = acc[...]
        _row_copies(lambda r: stage.at[pl.ds(r, 1), :], lambda r: y_hbm.at[pl.ds(dst_ref[0, r], 1), :],
                    nv_ref[i], s_sem, True)
        pl.when(i == n_used - 1)(lambda: wait_scatter(nv_ref[i]))


def _moe_ffn(x, shift, scale, tile_exp, n_used, n_valid, src, dst, wg, wu, wd, n_out_rows, tm, tf=512):
    m, d = x.shape
    n_exp, _, ff = wg.shape
    n_tiles = src.shape[0]
    tf = min(tf, ff)
    assert ff % tf == 0
    nf = ff // tf
    vec = pl.BlockSpec((1, d), lambda i, f, te, nu, nv: (0, 0))
    idx = pl.BlockSpec((None, 1, tm), lambda i, f, te, nu, nv: (i, 0, 0), memory_space=pltpu.SMEM)
    idx_next = pl.BlockSpec((None, 1, tm), lambda i, f, te, nu, nv: (jnp.minimum(i + 1, n_tiles - 1), 0, 0),
                            memory_space=pltpu.SMEM)

    def f_eff(i, f, nu):
        return jnp.where(i < nu[0], f, nf - 1)

    grid_spec = pltpu.PrefetchScalarGridSpec(
        num_scalar_prefetch=3,
        grid=(n_tiles, nf),
        in_specs=[
            idx, idx_next, idx,
            pl.BlockSpec(memory_space=pl.ANY),
            vec, vec,
            pl.BlockSpec((None, d, tf), lambda i, f, te, nu, nv: (te[i], 0, f_eff(i, f, nu))),
            pl.BlockSpec((None, d, tf), lambda i, f, te, nu, nv: (te[i], 0, f_eff(i, f, nu))),
            pl.BlockSpec((None, tf, d), lambda i, f, te, nu, nv: (te[i], f_eff(i, f, nu), 0)),
        ],
        out_specs=pl.BlockSpec(memory_space=pl.ANY),
        scratch_shapes=[pltpu.VMEM((tm, d), F32), pltpu.VMEM((tm, d), BF16), pltpu.VMEM((tm, d), F32),
                        pltpu.VMEM((tm, d), F32), pltpu.SemaphoreType.DMA((2,))],
    )
    return pl.pallas_call(
        _moe_ffn_kernel,
        grid_spec=grid_spec,
        out_shape=jax.ShapeDtypeStruct((n_out_rows, d), F32),
        compiler_params=_cparams(("arbitrary", "arbitrary")),
        name="moe_grouped_ffn",
    )(tile_exp, n_used, n_valid, src, src, dst, x, shift, scale, wg, wu, wd)


def _moe_combine_kernel(x_ref, g_ref, info_ref, y1_ref, y2_ref, o_ref):
    info = info_ref[...]
    lane = lax.broadcasted_iota(jnp.int32, info.shape, 1)
    w1 = _lane_pick(info, lane, R_W1)
    w2 = _lane_pick(info, lane, R_W2)
    o_ref[...] = x_ref[...] + g_ref[...] * (w1 * y1_ref[...] + w2 * y2_ref[...])


def _moe_combine(x, gate, info, y, tm=512):
    m, d = x.shape
    nb = m // tm
    return pl.pallas_call(
        _moe_combine_kernel,
        grid=(nb,),
        in_specs=[
            pl.BlockSpec((tm, d), lambda i: (i, 0)),
            pl.BlockSpec((1, d), lambda i: (0, 0)),
            pl.BlockSpec((tm, LANES), lambda i: (i, 0)),
            pl.BlockSpec((tm, d), lambda i: (i, 0)),
            pl.BlockSpec((tm, d), lambda i: (nb + i, 0)),
        ],
        out_specs=pl.BlockSpec((tm, d), lambda i: (i, 0)),
        out_shape=jax.ShapeDtypeStruct((m, d), F32),
        compiler_params=_cparams(("parallel",)),
        name="moe_combine",
    )(x, gate, info, y, y)


def _moe_top2(x, shift, scale, gate, router, wg, wu, wd, tm=512):
    m, d = x.shape
    n_exp = router.shape[1]
    info, counts = _router(x, shift, scale, router)
    e1, e2 = info[:, R_E1].astype(jnp.int32), info[:, R_E2].astype(jnp.int32)
    r1, r2 = info[:, R_RANK1].astype(jnp.int32), info[:, R_RANK2].astype(jnp.int32)
    cnt = counts[0, :n_exp].astype(jnp.int32)
    padded = (cnt + tm - 1) // tm * tm
    ends = jnp.cumsum(padded)
    offs = ends - padded
    pos1, pos2 = offs[e1] + r1, offs[e2] + r2
    n_tiles = TOP_K * m // tm + n_exp
    n_rows = n_tiles * tm
    dst = jnp.full((n_rows,), -1, jnp.int32).at[jnp.concatenate([pos1, pos2])].set(
        jnp.arange(TOP_K * m, dtype=jnp.int32), unique_indices=True)
    src = jnp.where(dst >= 0, dst % m, 0)
    n_used = ends[-1] // tm
    tile_start = jnp.arange(n_tiles, dtype=jnp.int32) * tm
    tile_exp = jnp.searchsorted(ends, jnp.minimum(tile_start, ends[-1] - 1), side="right").astype(jnp.int32)
    tile_exp = jnp.minimum(tile_exp, n_exp - 1)
    n_valid = jnp.where(tile_start < ends[-1], jnp.clip(offs[tile_exp] + cnt[tile_exp] - tile_start, 0, tm), 0)
    y = _moe_ffn(x, shift, scale, tile_exp, n_used.reshape(1).astype(jnp.int32), n_valid.astype(jnp.int32),
                 src.reshape(n_tiles, 1, tm), dst.reshape(n_tiles, 1, tm), wg, wu, wd, TOP_K * m, tm)
    return _moe_combine(x, gate, info, y)


def _rope_tables(rows):
    n_freq = HEAD // 4
    inv = ROPE_THETA ** (-jnp.arange(n_freq, dtype=F32) / n_freq)
    ang_r = jnp.arange(rows, dtype=F32)[:, None] * inv
    ang_c = jnp.arange(GRID_W, dtype=F32)[:, None] * inv

    def full(fn):
        t = jnp.concatenate([jnp.broadcast_to(fn(ang_r)[:, None, :], (rows, GRID_W, n_freq)),
                             jnp.broadcast_to(fn(ang_c)[None, :, :], (rows, GRID_W, n_freq))], axis=-1)
        return t.reshape(rows * GRID_W, 2 * n_freq)

    cos, sin = full(jnp.cos), full(jnp.sin)
    return jnp.concatenate([cos, cos], axis=-1), jnp.concatenate([-sin, sin], axis=-1)


def _halves(a):
    shp = a.shape
    a = a.reshape(shp[:-1] + (shp[-1] // HEAD, HEAD // 2, 2))
    return jnp.swapaxes(a, -1, -2).reshape(shp)


def kernel(x, c, ctx, c_ctx, ada_w, ada_b, ev_w_in, ev_gate_b, ev_q_gain, ev_k_gain, ev_h_gain, ev_w_out,
           ev_ff_gate, ev_ff_up, ev_ff_down, od_w_in, od_q_gain, od_k_gain, od_lam_q1, od_lam_k1, od_lam_q2,
           od_lam_k2, od_sub_gain, od_w_out, od_router, od_ex_gate, od_ex_up, od_ex_down):
    bsz, s_len, d = x.shape
    n_ctx = ctx.shape[1]
    assert bsz == 1 and ada_w.shape[0] == 2
    a_heads = d // (2 * HEAD)
    a_kv = a_heads // GQA_GROUP
    b_heads = d // (2 * V_HEAD)
    c_heads = d // V_HEAD
    a_q, a_kvw, b_qk, b_v = a_heads * HEAD, a_kv * HEAD, b_heads * HEAD, b_heads * V_HEAD
    score_scale = HEAD ** -0.5 * math.log2(math.e)

    x_lat, x_ctx = x[0], ctx[0]
    cvecs = jnp.zeros((8, d), F32).at[0].set(c[0]).at[1].set(c_ctx)
    mods = _modulation(cvecs, ada_w, ada_b)

    def mod(layer, who):
        return [mods[layer, who, k * d:(k + 1) * d][None, :] for k in range(6)]

    cos_l, sin_l = _rope_tables(s_len // GRID_W)
    cos_c, sin_c = jnp.ones((n_ctx, HEAD), F32), jnp.zeros((n_ctx, HEAD), F32)

    w_in = ev_w_in[0]
    o0 = 0
    wqa = w_in[:, o0:o0 + a_q]; o0 += a_q
    wka = w_in[:, o0:o0 + a_kvw]; o0 += a_kvw
    wva = w_in[:, o0:o0 + a_kvw]; o0 += a_kvw
    wqb = w_in[:, o0:o0 + b_qk]; o0 += b_qk
    wkb = w_in[:, o0:o0 + b_qk]; o0 += b_qk
    wvb = w_in[:, o0:o0 + b_v]; o0 += b_v
    wob = w_in[:, o0:o0 + b_v]; o0 += b_v
    wgt = w_in[:, o0:]
    n_gates = wgt.shape[1]
    w0 = jnp.concatenate([_halves(wqa), wvb, wqb, wkb, _halves(wka), wva, wob], axis=1).astype(BF16)
    ones = lambda n: jnp.ones((n,), F32)
    cs0 = jnp.concatenate([
        jnp.tile(_halves(ev_q_gain[0]), a_heads) * score_scale, ones(b_v), ones(b_qk) * (HEAD ** -0.5), ones(b_qk),
        jnp.tile(_halves(ev_k_gain[0]), a_kv), ones(a_kvw), ones(b_v)])[None, :]
    modes0 = (["R"] * a_heads + ["P"] * (b_v // HEAD) + ["P"] * (2 * b_qk // HEAD) + ["R"] * a_kv
              + ["P"] * a_kv + ["P"] * (b_v // HEAD))
    nbf0 = a_q + b_v + 2 * b_qk + 2 * a_kvw
    wg0 = jnp.zeros((d, LANES), F32).at[:, :n_gates].set(wgt).astype(BF16)
    gb0 = jnp.zeros((1, LANES), F32).at[0, :n_gates].set(ev_gate_b[0])
    col_vb, col_qb, col_kb = a_q // b_v, (a_q + b_v) // b_qk, (a_q + b_v + b_qk) // b_qk
    col_ka = (a_q + b_v + 2 * b_qk) // HEAD
    col_va = col_ka + a_kv

    ml, mc = mod(0, 0), mod(0, 1)
    pl_b, pl_o, pl_g = _proj(x_lat, ml[0], ml[1], w0, cs0, cos_l, sin_l, modes0, nbf0, wg0, gb0)
    pc_b, pc_o, pc_g = _proj(x_ctx, mc[0], mc[1], w0, cs0, cos_c, sin_c, modes0, nbf0, wg0, gb0)
    p_all = jnp.concatenate([pl_b, pc_b], axis=0)
    g_all = jnp.concatenate([pl_g, pc_g], axis=0)
    g_row = g_all[:, :4 * b_heads].T

    ka_cols = slice(col_ka * HEAD, (col_ka + a_kv) * HEAD)
    att_l = _gqa_attention(pl_b, p_all[:, ka_cols].T, p_all, a_kv, col_va, tq=1024, tk=1280)
    att_c = _gqa_attention(pc_b, pc_b[:, ka_cols].T, pc_b, a_kv, col_va, tq=256, tk=256)
    hf, hb = _mlstm(p_all, g_all, g_row, b_heads, col_qb, col_kb, col_vb, s_len, n_ctx)

    hg = ev_h_gain[0].reshape(1, b_v)
    w_out0 = ev_w_out[0].astype(BF16)
    x_lat = _out_proj(x_lat, ml[2], w_out0, att_l, merge=(hf, hb, pl_o, hg, b_heads))
    x_ctx = _out_proj(x_ctx, mc[2], w_out0, att_c, merge=(hf, hb, pc_o, hg, b_heads), row_off=s_len)
    ffw = (ev_ff_gate[0].astype(BF16), ev_ff_up[0].astype(BF16), ev_ff_down[0].astype(BF16))
    x_lat = _ffn_dense(x_lat, ml[3], ml[4], ml[5], *ffw)
    x_ctx = _ffn_dense(x_ctx, mc[3], mc[4], mc[5], *ffw)

    c_qk = c_heads * 2 * HEAD
    w1 = od_w_in[0]
    w1 = jnp.concatenate([_halves(w1[:, :2 * c_qk]), w1[:, 2 * c_qk:]], axis=1).astype(BF16)
    cs1 = jnp.concatenate([jnp.tile(_halves(od_q_gain[0]), 2 * c_heads) * score_scale,
                           jnp.tile(_halves(od_k_gain[0]), 2 * c_heads), ones(c_heads * V_HEAD)])[None, :]
    modes1 = ["R"] * (4 * c_heads) + ["P"] * (c_heads * V_HEAD // HEAD)
    ml, mc = mod(1, 0), mod(1, 1)
    (ql,) = _proj(x_lat, ml[0], ml[1], w1, cs1, cos_l, sin_l, modes1, w1.shape[1])
    (qc,) = _proj(x_ctx, mc[0], mc[1], w1, cs1, cos_c, sin_c, modes1, w1.shape[1])
    kv_all = jnp.concatenate([ql, qc], axis=0)
    lam_init = 0.8 - 0.6 * math.exp(-0.3 * 1)
    vec = lambda a: a[0].reshape(1, -1)
    kt_all = kv_all[:, c_qk:2 * c_qk].T
    o_l = _diff_attention(ql, kt_all, kv_all, c_heads, lam_init, vec(od_lam_q1), vec(od_lam_k1), vec(od_lam_q2),
                          vec(od_lam_k2), vec(od_sub_gain), tq=2048, tk=1280)
    x_lat = _out_proj(x_lat, ml[2], od_w_out[0].astype(BF16), o_l)
    x_lat = _moe_top2(x_lat, ml[3], ml[4], ml[5], od_router[0], od_ex_gate[0].astype(BF16),
                      od_ex_up[0].astype(BF16), od_ex_down[0].astype(BF16))
    return x_lat[None]
```

```python
import functools
import math

import jax
import jax.numpy as jnp
from jax import lax
from jax.experimental import pallas as pl
from jax.experimental.pallas import tpu as pltpu

F32 = jnp.float32
BF16 = jnp.bfloat16

GRID_W = 64
ROPE_THETA = 10000.0
EPS = 1e-6
M_INIT = -1e30
HEAD = 128
V_HEAD = 256
GQA_GROUP = 4
TOP_K = 2
MLSTM_CHUNK = 256
GQA_SUB_ROWS = 256
DIFF_SUB_ROWS = 256
GATHER_UNROLL = 8
COMBINE_ROWS = 256
PIPE_SKEW = 1
SOFTMAX_ROW_GROUP = 64
LANES = 128
VMEM_LIMIT = 56 * 1024 * 1024
NEG_INF = float("-inf")


def _cparams(sem, flags=None):
    return pltpu.CompilerParams(dimension_semantics=sem, vmem_limit_bytes=VMEM_LIMIT, flags=flags)


ATTN_FLAGS = None


def _dot(a, b):
    return jnp.dot(a, b, preferred_element_type=F32)


def _dot_nt(a, b):
    return lax.dot_general(a, b, (((1,), (1,)), ((), ())), preferred_element_type=F32)


def _dot_tn(a, b):
    return lax.dot_general(a, b, (((0,), (0,)), ((), ())), preferred_element_type=F32)


def _rms(x):
    return x * lax.rsqrt(jnp.mean(x * x, axis=-1, keepdims=True) + EPS)


def _mod_kernel(c_ref, w_ref, b_ref, o_ref):
    c = c_ref[...]
    s = c * jax.nn.sigmoid(c)
    o_ref[...] = _dot(s.astype(BF16), w_ref[...].astype(BF16)) + b_ref[...]


def _modulation(cvecs, ada_w, ada_b):
    depth, d, n = ada_w.shape
    tn = 1024
    return pl.pallas_call(
        _mod_kernel,
        grid=(depth, n // tn),
        in_specs=[
            pl.BlockSpec((8, d), lambda l, j: (0, 0)),
            pl.BlockSpec((None, d, tn), lambda l, j: (l, 0, j)),
            pl.BlockSpec((None, 1, tn), lambda l, j: (l, 0, j)),
        ],
        out_specs=pl.BlockSpec((None, 8, tn), lambda l, j: (l, 0, j)),
        out_shape=jax.ShapeDtypeStruct((depth, 8, n), F32),
        compiler_params=_cparams(("parallel", "parallel")),
        name="adaln_mod",
    )(cvecs, ada_w, ada_b.reshape(depth, 1, n))


def _proj_kernel(patterns, n_bf_tiles, has_f32, has_gates, tn, *refs):
    it = iter(refs)
    x_ref, sh_ref, sc_ref, w_ref, cs_ref, cos_ref, sin_ref = (next(it) for _ in range(7))
    if has_gates:
        wg_ref, gb_ref = next(it), next(it)
    ob_ref = next(it)
    of_ref = next(it) if has_f32 else None
    og_ref = next(it) if has_gates else None
    h_scr = next(it)

    j = pl.program_id(1)

    @pl.when(j == 0)
    def _():
        h = _rms(x_ref[...]) * (1.0 + sc_ref[...]) + sh_ref[...]
        h_scr[...] = h.astype(BF16)
        if has_gates:
            og_ref[...] = _dot(h_scr[...], wg_ref[...]) + gb_ref[...]

    acc = _dot(h_scr[...], w_ref[...])
    cs = cs_ref[...]

    for lo, hi, pat in patterns:
        @pl.when(jnp.logical_and(j >= lo, j < hi))
        def _(pat=pat, lo=lo):
            to_f32 = has_f32 and lo >= n_bf_tiles
            out_ref = of_ref if to_f32 else ob_ref
            for hidx, mode in enumerate(pat):
                sl = slice(hidx * HEAD, (hidx + 1) * HEAD)
                y = acc[:, sl]
                if mode == "R":
                    y = _rms(y) * cs[:, sl]
                    y = y * cos_ref[...] + pltpu.roll(y, HEAD // 2, 1) * sin_ref[...]
                else:
                    y = y * cs[:, sl]
                out_ref[:, sl] = y.astype(out_ref.dtype)


def _proj(x, shift, scale, w, colscale, cos2, sin2, head_modes, n_bf_cols, wg=None, gb=None, tm=1024, tn=512):
    m, d = x.shape
    n = w.shape[1]
    tm = min(tm, m)
    assert m % tm == 0 and n % tn == 0 and n_bf_cols % tn == 0
    nj = n // tn
    n_bf_tiles = n_bf_cols // tn
    has_f32 = n_bf_cols < n
    has_gates = wg is not None
    hpt = tn // HEAD
    tiles = [tuple(head_modes[t * hpt:(t + 1) * hpt]) for t in range(nj)]
    patterns = []
    for t, pat in enumerate(tiles):
        if patterns and patterns[-1][2] == pat and not (has_f32 and t == n_bf_tiles):
            patterns[-1] = (patterns[-1][0], t + 1, pat)
        else:
            patterns.append((t, t + 1, pat))

    in_specs = [
        pl.BlockSpec((tm, d), lambda i, j: (i, 0)),
        pl.BlockSpec((1, d), lambda i, j: (0, 0)),
        pl.BlockSpec((1, d), lambda i, j: (0, 0)),
        pl.BlockSpec((d, tn), lambda i, j: (0, j)),
        pl.BlockSpec((1, tn), lambda i, j: (0, j)),
        pl.BlockSpec((tm, HEAD), lambda i, j: (i, 0)),
        pl.BlockSpec((tm, HEAD), lambda i, j: (i, 0)),
    ]
    args = [x, shift, scale, w, colscale, cos2, sin2]
    if has_gates:
        in_specs += [pl.BlockSpec((d, LANES), lambda i, j: (0, 0)), pl.BlockSpec((1, LANES), lambda i, j: (0, 0))]
        args += [wg, gb]
    last_bf = n_bf_tiles - 1
    out_specs = [pl.BlockSpec((tm, tn), lambda i, j: (i, jnp.minimum(j, last_bf)))]
    out_shape = [jax.ShapeDtypeStruct((m, n_bf_cols), BF16)]
    if has_f32:
        out_specs.append(pl.BlockSpec((tm, tn), lambda i, j: (i, jnp.maximum(j - n_bf_tiles, 0))))
        out_shape.append(jax.ShapeDtypeStruct((m, n - n_bf_cols), F32))
    if has_gates:
        out_specs.append(pl.BlockSpec((tm, LANES), lambda i, j: (i, 0)))
        out_shape.append(jax.ShapeDtypeStruct((m, LANES), F32))
    return pl.pallas_call(
        functools.partial(_proj_kernel, tuple(patterns), n_bf_tiles, has_f32, has_gates, tn),
        grid=(m // tm, nj),
        in_specs=in_specs,
        out_specs=out_specs,
        out_shape=out_shape,
        scratch_shapes=[pltpu.VMEM((tm, d), BF16)],
        compiler_params=_cparams(("parallel", "arbitrary")),
        name="norm_mod_proj",
    )(*args)


def _softmax_pipeline(score_fns, v, row_slices, m_scr, l_scr, acc_scr, s_scr, p_scr):
    n = len(score_fns)
    alphas = {}
    k = PIPE_SKEW
    for t in range(n + 2 * k):
        if t < n:
            s_scr[t] = score_fns[t]()
        c = t - k
        if 0 <= c < n:
            alphas[c] = _softmax_probs(row_slices[c], m_scr, l_scr, s_scr.at[c], p_scr.at[c])
        c = t - 2 * k
        if 0 <= c < n:
            rows = row_slices[c]
            acc_scr[rows, :] = alphas.pop(c) * acc_scr[rows, :] + _dot(p_scr[c], v)


def _softmax_probs(rows, m_scr, l_scr, s_buf, p_buf):
    sub, tk = s_buf.shape
    n_slab = tk // LANES
    mx = s_buf[:, 0:LANES]
    for c in range(1, n_slab):
        mx = jnp.maximum(mx, s_buf[:, c * LANES:(c + 1) * LANES])
    m_prev = m_scr[rows, :]
    m_new = jnp.maximum(m_prev, jnp.max(mx, axis=1, keepdims=True))
    alpha = jnp.exp2(m_prev - m_new)
    m_scr[rows, :] = m_new
    rg = min(sub, SOFTMAX_ROW_GROUP)
    for g in range(sub // rg):
        gr = slice(g * rg, (g + 1) * rg)
        mb = jnp.broadcast_to(m_new[gr, :], (rg, LANES))
        lsum = jnp.zeros((rg, LANES), F32)
        for c in range(n_slab):
            cs = slice(c * LANES, (c + 1) * LANES)
            p = jnp.exp2(s_buf[gr, cs] - mb)
            if l_scr is not None:
                lsum = lsum + p
            p_buf[gr, cs] = p.astype(BF16)
        if l_scr is not None:
            lr = slice(rows.start + g * rg, rows.start + (g + 1) * rg)
            l_scr[lr, :] = alpha[gr, :] * l_scr[lr, :] + lsum
    return alpha


def _gqa_kernel(tq, q_ref, kt_ref, v_ref, o_ref, m_scr, acc_scr, s_scr, p_scr):
    kj = pl.program_id(2)

    @pl.when(kj == 0)
    def _():
        m_scr[...] = jnp.full(m_scr.shape, NEG_INF, F32)
        acc_scr[...] = jnp.zeros(acc_scr.shape, F32)

    kt = kt_ref[...]
    v = jnp.concatenate([v_ref[...], jnp.ones(v_ref.shape, BF16)], axis=1)
    sub = s_scr.shape[1]
    nb = tq // sub
    fns, rows = [], []
    for g in range(GQA_GROUP):
        for b in range(nb):
            fns.append(lambda g=g, b=b: _dot(q_ref[b * sub:(b + 1) * sub, g * HEAD:(g + 1) * HEAD], kt))
            rows.append(slice(g * tq + b * sub, g * tq + (b + 1) * sub))
    _softmax_pipeline(fns, v, rows, m_scr, None, acc_scr, s_scr, p_scr)

    @pl.when(kj == pl.num_programs(2) - 1)
    def _():
        out = acc_scr[:, :HEAD] / acc_scr[:, HEAD:HEAD + 1]
        for g in range(GQA_GROUP):
            o_ref[:, g * HEAD:(g + 1) * HEAD] = out[g * tq:(g + 1) * tq, :].astype(o_ref.dtype)


def _gqa_attention(q_arr, kt_arr, kv_arr, n_kv, v_col, tq, tk):
    s_len = q_arr.shape[0]
    lk = kv_arr.shape[0]
    tq = min(tq, s_len)
    tk = min(tk, lk)
    assert s_len % tq == 0 and lk % tk == 0 and tk % LANES == 0
    gw = GQA_GROUP * HEAD
    sub = min(tq, GQA_SUB_ROWS)
    n_chain = GQA_GROUP * (tq // sub)
    return pl.pallas_call(
        functools.partial(_gqa_kernel, tq),
        grid=(n_kv, s_len // tq, lk // tk),
        in_specs=[
            pl.BlockSpec((tq, gw), lambda n, i, j: (i, n)),
            pl.BlockSpec((HEAD, tk), lambda n, i, j: (n, j)),
            pl.BlockSpec((tk, HEAD), lambda n, i, j: (j, v_col + n)),
        ],
        out_specs=pl.BlockSpec((tq, gw), lambda n, i, j: (i, n)),
        out_shape=jax.ShapeDtypeStruct((s_len, n_kv * gw), BF16),
        scratch_shapes=[
            pltpu.VMEM((GQA_GROUP * tq, 1), F32),
            pltpu.VMEM((GQA_GROUP * tq, 2 * HEAD), F32),
            pltpu.VMEM((n_chain, sub, tk), F32),
            pltpu.VMEM((n_chain, sub, tk), BF16),
        ],
        compiler_params=_cparams(("parallel", "parallel", "arbitrary"), ATTN_FLAGS),
        name="gqa_flash",
    )(q_arr, kt_arr, kv_arr)


def _diff_kernel(tq, lam_init, q_ref, kt_ref, v_ref, lq1_ref, lk1_ref, lq2_ref, lk2_ref, sg_ref, o_ref,
                 m_scr, l_scr, acc_scr, s_scr, p_scr):
    kj = pl.program_id(2)

    @pl.when(kj == 0)
    def _():
        m_scr[...] = jnp.full(m_scr.shape, NEG_INF, F32)
        l_scr[...] = jnp.zeros(l_scr.shape, F32)
        acc_scr[...] = jnp.zeros(acc_scr.shape, F32)

    v = v_ref[...]
    sub = s_scr.shape[1]
    nb = tq // sub
    fns, rows = [], []
    for b in range(nb):
        for r in range(2):
            fns.append(lambda r=r, b=b: _dot(q_ref[b * sub:(b + 1) * sub, r * HEAD:(r + 1) * HEAD],
                                             kt_ref[r * HEAD:(r + 1) * HEAD, :]))
            rows.append(slice(r * tq + b * sub, r * tq + (b + 1) * sub))
    _softmax_pipeline(fns, v, rows, m_scr, l_scr, acc_scr, s_scr, p_scr)

    @pl.when(kj == pl.num_programs(2) - 1)
    def _():
        lam = (jnp.exp(jnp.sum(lq1_ref[...] * lk1_ref[...], axis=1, keepdims=True))
               - jnp.exp(jnp.sum(lq2_ref[...] * lk2_ref[...], axis=1, keepdims=True)) + lam_init)
        o = acc_scr[...] / jnp.sum(l_scr[...], axis=1, keepdims=True)
        o = o[:tq, :] - lam * o[tq:, :]
        o = _rms(o) * sg_ref[...] * (1.0 - lam_init)
        o_ref[...] = o.astype(o_ref.dtype)


def _diff_attention(q_arr, kt_arr, kv_arr, n_heads, lam_init, lq1, lk1, lq2, lk2, sub_gain, tq, tk):
    s_len = q_arr.shape[0]
    lk = kv_arr.shape[0]
    tq = min(tq, s_len)
    tk = min(tk, lk)
    assert s_len % tq == 0 and lk % tk == 0 and tk % LANES == 0
    sub = min(tq, DIFF_SUB_ROWS)
    n_chain = 2 * (tq // sub)
    vec = pl.BlockSpec((1, HEAD), lambda h, i, j: (0, 0))
    return pl.pallas_call(
        functools.partial(_diff_kernel, tq, lam_init),
        grid=(n_heads, s_len // tq, lk // tk),
        in_specs=[
            pl.BlockSpec((tq, V_HEAD), lambda h, i, j: (i, h)),
            pl.BlockSpec((V_HEAD, tk), lambda h, i, j: (h, j)),
            pl.BlockSpec((tk, V_HEAD), lambda h, i, j: (j, 2 * n_heads + h)),
            vec, vec, vec, vec,
            pl.BlockSpec((1, V_HEAD), lambda h, i, j: (0, 0)),
        ],
        out_specs=pl.BlockSpec((tq, V_HEAD), lambda h, i, j: (i, h)),
        out_shape=jax.ShapeDtypeStruct((s_len, n_heads * V_HEAD), BF16),
        scratch_shapes=[
            pltpu.VMEM((2 * tq, 1), F32),
            pltpu.VMEM((2 * tq, LANES), F32),
            pltpu.VMEM((2 * tq, V_HEAD), F32),
            pltpu.VMEM((n_chain, sub, tk), F32),
            pltpu.VMEM((n_chain, sub, tk), BF16),
        ],
        compiler_params=_cparams(("parallel", "parallel", "arbitrary"), ATTN_FLAGS),
        name="diff_flash",
    )(q_arr, kt_arr, kv_arr, lq1, lk1, lq2, lk2, sub_gain)


def _split3(x):
    hi = x.astype(BF16)
    r1 = x - hi.astype(F32)
    mid = r1.astype(BF16)
    lo = (r1 - mid.astype(F32)).astype(BF16)
    return hi, mid, lo


def _log_sigmoid(x):
    return jnp.minimum(x, 0.0) - jnp.log(1.0 + jnp.exp(-jnp.abs(x)))


def _mlstm_kernel(nh, chunk, qf_ref, kf_ref, vf_ref, gcf_ref, grf_ref, qb_ref, kb_ref, vb_ref, gcb_ref, grb_ref,
                  hf_ref, hb_ref, c_scr, n_scr, m_scr):
    t = pl.program_id(0)

    @pl.when(t == 0)
    def _():
        c_scr[...] = jnp.zeros(c_scr.shape, F32)
        n_scr[...] = jnp.zeros(n_scr.shape, F32)
        m_scr[...] = jnp.full(m_scr.shape, M_INIT, F32)

    row = lax.broadcasted_iota(jnp.int32, (chunk, chunk), 0)
    col = lax.broadcasted_iota(jnp.int32, (chunk, chunk), 1)
    lower = col <= row
    upper = col >= row
    lower_b = lower.astype(BF16)
    upper_b = upper.astype(BF16)

    for rev in (False, True):
        q_ref, k_ref, v_ref, gc_ref, gr_ref, h_ref = (
            (qb_ref, kb_ref, vb_ref, gcb_ref, grb_ref, hb_ref) if rev else
            (qf_ref, kf_ref, vf_ref, gcf_ref, grf_ref, hf_ref))
        mask = upper if rev else lower
        gcol = gc_ref[...]
        grow = gr_ref[...]
        cum_col = sum(_dot(upper_b if rev else lower_b, piece) for piece in _split3(_log_sigmoid(gcol)))
        cum_row = sum(_dot(piece, lower_b if rev else upper_b) for piece in _split3(_log_sigmoid(grow)))
        for hd in range(nh):
            idx = (1 if rev else 0) * nh + hd
            ki = (2 if rev else 0) * nh + hd
            kf = ki + nh
            qc = q_ref[:, hd * HEAD:(hd + 1) * HEAD]
            kc = k_ref[:, hd * HEAD:(hd + 1) * HEAD]
            vc = v_ref[:, hd * V_HEAD:(hd + 1) * V_HEAD]
            i_col = gcol[:, ki:ki + 1]
            i_row = grow[ki:ki + 1, :]
            b_col = cum_col[:, kf:kf + 1]
            b_row = cum_row[kf:kf + 1, :]
            c_mem = c_scr[idx]
            n_mem = n_scr[idx]
            m = m_scr[idx][:, :1]

            dlog = jnp.where(mask, b_col + (i_row - b_row), NEG_INF)
            inter = b_col + m
            m_out = jnp.maximum(inter, jnp.max(dlog, axis=1, keepdims=True))
            w_intra = jnp.exp(dlog - m_out)
            w_inter = jnp.exp(inter - m_out)
            qk = _dot_nt(qc, kc) * w_intra
            num = w_inter * _dot(qc, c_mem.astype(BF16)) + _dot(qk.astype(BF16), vc)
            den = (w_inter * jnp.sum(qc.astype(F32) * n_mem, axis=1, keepdims=True)
                   + jnp.sum(qk, axis=1, keepdims=True))
            h = num / jnp.maximum(jnp.abs(den), jnp.exp(-m_out))
            h_ref[:, hd * V_HEAD:(hd + 1) * V_HEAD] = h

            b_last = b_col[0:1, :] if rev else b_col[chunk - 1:chunk, :]
            g_col = b_last - b_col + i_col
            m_new = jnp.maximum(b_last + m, jnp.max(g_col, axis=0, keepdims=True))
            a_dec = jnp.exp(b_last + m - m_new)
            kw = kc.astype(F32) * jnp.exp(g_col - m_new)
            c_scr[idx] = a_dec * c_mem + _dot_tn(kw.astype(BF16), vc)
            n_scr[idx] = a_dec * n_mem + jnp.sum(kw, axis=0, keepdims=True)
            m_scr[idx] = jnp.broadcast_to(m_new, (1, LANES))


def _mlstm(qkv, gates_col, gates_row, nh, q_col, k_col, v_col, n_lat, n_ctx):
    t_len = qkv.shape[0]
    chunk = MLSTM_CHUNK
    assert n_lat % chunk == 0 and n_ctx % chunk == 0 and t_len == n_lat + n_ctx
    nlat, nctx = n_lat // chunk, n_ctx // chunk
    nc = nlat + nctx

    def fwd(t):
        return jnp.where(t < nctx, nlat + t, t - nctx)

    def bwd(t):
        return jnp.where(t < nctx, nc - 1 - t, nlat - 1 - (t - nctx))

    qw, vw = nh * HEAD, nh * V_HEAD

    def specs(order):
        return [
            pl.BlockSpec((chunk, qw), lambda t: (order(t), q_col)),
            pl.BlockSpec((chunk, qw), lambda t: (order(t), k_col)),
            pl.BlockSpec((chunk, vw), lambda t: (order(t), v_col)),
            pl.BlockSpec((chunk, LANES), lambda t: (order(t), 0)),
            pl.BlockSpec((4 * nh, chunk), lambda t: (0, order(t))),
        ]

    return pl.pallas_call(
        functools.partial(_mlstm_kernel, nh, chunk),
        grid=(nc,),
        in_specs=specs(fwd) + specs(bwd),
        out_specs=[pl.BlockSpec((chunk, vw), lambda t: (fwd(t), 0)),
                   pl.BlockSpec((chunk, vw), lambda t: (bwd(t), 0))],
        out_shape=[jax.ShapeDtypeStruct((t_len, vw), F32)] * 2,
        scratch_shapes=[
            pltpu.VMEM((2 * nh, HEAD, V_HEAD), F32),
            pltpu.VMEM((2 * nh, 1, HEAD), F32),
            pltpu.VMEM((2 * nh, 1, LANES), F32),
        ],
        compiler_params=_cparams(("arbitrary",)),
        name="mlstm_scan",
    )(qkv, qkv, qkv, gates_col, gates_row, qkv, qkv, qkv, gates_col, gates_row)


def _out_kernel(merge, nh, *refs):
    if merge:
        a_ref, hf_ref, hb_ref, o_ref, hg_ref, w_ref, x_ref, g_ref, y_ref, mix_scr = refs
        aw = a_ref.shape[1]

        @pl.when(pl.program_id(1) == 0)
        def _():
            mix_scr[:, :aw] = a_ref[...]
            for hd in range(nh):
                sl = slice(hd * V_HEAD, (hd + 1) * V_HEAD)
                hm = _rms(hf_ref[:, sl] + hb_ref[:, sl]) * hg_ref[:, sl] * jax.nn.sigmoid(o_ref[:, sl])
                mix_scr[:, aw + hd * V_HEAD:aw + (hd + 1) * V_HEAD] = hm.astype(BF16)

        lhs = mix_scr[...]
    else:
        a_ref, w_ref, x_ref, g_ref, y_ref = refs
        lhs = a_ref[...]
    y_ref[...] = x_ref[...] + g_ref[...] * _dot(lhs, w_ref[...])


def _out_proj(x, gate, w, a, merge=None, row_off=0, tm=512, tn=1024):
    m, d = x.shape
    kdim, n = w.shape
    tm = min(tm, m)
    assert m % tm == 0 and n % tn == 0 and row_off % tm == 0
    off = row_off // tm
    in_specs = [pl.BlockSpec((tm, a.shape[1]), lambda i, j: (i, 0))]
    args = [a]
    scratch = []
    nh = 0
    if merge is not None:
        hf, hb, o, hg, nh = merge
        hw = hf.shape[1]
        in_specs += [pl.BlockSpec((tm, hw), lambda i, j: (i + off, 0)),
                     pl.BlockSpec((tm, hw), lambda i, j: (i + off, 0)),
                     pl.BlockSpec((tm, hw), lambda i, j: (i, 0)),
                     pl.BlockSpec((1, hw), lambda i, j: (0, 0))]
        args += [hf, hb, o, hg]
        scratch = [pltpu.VMEM((tm, kdim), BF16)]
    in_specs += [pl.BlockSpec((kdim, tn), lambda i, j: (0, j)),
                 pl.BlockSpec((tm, tn), lambda i, j: (i, j)),
                 pl.BlockSpec((1, tn), lambda i, j: (0, j))]
    args += [w, x, gate]
    return pl.pallas_call(
        functools.partial(_out_kernel, merge is not None, nh),
        grid=(m // tm, n // tn),
        in_specs=in_specs,
        out_specs=pl.BlockSpec((tm, tn), lambda i, j: (i, j)),
        out_shape=jax.ShapeDtypeStruct((m, n), F32),
        scratch_shapes=scratch,
        compiler_params=_cparams(("parallel", "arbitrary")),
        name="out_proj_residual",
    )(*args)


def _ffn_kernel(x_ref, sh_ref, sc_ref, g_ref, wg_ref, wu_ref, wd_ref, y_ref, h_scr, acc_scr):
    f = pl.program_id(1)

    @pl.when(f == 0)
    def _():
        h = _rms(x_ref[...]) * (1.0 + sc_ref[...]) + sh_ref[...]
        h_scr[...] = h.astype(BF16)
        acc_scr[...] = jnp.zeros(acc_scr.shape, F32)

    h = h_scr[...]
    gt = _dot(h, wg_ref[...])
    up = _dot(h, wu_ref[...])
    act = (gt * jax.nn.sigmoid(gt)) * up
    acc_scr[...] += _dot(act.astype(BF16), wd_ref[...])

    @pl.when(f == pl.num_programs(1) - 1)
    def _():
        y_ref[...] = x_ref[...] + g_ref[...] * acc_scr[...]


def _ffn_dense(x, shift, scale, gate, wg, wu, wd, tm=512, tf=512):
    m, d = x.shape
    ff = wg.shape[1]
    tm = min(tm, m)
    assert m % tm == 0 and ff % tf == 0
    vec = pl.BlockSpec((1, d), lambda i, f: (0, 0))
    return pl.pallas_call(
        _ffn_kernel,
        grid=(m // tm, ff // tf),
        in_specs=[
            pl.BlockSpec((tm, d), lambda i, f: (i, 0)), vec, vec, vec,
            pl.BlockSpec((d, tf), lambda i, f: (0, f)),
            pl.BlockSpec((d, tf), lambda i, f: (0, f)),
            pl.BlockSpec((tf, d), lambda i, f: (f, 0)),
        ],
        out_specs=pl.BlockSpec((tm, d), lambda i, f: (i, 0)),
        out_shape=jax.ShapeDtypeStruct((m, d), F32),
        scratch_shapes=[pltpu.VMEM((tm, d), BF16), pltpu.VMEM((tm, d), F32)],
        compiler_params=_cparams(("parallel", "arbitrary")),
        name="ffn_dense",
    )(x, shift, scale, gate, wg, wu, wd)


R_E1, R_E2, R_W1, R_W2, R_RANK1, R_RANK2 = range(6)


def _lane_pick(a, lane, k):
    return jnp.sum(jnp.where(lane == k, a, 0.0), axis=1, keepdims=True)


def _router_kernel(n_exp, x_ref, sh_ref, sc_ref, r_ref, info_ref, cnt_ref, carry_scr):
    @pl.when(pl.program_id(0) == 0)
    def _():
        carry_scr[...] = jnp.zeros(carry_scr.shape, F32)

    tm = x_ref.shape[0]
    h = _rms(x_ref[...]) * (1.0 + sc_ref[...]) + sh_ref[...]
    logits = jnp.dot(h, r_ref[...], preferred_element_type=F32, precision=lax.Precision.HIGHEST)
    lane = lax.broadcasted_iota(jnp.int32, logits.shape, 1)
    logits = jnp.where(lane < n_exp, logits, NEG_INF)
    v1 = jnp.max(logits, axis=1, keepdims=True)
    i1 = jnp.min(jnp.where(logits == v1, lane, LANES), axis=1, keepdims=True)
    rest = jnp.where(lane == i1, NEG_INF, logits)
    v2 = jnp.max(rest, axis=1, keepdims=True)
    i2 = jnp.min(jnp.where(rest == v2, lane, LANES), axis=1, keepdims=True)
    e2 = jnp.exp(v2 - v1)
    w1 = 1.0 / (1.0 + e2)
    w2 = e2 / (1.0 + e2)

    chosen = jnp.where(jnp.logical_or(lane == i1, lane == i2), 1.0, 0.0)
    r_i = lax.broadcasted_iota(jnp.int32, (tm, tm), 0)
    c_i = lax.broadcasted_iota(jnp.int32, (tm, tm), 1)
    before = (c_i < r_i).astype(BF16)
    rank = _dot(before, chosen.astype(BF16)) + carry_scr[...]
    carry_scr[...] += jnp.sum(chosen, axis=0, keepdims=True)
    cnt_ref[...] = carry_scr[...]
    rank1 = jnp.sum(jnp.where(lane == i1, rank, 0.0), axis=1, keepdims=True)
    rank2 = jnp.sum(jnp.where(lane == i2, rank, 0.0), axis=1, keepdims=True)
    info = jnp.zeros(logits.shape, F32)
    for k, val in ((R_E1, i1.astype(F32)), (R_E2, i2.astype(F32)), (R_W1, w1), (R_W2, w2),
                   (R_RANK1, rank1), (R_RANK2, rank2)):
        info = jnp.where(lane == k, val, info)
    info_ref[...] = info


def _router(x, shift, scale, router, tm=512):
    m, d = x.shape
    n_exp = router.shape[1]
    vec = pl.BlockSpec((1, d), lambda i: (0, 0))
    router_pad = jnp.zeros((d, LANES), F32).at[:, :n_exp].set(router)
    return pl.pallas_call(
        functools.partial(_router_kernel, n_exp),
        grid=(m // tm,),
        in_specs=[pl.BlockSpec((tm, d), lambda i: (i, 0)), vec, vec, pl.BlockSpec((d, LANES), lambda i: (0, 0))],
        out_specs=[pl.BlockSpec((tm, LANES), lambda i: (i, 0)), pl.BlockSpec((1, LANES), lambda i: (0, 0))],
        out_shape=[jax.ShapeDtypeStruct((m, LANES), F32), jax.ShapeDtypeStruct((1, LANES), F32)],
        scratch_shapes=[pltpu.VMEM((1, LANES), F32)],
        compiler_params=_cparams(("arbitrary",)),
        name="moe_router",
    )(x, shift, scale, router_pad)


def _gather_rows(src_hbm, idx_ref, buf, sem, start):
    n_rows = buf.shape[0]
    if not start:
        pltpu.make_async_copy(src_hbm.at[pl.ds(0, n_rows), :], buf, sem).wait()
        return

    def body(r, carry):
        pltpu.make_async_copy(src_hbm.at[pl.ds(idx_ref[0, r], 1), :], buf.at[pl.ds(r, 1), :], sem).start()
        return carry
    lax.fori_loop(0, n_rows, body, 0, unroll=GATHER_UNROLL)


def _moe_ffn_kernel(te_ref, nu_ref, src_ref, nsrc_ref, x_hbm, sh_ref, sc_ref, wg_ref, wu_ref, wd_ref, y_ref,
                    xbuf, hbuf, sem):
    i = pl.program_id(0)
    f = pl.program_id(1)
    n_used = nu_ref[0]
    active = i < n_used

    @pl.when(f == 0)
    def _():
        y_ref[...] = jnp.zeros(y_ref.shape, F32)

    @pl.when(jnp.logical_and(active, f == 0))
    def _():
        pl.when(i == 0)(lambda: _gather_rows(x_hbm, src_ref, xbuf, sem, True))
        _gather_rows(x_hbm, src_ref, xbuf, sem, False)
        h = _rms(xbuf[...]) * (1.0 + sc_ref[...]) + sh_ref[...]
        hbuf[...] = h.astype(BF16)
        pl.when(i + 1 < n_used)(lambda: _gather_rows(x_hbm, nsrc_ref, xbuf, sem, True))

    @pl.when(active)
    def _():
        h = hbuf[...]
        gt = _dot(h, wg_ref[...])
        up = _dot(h, wu_ref[...])
        act = (gt * jax.nn.sigmoid(gt)) * up
        y_ref[...] += _dot(act.astype(BF16), wd_ref[...])


def _moe_ffn(x, shift, scale, tile_exp, n_used, src, wg, wu, wd, tm, tf=1024):
    m, d = x.shape
    n_exp, _, ff = wg.shape
    n_tiles = src.shape[0]
    tf = min(tf, ff)
    assert ff % tf == 0
    nf = ff // tf
    vec = pl.BlockSpec((1, d), lambda i, f, te, nu: (0, 0))
    idx = pl.BlockSpec((None, 1, tm), lambda i, f, te, nu: (i, 0, 0), memory_space=pltpu.SMEM)
    idx_next = pl.BlockSpec((None, 1, tm), lambda i, f, te, nu: (jnp.minimum(i + 1, n_tiles - 1), 0, 0),
                            memory_space=pltpu.SMEM)

    def f_eff(i, f, nu):
        return jnp.where(i < nu[0], f, nf - 1)

    grid_spec = pltpu.PrefetchScalarGridSpec(
        num_scalar_prefetch=2,
        grid=(n_tiles, nf),
        in_specs=[
            idx, idx_next,
            pl.BlockSpec(memory_space=pl.ANY),
            vec, vec,
            pl.BlockSpec((None, d, tf), lambda i, f, te, nu: (te[i], 0, f_eff(i, f, nu))),
            pl.BlockSpec((None, d, tf), lambda i, f, te, nu: (te[i], 0, f_eff(i, f, nu))),
            pl.BlockSpec((None, tf, d), lambda i, f, te, nu: (te[i], f_eff(i, f, nu), 0)),
        ],
        out_specs=pl.BlockSpec((tm, d), lambda i, f, te, nu: (i, 0)),
        scratch_shapes=[pltpu.VMEM((tm, d), F32), pltpu.VMEM((tm, d), BF16), pltpu.SemaphoreType.DMA(())],
    )
    return pl.pallas_call(
        _moe_ffn_kernel,
        grid_spec=grid_spec,
        out_shape=jax.ShapeDtypeStruct((n_tiles * tm, d), F32),
        compiler_params=_cparams(("arbitrary", "arbitrary")),
        name="moe_grouped_ffn",
    )(tile_exp, n_used, src, src, x, shift, scale, wg, wu, wd)


def _moe_combine_kernel(pos_ref, npos_ref, x_ref, g_ref, info_ref, y_hbm, o_ref, ybuf, sem):
    i = pl.program_id(0)
    tm = x_ref.shape[0]
    slot = i % 2
    pl.when(i == 0)(lambda: _gather_rows(y_hbm, pos_ref, ybuf.at[0], sem.at[0], True))
    for s in range(2):
        @pl.when(slot == s)
        def _(s=s):
            _gather_rows(y_hbm, pos_ref, ybuf.at[s], sem.at[s], False)
            pl.when(i + 1 < pl.num_programs(0))(
                lambda: _gather_rows(y_hbm, npos_ref, ybuf.at[1 - s], sem.at[1 - s], True))
            info = info_ref[...]
            lane = lax.broadcasted_iota(jnp.int32, info.shape, 1)
            w1 = _lane_pick(info, lane, R_W1)
            w2 = _lane_pick(info, lane, R_W2)
            o_ref[...] = x_ref[...] + g_ref[...] * (w1 * ybuf[s, :tm, :] + w2 * ybuf[s, tm:, :])


def _moe_combine(x, gate, info, y, pos, tm=256):
    m, d = x.shape
    nb = m // tm
    assert m % tm == 0 and pos.shape == (nb, 1, 2 * tm)
    idx = pl.BlockSpec((None, 1, 2 * tm), lambda i: (i, 0, 0), memory_space=pltpu.SMEM)
    idx_next = pl.BlockSpec((None, 1, 2 * tm), lambda i: (jnp.minimum(i + 1, nb - 1), 0, 0), memory_space=pltpu.SMEM)
    return pl.pallas_call(
        _moe_combine_kernel,
        grid=(nb,),
        in_specs=[
            idx, idx_next,
            pl.BlockSpec((tm, d), lambda i: (i, 0)),
            pl.BlockSpec((1, d), lambda i: (0, 0)),
            pl.BlockSpec((tm, LANES), lambda i: (i, 0)),
            pl.BlockSpec(memory_space=pl.ANY),
        ],
        out_specs=pl.BlockSpec((tm, d), lambda i: (i, 0)),
        out_shape=jax.ShapeDtypeStruct((m, d), F32),
        scratch_shapes=[pltpu.VMEM((2, 2 * tm, d), F32), pltpu.SemaphoreType.DMA((2,))],
        compiler_params=_cparams(("arbitrary",)),
        name="moe_combine",
    )(pos, pos, x, gate, info, y)


def _moe_top2(x, shift, scale, gate, router, wg, wu, wd, tm=512):
    m, d = x.shape
    n_exp = router.shape[1]
    info, counts = _router(x, shift, scale, router)
    e1, e2 = info[:, R_E1].astype(jnp.int32), info[:, R_E2].astype(jnp.int32)
    r1, r2 = info[:, R_RANK1].astype(jnp.int32), info[:, R_RANK2].astype(jnp.int32)
    cnt = counts[0, :n_exp].astype(jnp.int32)
    padded = (cnt + tm - 1) // tm * tm
    ends = jnp.cumsum(padded)
    offs = ends - padded
    pos1, pos2 = offs[e1] + r1, offs[e2] + r2
    n_tiles = TOP_K * m // tm + n_exp
    n_rows = n_tiles * tm
    tok = jnp.arange(m, dtype=jnp.int32)
    src = jnp.zeros((n_rows,), jnp.int32).at[jnp.concatenate([pos1, pos2])].set(
        jnp.concatenate([tok, tok]), unique_indices=True)
    n_used = ends[-1] // tm
    tile_start = jnp.arange(n_tiles, dtype=jnp.int32) * tm
    tile_exp = jnp.searchsorted(ends, jnp.minimum(tile_start, ends[-1] - 1), side="right").astype(jnp.int32)
    tile_exp = jnp.minimum(tile_exp, n_exp - 1)
    y = _moe_ffn(x, shift, scale, tile_exp, n_used.reshape(1).astype(jnp.int32), src.reshape(n_tiles, 1, tm),
                 wg, wu, wd, tm)
    tc = min(COMBINE_ROWS, m)
    pos = jnp.concatenate([pos1.reshape(m // tc, 1, tc), pos2.reshape(m // tc, 1, tc)], axis=2)
    return _moe_combine(x, gate, info, y, pos, tc)


def _rope_tables(rows):
    n_freq = HEAD // 4
    inv = ROPE_THETA ** (-jnp.arange(n_freq, dtype=F32) / n_freq)
    ang_r = jnp.arange(rows, dtype=F32)[:, None] * inv
    ang_c = jnp.arange(GRID_W, dtype=F32)[:, None] * inv

    def full(fn):
        t = jnp.concatenate([jnp.broadcast_to(fn(ang_r)[:, None, :], (rows, GRID_W, n_freq)),
                             jnp.broadcast_to(fn(ang_c)[None, :, :], (rows, GRID_W, n_freq))], axis=-1)
        return t.reshape(rows * GRID_W, 2 * n_freq)

    cos, sin = full(jnp.cos), full(jnp.sin)
    return jnp.concatenate([cos, cos], axis=-1), jnp.concatenate([-sin, sin], axis=-1)


def _halves(a):
    shp = a.shape
    a = a.reshape(shp[:-1] + (shp[-1] // HEAD, HEAD // 2, 2))
    return jnp.swapaxes(a, -1, -2).reshape(shp)


def kernel(x, c, ctx, c_ctx, ada_w, ada_b, ev_w_in, ev_gate_b, ev_q_gain, ev_k_gain, ev_h_gain, ev_w_out,
           ev_ff_gate, ev_ff_up, ev_ff_down, od_w_in, od_q_gain, od_k_gain, od_lam_q1, od_lam_k1, od_lam_q2,
           od_lam_k2, od_sub_gain, od_w_out, od_router, od_ex_gate, od_ex_up, od_ex_down):
    bsz, s_len, d = x.shape
    n_ctx = ctx.shape[1]
    assert bsz == 1 and ada_w.shape[0] == 2
    a_heads = d // (2 * HEAD)
    a_kv = a_heads // GQA_GROUP
    b_heads = d // (2 * V_HEAD)
    c_heads = d // V_HEAD
    a_q, a_kvw, b_qk, b_v = a_heads * HEAD, a_kv * HEAD, b_heads * HEAD, b_heads * V_HEAD
    score_scale = HEAD ** -0.5 * math.log2(math.e)

    x_lat, x_ctx = x[0], ctx[0]
    cvecs = jnp.zeros((8, d), F32).at[0].set(c[0]).at[1].set(c_ctx)
    mods = _modulation(cvecs, ada_w, ada_b)

    def mod(layer, who):
        return [mods[layer, who, k * d:(k + 1) * d][None, :] for k in range(6)]

    cos_l, sin_l = _rope_tables(s_len // GRID_W)
    cos_c, sin_c = jnp.ones((n_ctx, HEAD), F32), jnp.zeros((n_ctx, HEAD), F32)

    w_in = ev_w_in[0]
    o0 = 0
    wqa = w_in[:, o0:o0 + a_q]; o0 += a_q
    wka = w_in[:, o0:o0 + a_kvw]; o0 += a_kvw
    wva = w_in[:, o0:o0 + a_kvw]; o0 += a_kvw
    wqb = w_in[:, o0:o0 + b_qk]; o0 += b_qk
    wkb = w_in[:, o0:o0 + b_qk]; o0 += b_qk
    wvb = w_in[:, o0:o0 + b_v]; o0 += b_v
    wob = w_in[:, o0:o0 + b_v]; o0 += b_v
    wgt = w_in[:, o0:]
    n_gates = wgt.shape[1]
    w0 = jnp.concatenate([_halves(wqa), wvb, wqb, wkb, _halves(wka), wva, wob], axis=1).astype(BF16)
    ones = lambda n: jnp.ones((n,), F32)
    cs0 = jnp.concatenate([
        jnp.tile(_halves(ev_q_gain[0]), a_heads) * score_scale, ones(b_v), ones(b_qk) * (HEAD ** -0.5), ones(b_qk),
        jnp.tile(_halves(ev_k_gain[0]), a_kv), ones(a_kvw), ones(b_v)])[None, :]
    modes0 = (["R"] * a_heads + ["P"] * (b_v // HEAD) + ["P"] * (2 * b_qk // HEAD) + ["R"] * a_kv
              + ["P"] * a_kv + ["P"] * (b_v // HEAD))
    nbf0 = a_q + b_v + 2 * b_qk + 2 * a_kvw
    wg0 = jnp.zeros((d, LANES), F32).at[:, :n_gates].set(wgt).astype(BF16)
    gb0 = jnp.zeros((1, LANES), F32).at[0, :n_gates].set(ev_gate_b[0])
    col_vb, col_qb, col_kb = a_q // b_v, (a_q + b_v) // b_qk, (a_q + b_v + b_qk) // b_qk
    col_ka = (a_q + b_v + 2 * b_qk) // HEAD
    col_va = col_ka + a_kv

    ml, mc = mod(0, 0), mod(0, 1)
    pl_b, pl_o, pl_g = _proj(x_lat, ml[0], ml[1], w0, cs0, cos_l, sin_l, modes0, nbf0, wg0, gb0)
    pc_b, pc_o, pc_g = _proj(x_ctx, mc[0], mc[1], w0, cs0, cos_c, sin_c, modes0, nbf0, wg0, gb0)
    p_all = jnp.concatenate([pl_b, pc_b], axis=0)
    g_all = jnp.concatenate([pl_g, pc_g], axis=0)
    g_row = g_all[:, :4 * b_heads].T

    ka_cols = slice(col_ka * HEAD, (col_ka + a_kv) * HEAD)
    att_l = _gqa_attention(pl_b, p_all[:, ka_cols].T, p_all, a_kv, col_va, tq=1024, tk=1280)
    att_c = _gqa_attention(pc_b, pc_b[:, ka_cols].T, pc_b, a_kv, col_va, tq=256, tk=256)
    hf, hb = _mlstm(p_all, g_all, g_row, b_heads, col_qb, col_kb, col_vb, s_len, n_ctx)

    hg = ev_h_gain[0].reshape(1, b_v)
    w_out0 = ev_w_out[0].astype(BF16)
    x_lat = _out_proj(x_lat, ml[2], w_out0, att_l, merge=(hf, hb, pl_o, hg, b_heads))
    x_ctx = _out_proj(x_ctx, mc[2], w_out0, att_c, merge=(hf, hb, pc_o, hg, b_heads), row_off=s_len)
    ffw = (ev_ff_gate[0].astype(BF16), ev_ff_up[0].astype(BF16), ev_ff_down[0].astype(BF16))
    x_lat = _ffn_dense(x_lat, ml[3], ml[4], ml[5], *ffw)
    x_ctx = _ffn_dense(x_ctx, mc[3], mc[4], mc[5], *ffw)

    c_qk = c_heads * 2 * HEAD
    w1 = od_w_in[0]
    w1 = jnp.concatenate([_halves(w1[:, :2 * c_qk]), w1[:, 2 * c_qk:]], axis=1).astype(BF16)
    cs1 = jnp.concatenate([jnp.tile(_halves(od_q_gain[0]), 2 * c_heads) * score_scale,
                           jnp.tile(_halves(od_k_gain[0]), 2 * c_heads), ones(c_heads * V_HEAD)])[None, :]
    modes1 = ["R"] * (4 * c_heads) + ["P"] * (c_heads * V_HEAD // HEAD)
    ml, mc = mod(1, 0), mod(1, 1)
    (ql,) = _proj(x_lat, ml[0], ml[1], w1, cs1, cos_l, sin_l, modes1, w1.shape[1])
    (qc,) = _proj(x_ctx, mc[0], mc[1], w1, cs1, cos_c, sin_c, modes1, w1.shape[1])
    kv_all = jnp.concatenate([ql, qc], axis=0)
    lam_init = 0.8 - 0.6 * math.exp(-0.3 * 1)
    vec = lambda a: a[0].reshape(1, -1)
    kt_all = kv_all[:, c_qk:2 * c_qk].T
    o_l = _diff_attention(ql, kt_all, kv_all, c_heads, lam_init, vec(od_lam_q1), vec(od_lam_k1), vec(od_lam_q2),
                          vec(od_lam_k2), vec(od_sub_gain), tq=2048, tk=1280)
    x_lat = _out_proj(x_lat, ml[2], od_w_out[0].astype(BF16), o_l)
    x_lat = _moe_top2(x_lat, ml[3], ml[4], ml[5], od_router[0], od_ex_gate[0].astype(BF16),
                      od_ex_up[0].astype(BF16), od_ex_down[0].astype(BF16))
    return x_lat[None]
```

```python
import functools
import math

import jax
import jax.numpy as jnp
from jax import lax
from jax.experimental import pallas as pl
from jax.experimental.pallas import tpu as pltpu

F32 = jnp.float32
BF16 = jnp.bfloat16

GRID_W = 64
ROPE_THETA = 10000.0
EPS = 1e-6
M_INIT = -1e30
HEAD = 128
V_HEAD = 256
GQA_GROUP = 4
TOP_K = 2
MLSTM_CHUNK = 256
GQA_SUB_ROWS = 256
DIFF_SUB_ROWS = 256
GATHER_UNROLL = 8
COMBINE_ROWS = 256
PIPE_BUFS = 4
PIPE_SKEW = 1
SOFTMAX_ROW_GROUP = 64
LANES = 128
VMEM_LIMIT = 56 * 1024 * 1024
NEG_INF = float("-inf")


def _cparams(sem, flags=None):
    return pltpu.CompilerParams(dimension_semantics=sem, vmem_limit_bytes=VMEM_LIMIT, flags=flags)


ATTN_FLAGS = None


def _dot(a, b):
    return jnp.dot(a, b, preferred_element_type=F32)


def _dot_nt(a, b):
    return lax.dot_general(a, b, (((1,), (1,)), ((), ())), preferred_element_type=F32)


def _dot_tn(a, b):
    return lax.dot_general(a, b, (((0,), (0,)), ((), ())), preferred_element_type=F32)


def _rms(x):
    return x * lax.rsqrt(jnp.mean(x * x, axis=-1, keepdims=True) + EPS)


def _mod_kernel(c_ref, w_ref, b_ref, o_ref):
    c = c_ref[...]
    s = c * jax.nn.sigmoid(c)
    o_ref[...] = _dot(s.astype(BF16), w_ref[...].astype(BF16)) + b_ref[...]


def _modulation(cvecs, ada_w, ada_b):
    depth, d, n = ada_w.shape
    tn = 1024
    return pl.pallas_call(
        _mod_kernel,
        grid=(depth, n // tn),
        in_specs=[
            pl.BlockSpec((8, d), lambda l, j: (0, 0)),
            pl.BlockSpec((None, d, tn), lambda l, j: (l, 0, j)),
            pl.BlockSpec((None, 1, tn), lambda l, j: (l, 0, j)),
        ],
        out_specs=pl.BlockSpec((None, 8, tn), lambda l, j: (l, 0, j)),
        out_shape=jax.ShapeDtypeStruct((depth, 8, n), F32),
        compiler_params=_cparams(("parallel", "parallel")),
        name="adaln_mod",
    )(cvecs, ada_w, ada_b.reshape(depth, 1, n))


def _proj_kernel(patterns, n_bf_tiles, has_f32, has_gates, tn, *refs):
    it = iter(refs)
    x_ref, sh_ref, sc_ref, w_ref, cs_ref, cos_ref, sin_ref = (next(it) for _ in range(7))
    if has_gates:
        wg_ref, gb_ref = next(it), next(it)
    ob_ref = next(it)
    of_ref = next(it) if has_f32 else None
    og_ref = next(it) if has_gates else None
    h_scr = next(it)

    j = pl.program_id(1)

    @pl.when(j == 0)
    def _():
        h = _rms(x_ref[...]) * (1.0 + sc_ref[...]) + sh_ref[...]
        h_scr[...] = h.astype(BF16)
        if has_gates:
            og_ref[...] = _dot(h_scr[...], wg_ref[...]) + gb_ref[...]

    acc = _dot(h_scr[...], w_ref[...])
    cs = cs_ref[...]

    for lo, hi, pat in patterns:
        @pl.when(jnp.logical_and(j >= lo, j < hi))
        def _(pat=pat, lo=lo):
            to_f32 = has_f32 and lo >= n_bf_tiles
            out_ref = of_ref if to_f32 else ob_ref
            for hidx, mode in enumerate(pat):
                sl = slice(hidx * HEAD, (hidx + 1) * HEAD)
                y = acc[:, sl]
                if mode == "R":
                    y = _rms(y) * cs[:, sl]
                    y = y * cos_ref[...] + pltpu.roll(y, HEAD // 2, 1) * sin_ref[...]
                else:
                    y = y * cs[:, sl]
                out_ref[:, sl] = y.astype(out_ref.dtype)


def _proj(x, shift, scale, w, colscale, cos2, sin2, head_modes, n_bf_cols, wg=None, gb=None, tm=1024, tn=512):
    m, d = x.shape
    n = w.shape[1]
    tm = min(tm, m)
    assert m % tm == 0 and n % tn == 0 and n_bf_cols % tn == 0
    nj = n // tn
    n_bf_tiles = n_bf_cols // tn
    has_f32 = n_bf_cols < n
    has_gates = wg is not None
    hpt = tn // HEAD
    tiles = [tuple(head_modes[t * hpt:(t + 1) * hpt]) for t in range(nj)]
    patterns = []
    for t, pat in enumerate(tiles):
        if patterns and patterns[-1][2] == pat and not (has_f32 and t == n_bf_tiles):
            patterns[-1] = (patterns[-1][0], t + 1, pat)
        else:
            patterns.append((t, t + 1, pat))

    in_specs = [
        pl.BlockSpec((tm, d), lambda i, j: (i, 0)),
        pl.BlockSpec((1, d), lambda i, j: (0, 0)),
        pl.BlockSpec((1, d), lambda i, j: (0, 0)),
        pl.BlockSpec((d, tn), lambda i, j: (0, j)),
        pl.BlockSpec((1, tn), lambda i, j: (0, j)),
        pl.BlockSpec((tm, HEAD), lambda i, j: (i, 0)),
        pl.BlockSpec((tm, HEAD), lambda i, j: (i, 0)),
    ]
    args = [x, shift, scale, w, colscale, cos2, sin2]
    if has_gates:
        in_specs += [pl.BlockSpec((d, LANES), lambda i, j: (0, 0)), pl.BlockSpec((1, LANES), lambda i, j: (0, 0))]
        args += [wg, gb]
    last_bf = n_bf_tiles - 1
    out_specs = [pl.BlockSpec((tm, tn), lambda i, j: (i, jnp.minimum(j, last_bf)))]
    out_shape = [jax.ShapeDtypeStruct((m, n_bf_cols), BF16)]
    if has_f32:
        out_specs.append(pl.BlockSpec((tm, tn), lambda i, j: (i, jnp.maximum(j - n_bf_tiles, 0))))
        out_shape.append(jax.ShapeDtypeStruct((m, n - n_bf_cols), F32))
    if has_gates:
        out_specs.append(pl.BlockSpec((tm, LANES), lambda i, j: (i, 0)))
        out_shape.append(jax.ShapeDtypeStruct((m, LANES), F32))
    return pl.pallas_call(
        functools.partial(_proj_kernel, tuple(patterns), n_bf_tiles, has_f32, has_gates, tn),
        grid=(m // tm, nj),
        in_specs=in_specs,
        out_specs=out_specs,
        out_shape=out_shape,
        scratch_shapes=[pltpu.VMEM((tm, d), BF16)],
        compiler_params=_cparams(("parallel", "arbitrary")),
        name="norm_mod_proj",
    )(*args)


def _softmax_pipeline(score_fns, v, row_slices, m_scr, l_scr, acc_scr, s_scr, p_scr):
    n = len(score_fns)
    nbuf = s_scr.shape[0]
    alphas = {}
    k = PIPE_SKEW
    assert nbuf > 2 * k or nbuf >= n
    for t in range(n + 2 * k):
        if t < n:
            s_scr[t % nbuf] = score_fns[t]()
        c = t - k
        if 0 <= c < n:
            alphas[c] = _softmax_probs(row_slices[c], m_scr, l_scr, s_scr.at[c % nbuf], p_scr.at[c % nbuf])
        c = t - 2 * k
        if 0 <= c < n:
            rows = row_slices[c]
            acc_scr[rows, :] = alphas.pop(c) * acc_scr[rows, :] + _dot(p_scr[c % nbuf], v)


def _softmax_probs(rows, m_scr, l_scr, s_buf, p_buf):
    sub, tk = s_buf.shape
    n_slab = tk // LANES
    mx = s_buf[:, 0:LANES]
    for c in range(1, n_slab):
        mx = jnp.maximum(mx, s_buf[:, c * LANES:(c + 1) * LANES])
    m_prev = m_scr[rows, :]
    m_new = jnp.maximum(m_prev, jnp.max(mx, axis=1, keepdims=True))
    alpha = jnp.exp2(m_prev - m_new)
    m_scr[rows, :] = m_new
    rg = min(sub, SOFTMAX_ROW_GROUP)
    for g in range(sub // rg):
        gr = slice(g * rg, (g + 1) * rg)
        mb = jnp.broadcast_to(m_new[gr, :], (rg, LANES))
        lsum = jnp.zeros((rg, LANES), F32)
        for c in range(n_slab):
            cs = slice(c * LANES, (c + 1) * LANES)
            p = jnp.exp2(s_buf[gr, cs] - mb)
            if l_scr is not None:
                lsum = lsum + p
            p_buf[gr, cs] = p.astype(BF16)
        if l_scr is not None:
            lr = slice(rows.start + g * rg, rows.start + (g + 1) * rg)
            l_scr[lr, :] = alpha[gr, :] * l_scr[lr, :] + lsum
    return alpha


def _gqa_kernel(tq, q_ref, k_ref, v_ref, o_ref, m_scr, acc_scr, s_scr, p_scr):
    kj = pl.program_id(2)

    @pl.when(kj == 0)
    def _():
        m_scr[...] = jnp.full(m_scr.shape, NEG_INF, F32)
        acc_scr[...] = jnp.zeros(acc_scr.shape, F32)

    k = k_ref[...]
    v = jnp.concatenate([v_ref[...], jnp.ones(v_ref.shape, BF16)], axis=1)
    sub = s_scr.shape[1]
    nb = tq // sub
    fns, rows = [], []
    for g in range(GQA_GROUP):
        for b in range(nb):
            fns.append(lambda g=g, b=b: _dot_nt(q_ref[b * sub:(b + 1) * sub, g * HEAD:(g + 1) * HEAD], k))
            rows.append(slice(g * tq + b * sub, g * tq + (b + 1) * sub))
    _softmax_pipeline(fns, v, rows, m_scr, None, acc_scr, s_scr, p_scr)

    @pl.when(kj == pl.num_programs(2) - 1)
    def _():
        out = acc_scr[:, :HEAD] / acc_scr[:, HEAD:HEAD + 1]
        for g in range(GQA_GROUP):
            o_ref[:, g * HEAD:(g + 1) * HEAD] = out[g * tq:(g + 1) * tq, :].astype(o_ref.dtype)


def _gqa_attention(q_arr, kv_arr, n_kv, k_col, v_col, tq, tk):
    s_len = q_arr.shape[0]
    lk = kv_arr.shape[0]
    tq = min(tq, s_len)
    tk = min(tk, lk)
    assert s_len % tq == 0 and lk % tk == 0 and tk % LANES == 0
    gw = GQA_GROUP * HEAD
    sub = min(tq, GQA_SUB_ROWS)
    n_chain = GQA_GROUP * (tq // sub)
    return pl.pallas_call(
        functools.partial(_gqa_kernel, tq),
        grid=(n_kv, s_len // tq, lk // tk),
        in_specs=[
            pl.BlockSpec((tq, gw), lambda n, i, j: (i, n)),
            pl.BlockSpec((tk, HEAD), lambda n, i, j: (j, k_col + n)),
            pl.BlockSpec((tk, HEAD), lambda n, i, j: (j, v_col + n)),
        ],
        out_specs=pl.BlockSpec((tq, gw), lambda n, i, j: (i, n)),
        out_shape=jax.ShapeDtypeStruct((s_len, n_kv * gw), BF16),
        scratch_shapes=[
            pltpu.VMEM((GQA_GROUP * tq, 1), F32),
            pltpu.VMEM((GQA_GROUP * tq, 2 * HEAD), F32),
            pltpu.VMEM((min(n_chain, PIPE_BUFS), sub, tk), F32),
            pltpu.VMEM((min(n_chain, PIPE_BUFS), sub, tk), BF16),
        ],
        compiler_params=_cparams(("parallel", "parallel", "arbitrary"), ATTN_FLAGS),
        name="gqa_flash",
    )(q_arr, kv_arr, kv_arr)


def _diff_kernel(tq, lam_init, q_ref, k_ref, v_ref, lq1_ref, lk1_ref, lq2_ref, lk2_ref, sg_ref, o_ref,
                 m_scr, l_scr, acc_scr, s_scr, p_scr):
    kj = pl.program_id(2)

    @pl.when(kj == 0)
    def _():
        m_scr[...] = jnp.full(m_scr.shape, NEG_INF, F32)
        l_scr[...] = jnp.zeros(l_scr.shape, F32)
        acc_scr[...] = jnp.zeros(acc_scr.shape, F32)

    v = v_ref[...]
    sub = s_scr.shape[1]
    nb = tq // sub
    fns, rows = [], []
    for b in range(nb):
        for r in range(2):
            fns.append(lambda r=r, b=b: _dot_nt(q_ref[b * sub:(b + 1) * sub, r * HEAD:(r + 1) * HEAD],
                                                k_ref[:, r * HEAD:(r + 1) * HEAD]))
            rows.append(slice(r * tq + b * sub, r * tq + (b + 1) * sub))
    _softmax_pipeline(fns, v, rows, m_scr, l_scr, acc_scr, s_scr, p_scr)

    @pl.when(kj == pl.num_programs(2) - 1)
    def _():
        lam = (jnp.exp(jnp.sum(lq1_ref[...] * lk1_ref[...], axis=1, keepdims=True))
               - jnp.exp(jnp.sum(lq2_ref[...] * lk2_ref[...], axis=1, keepdims=True)) + lam_init)
        o = acc_scr[...] / jnp.sum(l_scr[...], axis=1, keepdims=True)
        o = o[:tq, :] - lam * o[tq:, :]
        o = _rms(o) * sg_ref[...] * (1.0 - lam_init)
        o_ref[...] = o.astype(o_ref.dtype)


def _diff_attention(q_arr, kv_arr, n_heads, lam_init, lq1, lk1, lq2, lk2, sub_gain, tq, tk):
    s_len = q_arr.shape[0]
    lk = kv_arr.shape[0]
    tq = min(tq, s_len)
    tk = min(tk, lk)
    assert s_len % tq == 0 and lk % tk == 0 and tk % LANES == 0
    sub = min(tq, DIFF_SUB_ROWS)
    n_chain = 2 * (tq // sub)
    vec = pl.BlockSpec((1, HEAD), lambda h, i, j: (0, 0))
    return pl.pallas_call(
        functools.partial(_diff_kernel, tq, lam_init),
        grid=(n_heads, s_len // tq, lk // tk),
        in_specs=[
            pl.BlockSpec((tq, V_HEAD), lambda h, i, j: (i, h)),
            pl.BlockSpec((tk, V_HEAD), lambda h, i, j: (j, n_heads + h)),
            pl.BlockSpec((tk, V_HEAD), lambda h, i, j: (j, 2 * n_heads + h)),
            vec, vec, vec, vec,
            pl.BlockSpec((1, V_HEAD), lambda h, i, j: (0, 0)),
        ],
        out_specs=pl.BlockSpec((tq, V_HEAD), lambda h, i, j: (i, h)),
        out_shape=jax.ShapeDtypeStruct((s_len, n_heads * V_HEAD), BF16),
        scratch_shapes=[
            pltpu.VMEM((2 * tq, 1), F32),
            pltpu.VMEM((2 * tq, LANES), F32),
            pltpu.VMEM((2 * tq, V_HEAD), F32),
            pltpu.VMEM((min(n_chain, PIPE_BUFS), sub, tk), F32),
            pltpu.VMEM((min(n_chain, PIPE_BUFS), sub, tk), BF16),
        ],
        compiler_params=_cparams(("parallel", "parallel", "arbitrary"), ATTN_FLAGS),
        name="diff_flash",
    )(q_arr, kv_arr, kv_arr, lq1, lk1, lq2, lk2, sub_gain)


def _split3(x):
    hi = x.astype(BF16)
    r1 = x - hi.astype(F32)
    mid = r1.astype(BF16)
    lo = (r1 - mid.astype(F32)).astype(BF16)
    return hi, mid, lo


def _log_sigmoid(x):
    return jnp.minimum(x, 0.0) - jnp.log(1.0 + jnp.exp(-jnp.abs(x)))


def _mlstm_kernel(nh, chunk, qf_ref, kf_ref, vf_ref, gcf_ref, grf_ref, qb_ref, kb_ref, vb_ref, gcb_ref, grb_ref,
                  hf_ref, hb_ref, c_scr, n_scr, m_scr):
    t = pl.program_id(0)

    @pl.when(t == 0)
    def _():
        c_scr[...] = jnp.zeros(c_scr.shape, F32)
        n_scr[...] = jnp.zeros(n_scr.shape, F32)
        m_scr[...] = jnp.full(m_scr.shape, M_INIT, F32)

    row = lax.broadcasted_iota(jnp.int32, (chunk, chunk), 0)
    col = lax.broadcasted_iota(jnp.int32, (chunk, chunk), 1)
    lower = col <= row
    upper = col >= row
    lower_b = lower.astype(BF16)
    upper_b = upper.astype(BF16)

    for rev in (False, True):
        q_ref, k_ref, v_ref, gc_ref, gr_ref, h_ref = (
            (qb_ref, kb_ref, vb_ref, gcb_ref, grb_ref, hb_ref) if rev else
            (qf_ref, kf_ref, vf_ref, gcf_ref, grf_ref, hf_ref))
        mask = upper if rev else lower
        gcol = gc_ref[...]
        grow = gr_ref[...]
        cum_col = sum(_dot(upper_b if rev else lower_b, piece) for piece in _split3(_log_sigmoid(gcol)))
        cum_row = sum(_dot(piece, lower_b if rev else upper_b) for piece in _split3(_log_sigmoid(grow)))
        for hd in range(nh):
            idx = (1 if rev else 0) * nh + hd
            ki = (2 if rev else 0) * nh + hd
            kf = ki + nh
            qc = q_ref[:, hd * HEAD:(hd + 1) * HEAD]
            kc = k_ref[:, hd * HEAD:(hd + 1) * HEAD]
            vc = v_ref[:, hd * V_HEAD:(hd + 1) * V_HEAD]
            i_col = gcol[:, ki:ki + 1]
            i_row = grow[ki:ki + 1, :]
            b_col = cum_col[:, kf:kf + 1]
            b_row = cum_row[kf:kf + 1, :]
            c_mem = c_scr[idx]
            n_mem = n_scr[idx]
            m = m_scr[idx][:, :1]

            dlog = jnp.where(mask, b_col + (i_row - b_row), NEG_INF)
            inter = b_col + m
            m_out = jnp.maximum(inter, jnp.max(dlog, axis=1, keepdims=True))
            w_intra = jnp.exp(dlog - m_out)
            w_inter = jnp.exp(inter - m_out)
            qk = _dot_nt(qc, kc) * w_intra
            num = w_inter * _dot(qc, c_mem.astype(BF16)) + _dot(qk.astype(BF16), vc)
            den = (w_inter * jnp.sum(qc.astype(F32) * n_mem, axis=1, keepdims=True)
                   + jnp.sum(qk, axis=1, keepdims=True))
            h = num / jnp.maximum(jnp.abs(den), jnp.exp(-m_out))
            h_ref[:, hd * V_HEAD:(hd + 1) * V_HEAD] = h

            b_last = b_col[0:1, :] if rev else b_col[chunk - 1:chunk, :]
            g_col = b_last - b_col + i_col
            m_new = jnp.maximum(b_last + m, jnp.max(g_col, axis=0, keepdims=True))
            a_dec = jnp.exp(b_last + m - m_new)
            kw = kc.astype(F32) * jnp.exp(g_col - m_new)
            c_scr[idx] = a_dec * c_mem + _dot_tn(kw.astype(BF16), vc)
            n_scr[idx] = a_dec * n_mem + jnp.sum(kw, axis=0, keepdims=True)
            m_scr[idx] = jnp.broadcast_to(m_new, (1, LANES))


def _mlstm(qkv, gates_col, gates_row, nh, q_col, k_col, v_col, n_lat, n_ctx):
    t_len = qkv.shape[0]
    chunk = MLSTM_CHUNK
    assert n_lat % chunk == 0 and n_ctx % chunk == 0 and t_len == n_lat + n_ctx
    nlat, nctx = n_lat // chunk, n_ctx // chunk
    nc = nlat + nctx

    def fwd(t):
        return jnp.where(t < nctx, nlat + t, t - nctx)

    def bwd(t):
        return jnp.where(t < nctx, nc - 1 - t, nlat - 1 - (t - nctx))

    qw, vw = nh * HEAD, nh * V_HEAD

    def specs(order):
        return [
            pl.BlockSpec((chunk, qw), lambda t: (order(t), q_col)),
            pl.BlockSpec((chunk, qw), lambda t: (order(t), k_col)),
            pl.BlockSpec((chunk, vw), lambda t: (order(t), v_col)),
            pl.BlockSpec((chunk, LANES), lambda t: (order(t), 0)),
            pl.BlockSpec((4 * nh, chunk), lambda t: (0, order(t))),
        ]

    return pl.pallas_call(
        functools.partial(_mlstm_kernel, nh, chunk),
        grid=(nc,),
        in_specs=specs(fwd) + specs(bwd),
        out_specs=[pl.BlockSpec((chunk, vw), lambda t: (fwd(t), 0)),
                   pl.BlockSpec((chunk, vw), lambda t: (bwd(t), 0))],
        out_shape=[jax.ShapeDtypeStruct((t_len, vw), F32)] * 2,
        scratch_shapes=[
            pltpu.VMEM((2 * nh, HEAD, V_HEAD), F32),
            pltpu.VMEM((2 * nh, 1, HEAD), F32),
            pltpu.VMEM((2 * nh, 1, LANES), F32),
        ],
        compiler_params=_cparams(("arbitrary",)),
        name="mlstm_scan",
    )(qkv, qkv, qkv, gates_col, gates_row, qkv, qkv, qkv, gates_col, gates_row)


def _out_kernel(merge, nh, *refs):
    if merge:
        a_ref, hf_ref, hb_ref, o_ref, hg_ref, w_ref, x_ref, g_ref, y_ref, mix_scr = refs
        aw = a_ref.shape[1]

        @pl.when(pl.program_id(1) == 0)
        def _():
            mix_scr[:, :aw] = a_ref[...]
            for hd in range(nh):
                sl = slice(hd * V_HEAD, (hd + 1) * V_HEAD)
                hm = _rms(hf_ref[:, sl] + hb_ref[:, sl]) * hg_ref[:, sl] * jax.nn.sigmoid(o_ref[:, sl])
                mix_scr[:, aw + hd * V_HEAD:aw + (hd + 1) * V_HEAD] = hm.astype(BF16)

        lhs = mix_scr[...]
    else:
        a_ref, w_ref, x_ref, g_ref, y_ref = refs
        lhs = a_ref[...]
    y_ref[...] = x_ref[...] + g_ref[...] * _dot(lhs, w_ref[...])


def _out_proj(x, gate, w, a, merge=None, row_off=0, tm=512, tn=1024):
    m, d = x.shape
    kdim, n = w.shape
    tm = min(tm, m)
    assert m % tm == 0 and n % tn == 0 and row_off % tm == 0
    off = row_off // tm
    in_specs = [pl.BlockSpec((tm, a.shape[1]), lambda i, j: (i, 0))]
    args = [a]
    scratch = []
    nh = 0
    if merge is not None:
        hf, hb, o, hg, nh = merge
        hw = hf.shape[1]
        in_specs += [pl.BlockSpec((tm, hw), lambda i, j: (i + off, 0)),
                     pl.BlockSpec((tm, hw), lambda i, j: (i + off, 0)),
                     pl.BlockSpec((tm, hw), lambda i, j: (i, 0)),
                     pl.BlockSpec((1, hw), lambda i, j: (0, 0))]
        args += [hf, hb, o, hg]
        scratch = [pltpu.VMEM((tm, kdim), BF16)]
    in_specs += [pl.BlockSpec((kdim, tn), lambda i, j: (0, j)),
                 pl.BlockSpec((tm, tn), lambda i, j: (i, j)),
                 pl.BlockSpec((1, tn), lambda i, j: (0, j))]
    args += [w, x, gate]
    return pl.pallas_call(
        functools.partial(_out_kernel, merge is not None, nh),
        grid=(m // tm, n // tn),
        in_specs=in_specs,
        out_specs=pl.BlockSpec((tm, tn), lambda i, j: (i, j)),
        out_shape=jax.ShapeDtypeStruct((m, n), F32),
        scratch_shapes=scratch,
        compiler_params=_cparams(("parallel", "arbitrary")),
        name="out_proj_residual",
    )(*args)


def _ffn_kernel(x_ref, sh_ref, sc_ref, g_ref, wg_ref, wu_ref, wd_ref, y_ref, h_scr, acc_scr):
    f = pl.program_id(1)

    @pl.when(f == 0)
    def _():
        h = _rms(x_ref[...]) * (1.0 + sc_ref[...]) + sh_ref[...]
        h_scr[...] = h.astype(BF16)
        acc_scr[...] = jnp.zeros(acc_scr.shape, F32)

    h = h_scr[...]
    gt = _dot(h, wg_ref[...])
    up = _dot(h, wu_ref[...])
    act = (gt * jax.nn.sigmoid(gt)) * up
    acc_scr[...] += _dot(act.astype(BF16), wd_ref[...])

    @pl.when(f == pl.num_programs(1) - 1)
    def _():
        y_ref[...] = x_ref[...] + g_ref[...] * acc_scr[...]


def _ffn_dense(x, shift, scale, gate, wg, wu, wd, tm=512, tf=512):
    m, d = x.shape
    ff = wg.shape[1]
    tm = min(tm, m)
    assert m % tm == 0 and ff % tf == 0
    vec = pl.BlockSpec((1, d), lambda i, f: (0, 0))
    return pl.pallas_call(
        _ffn_kernel,
        grid=(m // tm, ff // tf),
        in_specs=[
            pl.BlockSpec((tm, d), lambda i, f: (i, 0)), vec, vec, vec,
            pl.BlockSpec((d, tf), lambda i, f: (0, f)),
            pl.BlockSpec((d, tf), lambda i, f: (0, f)),
            pl.BlockSpec((tf, d), lambda i, f: (f, 0)),
        ],
        out_specs=pl.BlockSpec((tm, d), lambda i, f: (i, 0)),
        out_shape=jax.ShapeDtypeStruct((m, d), F32),
        scratch_shapes=[pltpu.VMEM((tm, d), BF16), pltpu.VMEM((tm, d), F32)],
        compiler_params=_cparams(("parallel", "arbitrary")),
        name="ffn_dense",
    )(x, shift, scale, gate, wg, wu, wd)


R_E1, R_E2, R_W1, R_W2, R_RANK1, R_RANK2 = range(6)


def _lane_pick(a, lane, k):
    return jnp.sum(jnp.where(lane == k, a, 0.0), axis=1, keepdims=True)


def _router_kernel(n_exp, x_ref, sh_ref, sc_ref, r_ref, info_ref, cnt_ref, carry_scr):
    @pl.when(pl.program_id(0) == 0)
    def _():
        carry_scr[...] = jnp.zeros(carry_scr.shape, F32)

    tm = x_ref.shape[0]
    h = _rms(x_ref[...]) * (1.0 + sc_ref[...]) + sh_ref[...]
    logits = jnp.dot(h, r_ref[...], preferred_element_type=F32, precision=lax.Precision.HIGHEST)
    lane = lax.broadcasted_iota(jnp.int32, logits.shape, 1)
    logits = jnp.where(lane < n_exp, logits, NEG_INF)
    v1 = jnp.max(logits, axis=1, keepdims=True)
    i1 = jnp.min(jnp.where(logits == v1, lane, LANES), axis=1, keepdims=True)
    rest = jnp.where(lane == i1, NEG_INF, logits)
    v2 = jnp.max(rest, axis=1, keepdims=True)
    i2 = jnp.min(jnp.where(rest == v2, lane, LANES), axis=1, keepdims=True)
    e2 = jnp.exp(v2 - v1)
    w1 = 1.0 / (1.0 + e2)
    w2 = e2 / (1.0 + e2)

    chosen = jnp.where(jnp.logical_or(lane == i1, lane == i2), 1.0, 0.0)
    r_i = lax.broadcasted_iota(jnp.int32, (tm, tm), 0)
    c_i = lax.broadcasted_iota(jnp.int32, (tm, tm), 1)
    before = (c_i < r_i).astype(BF16)
    rank = _dot(before, chosen.astype(BF16)) + carry_scr[...]
    carry_scr[...] += jnp.sum(chosen, axis=0, keepdims=True)
    cnt_ref[...] = carry_scr[...]
    rank1 = jnp.sum(jnp.where(lane == i1, rank, 0.0), axis=1, keepdims=True)
    rank2 = jnp.sum(jnp.where(lane == i2, rank, 0.0), axis=1, keepdims=True)
    info = jnp.zeros(logits.shape, F32)
    for k, val in ((R_E1, i1.astype(F32)), (R_E2, i2.astype(F32)), (R_W1, w1), (R_W2, w2),
                   (R_RANK1, rank1), (R_RANK2, rank2)):
        info = jnp.where(lane == k, val, info)
    info_ref[...] = info


def _router(x, shift, scale, router, tm=512):
    m, d = x.shape
    n_exp = router.shape[1]
    vec = pl.BlockSpec((1, d), lambda i: (0, 0))
    router_pad = jnp.zeros((d, LANES), F32).at[:, :n_exp].set(router)
    return pl.pallas_call(
        functools.partial(_router_kernel, n_exp),
        grid=(m // tm,),
        in_specs=[pl.BlockSpec((tm, d), lambda i: (i, 0)), vec, vec, pl.BlockSpec((d, LANES), lambda i: (0, 0))],
        out_specs=[pl.BlockSpec((tm, LANES), lambda i: (i, 0)), pl.BlockSpec((1, LANES), lambda i: (0, 0))],
        out_shape=[jax.ShapeDtypeStruct((m, LANES), F32), jax.ShapeDtypeStruct((1, LANES), F32)],
        scratch_shapes=[pltpu.VMEM((1, LANES), F32)],
        compiler_params=_cparams(("arbitrary",)),
        name="moe_router",
    )(x, shift, scale, router_pad)


def _gather_rows(src_hbm, idx_ref, buf, sem, start):
    n_rows = buf.shape[0]
    if not start:
        pltpu.make_async_copy(src_hbm.at[pl.ds(0, n_rows), :], buf, sem).wait()
        return

    def body(r, carry):
        pltpu.make_async_copy(src_hbm.at[pl.ds(idx_ref[0, r], 1), :], buf.at[pl.ds(r, 1), :], sem).start()
        return carry
    lax.fori_loop(0, n_rows, body, 0, unroll=GATHER_UNROLL)


def _moe_ffn_kernel(te_ref, nu_ref, src_ref, nsrc_ref, x_hbm, sh_ref, sc_ref, wg_ref, wu_ref, wd_ref, y_ref,
                    xbuf, hbuf, sem):
    i = pl.program_id(0)
    f = pl.program_id(1)
    n_used = nu_ref[0]
    active = i < n_used

    @pl.when(f == 0)
    def _():
        y_ref[...] = jnp.zeros(y_ref.shape, F32)

    @pl.when(jnp.logical_and(active, f == 0))
    def _():
        pl.when(i == 0)(lambda: _gather_rows(x_hbm, src_ref, xbuf, sem, True))
        _gather_rows(x_hbm, src_ref, xbuf, sem, False)
        h = _rms(xbuf[...]) * (1.0 + sc_ref[...]) + sh_ref[...]
        hbuf[...] = h.astype(BF16)
        pl.when(i + 1 < n_used)(lambda: _gather_rows(x_hbm, nsrc_ref, xbuf, sem, True))

    @pl.when(active)
    def _():
        h = hbuf[...]
        gt = _dot(h, wg_ref[...])
        up = _dot(h, wu_ref[...])
        act = (gt * jax.nn.sigmoid(gt)) * up
        y_ref[...] += _dot(act.astype(BF16), wd_ref[...])


def _moe_ffn(x, shift, scale, tile_exp, n_used, src, wg, wu, wd, tm, tf=1024):
    m, d = x.shape
    n_exp, _, ff = wg.shape
    n_tiles = src.shape[0]
    tf = min(tf, ff)
    assert ff % tf == 0
    nf = ff // tf
    vec = pl.BlockSpec((1, d), lambda i, f, te, nu: (0, 0))
    idx = pl.BlockSpec((None, 1, tm), lambda i, f, te, nu: (i, 0, 0), memory_space=pltpu.SMEM)
    idx_next = pl.BlockSpec((None, 1, tm), lambda i, f, te, nu: (jnp.minimum(i + 1, n_tiles - 1), 0, 0),
                            memory_space=pltpu.SMEM)

    def f_eff(i, f, nu):
        return jnp.where(i < nu[0], f, nf - 1)

    grid_spec = pltpu.PrefetchScalarGridSpec(
        num_scalar_prefetch=2,
        grid=(n_tiles, nf),
        in_specs=[
            idx, idx_next,
            pl.BlockSpec(memory_space=pl.ANY),
            vec, vec,
            pl.BlockSpec((None, d, tf), lambda i, f, te, nu: (te[i], 0, f_eff(i, f, nu))),
            pl.BlockSpec((None, d, tf), lambda i, f, te, nu: (te[i], 0, f_eff(i, f, nu))),
            pl.BlockSpec((None, tf, d), lambda i, f, te, nu: (te[i], f_eff(i, f, nu), 0)),
        ],
        out_specs=pl.BlockSpec((tm, d), lambda i, f, te, nu: (i, 0)),
        scratch_shapes=[pltpu.VMEM((tm, d), F32), pltpu.VMEM((tm, d), BF16), pltpu.SemaphoreType.DMA(())],
    )
    return pl.pallas_call(
        _moe_ffn_kernel,
        grid_spec=grid_spec,
        out_shape=jax.ShapeDtypeStruct((n_tiles * tm, d), F32),
        compiler_params=_cparams(("arbitrary", "arbitrary")),
        name="moe_grouped_ffn",
    )(tile_exp, n_used, src, src, x, shift, scale, wg, wu, wd)


def _moe_combine_kernel(pos_ref, npos_ref, x_ref, g_ref, info_ref, y_hbm, o_ref, ybuf, sem):
    i = pl.program_id(0)
    tm = x_ref.shape[0]
    slot = i % 2
    pl.when(i == 0)(lambda: _gather_rows(y_hbm, pos_ref, ybuf.at[0], sem.at[0], True))
    for s in range(2):
        @pl.when(slot == s)
        def _(s=s):
            _gather_rows(y_hbm, pos_ref, ybuf.at[s], sem.at[s], False)
            pl.when(i + 1 < pl.num_programs(0))(
                lambda: _gather_rows(y_hbm, npos_ref, ybuf.at[1 - s], sem.at[1 - s], True))
            info = info_ref[...]
            lane = lax.broadcasted_iota(jnp.int32, info.shape, 1)
            w1 = _lane_pick(info, lane, R_W1)
            w2 = _lane_pick(info, lane, R_W2)
            o_ref[...] = x_ref[...] + g_ref[...] * (w1 * ybuf[s, :tm, :] + w2 * ybuf[s, tm:, :])


def _moe_combine(x, gate, info, y, pos, tm=256):
    m, d = x.shape
    nb = m // tm
    assert m % tm == 0 and pos.shape == (nb, 1, 2 * tm)
    idx = pl.BlockSpec((None, 1, 2 * tm), lambda i: (i, 0, 0), memory_space=pltpu.SMEM)
    idx_next = pl.BlockSpec((None, 1, 2 * tm), lambda i: (jnp.minimum(i + 1, nb - 1), 0, 0), memory_space=pltpu.SMEM)
    return pl.pallas_call(
        _moe_combine_kernel,
        grid=(nb,),
        in_specs=[
            idx, idx_next,
            pl.BlockSpec((tm, d), lambda i: (i, 0)),
            pl.BlockSpec((1, d), lambda i: (0, 0)),
            pl.BlockSpec((tm, LANES), lambda i: (i, 0)),
            pl.BlockSpec(memory_space=pl.ANY),
        ],
        out_specs=pl.BlockSpec((tm, d), lambda i: (i, 0)),
        out_shape=jax.ShapeDtypeStruct((m, d), F32),
        scratch_shapes=[pltpu.VMEM((2, 2 * tm, d), F32), pltpu.SemaphoreType.DMA((2,))],
        compiler_params=_cparams(("arbitrary",)),
        name="moe_combine",
    )(pos, pos, x, gate, info, y)


def _moe_top2(x, shift, scale, gate, router, wg, wu, wd, tm=512):
    m, d = x.shape
    n_exp = router.shape[1]
    info, counts = _router(x, shift, scale, router)
    e1, e2 = info[:, R_E1].astype(jnp.int32), info[:, R_E2].astype(jnp.int32)
    r1, r2 = info[:, R_RANK1].astype(jnp.int32), info[:, R_RANK2].astype(jnp.int32)
    cnt = counts[0, :n_exp].astype(jnp.int32)
    padded = (cnt + tm - 1) // tm * tm
    ends = jnp.cumsum(padded)
    offs = ends - padded
    pos1, pos2 = offs[e1] + r1, offs[e2] + r2
    n_tiles = TOP_K * m // tm + n_exp
    n_rows = n_tiles * tm
    tok = jnp.arange(m, dtype=jnp.int32)
    src = jnp.zeros((n_rows,), jnp.int32).at[jnp.concatenate([pos1, pos2])].set(
        jnp.concatenate([tok, tok]), unique_indices=True)
    n_used = ends[-1] // tm
    tile_start = jnp.arange(n_tiles, dtype=jnp.int32) * tm
    tile_exp = jnp.searchsorted(ends, jnp.minimum(tile_start, ends[-1] - 1), side="right").astype(jnp.int32)
    tile_exp = jnp.minimum(tile_exp, n_exp - 1)
    y = _moe_ffn(x, shift, scale, tile_exp, n_used.reshape(1).astype(jnp.int32), src.reshape(n_tiles, 1, tm),
                 wg, wu, wd, tm)
    tc = min(COMBINE_ROWS, m)
    pos = jnp.concatenate([pos1.reshape(m // tc, 1, tc), pos2.reshape(m // tc, 1, tc)], axis=2)
    return _moe_combine(x, gate, info, y, pos, tc)


def _rope_tables(rows):
    n_freq = HEAD // 4
    inv = ROPE_THETA ** (-jnp.arange(n_freq, dtype=F32) / n_freq)
    ang_r = jnp.arange(rows, dtype=F32)[:, None] * inv
    ang_c = jnp.arange(GRID_W, dtype=F32)[:, None] * inv

    def full(fn):
        t = jnp.concatenate([jnp.broadcast_to(fn(ang_r)[:, None, :], (rows, GRID_W, n_freq)),
                             jnp.broadcast_to(fn(ang_c)[None, :, :], (rows, GRID_W, n_freq))], axis=-1)
        return t.reshape(rows * GRID_W, 2 * n_freq)

    cos, sin = full(jnp.cos), full(jnp.sin)
    return jnp.concatenate([cos, cos], axis=-1), jnp.concatenate([-sin, sin], axis=-1)


def _halves(a):
    shp = a.shape
    a = a.reshape(shp[:-1] + (shp[-1] // HEAD, HEAD // 2, 2))
    return jnp.swapaxes(a, -1, -2).reshape(shp)


def kernel(x, c, ctx, c_ctx, ada_w, ada_b, ev_w_in, ev_gate_b, ev_q_gain, ev_k_gain, ev_h_gain, ev_w_out,
           ev_ff_gate, ev_ff_up, ev_ff_down, od_w_in, od_q_gain, od_k_gain, od_lam_q1, od_lam_k1, od_lam_q2,
           od_lam_k2, od_sub_gain, od_w_out, od_router, od_ex_gate, od_ex_up, od_ex_down):
    bsz, s_len, d = x.shape
    n_ctx = ctx.shape[1]
    assert bsz == 1 and ada_w.shape[0] == 2
    a_heads = d // (2 * HEAD)
    a_kv = a_heads // GQA_GROUP
    b_heads = d // (2 * V_HEAD)
    c_heads = d // V_HEAD
    a_q, a_kvw, b_qk, b_v = a_heads * HEAD, a_kv * HEAD, b_heads * HEAD, b_heads * V_HEAD
    score_scale = HEAD ** -0.5 * math.log2(math.e)

    x_lat, x_ctx = x[0], ctx[0]
    cvecs = jnp.zeros((8, d), F32).at[0].set(c[0]).at[1].set(c_ctx)
    mods = _modulation(cvecs, ada_w, ada_b)

    def mod(layer, who):
        return [mods[layer, who, k * d:(k + 1) * d][None, :] for k in range(6)]

    cos_l, sin_l = _rope_tables(s_len // GRID_W)
    cos_c, sin_c = jnp.ones((n_ctx, HEAD), F32), jnp.zeros((n_ctx, HEAD), F32)

    w_in = ev_w_in[0]
    o0 = 0
    wqa = w_in[:, o0:o0 + a_q]; o0 += a_q
    wka = w_in[:, o0:o0 + a_kvw]; o0 += a_kvw
    wva = w_in[:, o0:o0 + a_kvw]; o0 += a_kvw
    wqb = w_in[:, o0:o0 + b_qk]; o0 += b_qk
    wkb = w_in[:, o0:o0 + b_qk]; o0 += b_qk
    wvb = w_in[:, o0:o0 + b_v]; o0 += b_v
    wob = w_in[:, o0:o0 + b_v]; o0 += b_v
    wgt = w_in[:, o0:]
    n_gates = wgt.shape[1]
    w0 = jnp.concatenate([_halves(wqa), wvb, wqb, wkb, _halves(wka), wva, wob], axis=1).astype(BF16)
    ones = lambda n: jnp.ones((n,), F32)
    cs0 = jnp.concatenate([
        jnp.tile(_halves(ev_q_gain[0]), a_heads) * score_scale, ones(b_v), ones(b_qk) * (HEAD ** -0.5), ones(b_qk),
        jnp.tile(_halves(ev_k_gain[0]), a_kv), ones(a_kvw), ones(b_v)])[None, :]
    modes0 = (["R"] * a_heads + ["P"] * (b_v // HEAD) + ["P"] * (2 * b_qk // HEAD) + ["R"] * a_kv
              + ["P"] * a_kv + ["P"] * (b_v // HEAD))
    nbf0 = a_q + b_v + 2 * b_qk + 2 * a_kvw
    wg0 = jnp.zeros((d, LANES), F32).at[:, :n_gates].set(wgt).astype(BF16)
    gb0 = jnp.zeros((1, LANES), F32).at[0, :n_gates].set(ev_gate_b[0])
    col_vb, col_qb, col_kb = a_q // b_v, (a_q + b_v) // b_qk, (a_q + b_v + b_qk) // b_qk
    col_ka = (a_q + b_v + 2 * b_qk) // HEAD
    col_va = col_ka + a_kv

    ml, mc = mod(0, 0), mod(0, 1)
    pl_b, pl_o, pl_g = _proj(x_lat, ml[0], ml[1], w0, cs0, cos_l, sin_l, modes0, nbf0, wg0, gb0)
    pc_b, pc_o, pc_g = _proj(x_ctx, mc[0], mc[1], w0, cs0, cos_c, sin_c, modes0, nbf0, wg0, gb0)
    p_all = jnp.concatenate([pl_b, pc_b], axis=0)
    g_all = jnp.concatenate([pl_g, pc_g], axis=0)
    g_row = g_all[:, :4 * b_heads].T

    att_l = _gqa_attention(pl_b, p_all, a_kv, col_ka, col_va, tq=1024, tk=1280)
    att_c = _gqa_attention(pc_b, pc_b, a_kv, col_ka, col_va, tq=256, tk=256)
    hf, hb = _mlstm(p_all, g_all, g_row, b_heads, col_qb, col_kb, col_vb, s_len, n_ctx)

    hg = ev_h_gain[0].reshape(1, b_v)
    w_out0 = ev_w_out[0].astype(BF16)
    x_lat = _out_proj(x_lat, ml[2], w_out0, att_l, merge=(hf, hb, pl_o, hg, b_heads))
    x_ctx = _out_proj(x_ctx, mc[2], w_out0, att_c, merge=(hf, hb, pc_o, hg, b_heads), row_off=s_len)
    ffw = (ev_ff_gate[0].astype(BF16), ev_ff_up[0].astype(BF16), ev_ff_down[0].astype(BF16))
    x_lat = _ffn_dense(x_lat, ml[3], ml[4], ml[5], *ffw)
    x_ctx = _ffn_dense(x_ctx, mc[3], mc[4], mc[5], *ffw)

    c_qk = c_heads * 2 * HEAD
    w1 = od_w_in[0]
    w1 = jnp.concatenate([_halves(w1[:, :2 * c_qk]), w1[:, 2 * c_qk:]], axis=1).astype(BF16)
    cs1 = jnp.concatenate([jnp.tile(_halves(od_q_gain[0]), 2 * c_heads) * score_scale,
                           jnp.tile(_halves(od_k_gain[0]), 2 * c_heads), ones(c_heads * V_HEAD)])[None, :]
    modes1 = ["R"] * (4 * c_heads) + ["P"] * (c_heads * V_HEAD // HEAD)
    ml, mc = mod(1, 0), mod(1, 1)
    (ql,) = _proj(x_lat, ml[0], ml[1], w1, cs1, cos_l, sin_l, modes1, w1.shape[1])
    (qc,) = _proj(x_ctx, mc[0], mc[1], w1, cs1, cos_c, sin_c, modes1, w1.shape[1])
    kv_all = jnp.concatenate([ql, qc], axis=0)
    lam_init = 0.8 - 0.6 * math.exp(-0.3 * 1)
    vec = lambda a: a[0].reshape(1, -1)
    o_l = _diff_attention(ql, kv_all, c_heads, lam_init, vec(od_lam_q1), vec(od_lam_k1), vec(od_lam_q2),
                          vec(od_lam_k2), vec(od_sub_gain), tq=2048, tk=1280)
    x_lat = _out_proj(x_lat, ml[2], od_w_out[0].astype(BF16), o_l)
    x_lat = _moe_top2(x_lat, ml[3], ml[4], ml[5], od_router[0], od_ex_gate[0].astype(BF16),
                      od_ex_up[0].astype(BF16), od_ex_down[0].astype(BF16))
    return x_lat[None]
```

```python
import functools
import math

import jax
import jax.numpy as jnp
from jax import lax
from jax.experimental import pallas as pl
from jax.experimental.pallas import tpu as pltpu

F32 = jnp.float32
BF16 = jnp.bfloat16

GRID_W = 64
ROPE_THETA = 10000.0
EPS = 1e-6
M_INIT = -1e30
HEAD = 128
V_HEAD = 256
GQA_GROUP = 4
TOP_K = 2
MLSTM_CHUNK = 256
GQA_SUB_ROWS = 256
DIFF_SUB_ROWS = 256
GATHER_UNROLL = 8
COMBINE_ROWS = 256
PIPE_SKEW = 1
LANES = 128
VMEM_LIMIT = 56 * 1024 * 1024
NEG_INF = float("-inf")


def _cparams(sem, flags=None):
    return pltpu.CompilerParams(dimension_semantics=sem, vmem_limit_bytes=VMEM_LIMIT, flags=flags)


ATTN_FLAGS = None


def _dot(a, b):
    return jnp.dot(a, b, preferred_element_type=F32)


def _dot_nt(a, b):
    return lax.dot_general(a, b, (((1,), (1,)), ((), ())), preferred_element_type=F32)


def _dot_tn(a, b):
    return lax.dot_general(a, b, (((0,), (0,)), ((), ())), preferred_element_type=F32)


def _rms(x):
    return x * lax.rsqrt(jnp.mean(x * x, axis=-1, keepdims=True) + EPS)


def _mod_kernel(c_ref, w_ref, b_ref, o_ref):
    c = c_ref[...]
    s = c * jax.nn.sigmoid(c)
    o_ref[...] = _dot(s.astype(BF16), w_ref[...].astype(BF16)) + b_ref[...]


def _modulation(cvecs, ada_w, ada_b):
    depth, d, n = ada_w.shape
    tn = 1024
    return pl.pallas_call(
        _mod_kernel,
        grid=(depth, n // tn),
        in_specs=[
            pl.BlockSpec((8, d), lambda l, j: (0, 0)),
            pl.BlockSpec((None, d, tn), lambda l, j: (l, 0, j)),
            pl.BlockSpec((None, 1, tn), lambda l, j: (l, 0, j)),
        ],
        out_specs=pl.BlockSpec((None, 8, tn), lambda l, j: (l, 0, j)),
        out_shape=jax.ShapeDtypeStruct((depth, 8, n), F32),
        compiler_params=_cparams(("parallel", "parallel")),
        name="adaln_mod",
    )(cvecs, ada_w, ada_b.reshape(depth, 1, n))


def _proj_kernel(patterns, n_bf_tiles, has_f32, has_gates, tn, *refs):
    it = iter(refs)
    x_ref, sh_ref, sc_ref, w_ref, cs_ref, cos_ref, sin_ref = (next(it) for _ in range(7))
    if has_gates:
        wg_ref, gb_ref = next(it), next(it)
    ob_ref = next(it)
    of_ref = next(it) if has_f32 else None
    og_ref = next(it) if has_gates else None
    h_scr = next(it)

    j = pl.program_id(1)

    @pl.when(j == 0)
    def _():
        h = _rms(x_ref[...]) * (1.0 + sc_ref[...]) + sh_ref[...]
        h_scr[...] = h.astype(BF16)
        if has_gates:
            og_ref[...] = _dot(h_scr[...], wg_ref[...]) + gb_ref[...]

    acc = _dot(h_scr[...], w_ref[...])
    cs = cs_ref[...]

    for lo, hi, pat in patterns:
        @pl.when(jnp.logical_and(j >= lo, j < hi))
        def _(pat=pat, lo=lo):
            to_f32 = has_f32 and lo >= n_bf_tiles
            out_ref = of_ref if to_f32 else ob_ref
            for hidx, mode in enumerate(pat):
                sl = slice(hidx * HEAD, (hidx + 1) * HEAD)
                y = acc[:, sl]
                if mode == "R":
                    y = _rms(y) * cs[:, sl]
                    y = y * cos_ref[...] + pltpu.roll(y, HEAD // 2, 1) * sin_ref[...]
                else:
                    y = y * cs[:, sl]
                out_ref[:, sl] = y.astype(out_ref.dtype)


def _proj(x, shift, scale, w, colscale, cos2, sin2, head_modes, n_bf_cols, wg=None, gb=None, tm=1024, tn=512):
    m, d = x.shape
    n = w.shape[1]
    tm = min(tm, m)
    assert m % tm == 0 and n % tn == 0 and n_bf_cols % tn == 0
    nj = n // tn
    n_bf_tiles = n_bf_cols // tn
    has_f32 = n_bf_cols < n
    has_gates = wg is not None
    hpt = tn // HEAD
    tiles = [tuple(head_modes[t * hpt:(t + 1) * hpt]) for t in range(nj)]
    patterns = []
    for t, pat in enumerate(tiles):
        if patterns and patterns[-1][2] == pat and not (has_f32 and t == n_bf_tiles):
            patterns[-1] = (patterns[-1][0], t + 1, pat)
        else:
            patterns.append((t, t + 1, pat))

    in_specs = [
        pl.BlockSpec((tm, d), lambda i, j: (i, 0)),
        pl.BlockSpec((1, d), lambda i, j: (0, 0)),
        pl.BlockSpec((1, d), lambda i, j: (0, 0)),
        pl.BlockSpec((d, tn), lambda i, j: (0, j)),
        pl.BlockSpec((1, tn), lambda i, j: (0, j)),
        pl.BlockSpec((tm, HEAD), lambda i, j: (i, 0)),
        pl.BlockSpec((tm, HEAD), lambda i, j: (i, 0)),
    ]
    args = [x, shift, scale, w, colscale, cos2, sin2]
    if has_gates:
        in_specs += [pl.BlockSpec((d, LANES), lambda i, j: (0, 0)), pl.BlockSpec((1, LANES), lambda i, j: (0, 0))]
        args += [wg, gb]
    last_bf = n_bf_tiles - 1
    out_specs = [pl.BlockSpec((tm, tn), lambda i, j: (i, jnp.minimum(j, last_bf)))]
    out_shape = [jax.ShapeDtypeStruct((m, n_bf_cols), BF16)]
    if has_f32:
        out_specs.append(pl.BlockSpec((tm, tn), lambda i, j: (i, jnp.maximum(j - n_bf_tiles, 0))))
        out_shape.append(jax.ShapeDtypeStruct((m, n - n_bf_cols), F32))
    if has_gates:
        out_specs.append(pl.BlockSpec((tm, LANES), lambda i, j: (i, 0)))
        out_shape.append(jax.ShapeDtypeStruct((m, LANES), F32))
    return pl.pallas_call(
        functools.partial(_proj_kernel, tuple(patterns), n_bf_tiles, has_f32, has_gates, tn),
        grid=(m // tm, nj),
        in_specs=in_specs,
        out_specs=out_specs,
        out_shape=out_shape,
        scratch_shapes=[pltpu.VMEM((tm, d), BF16)],
        compiler_params=_cparams(("parallel", "arbitrary")),
        name="norm_mod_proj",
    )(*args)


def _softmax_pipeline(score_fns, v, row_slices, m_scr, l_scr, acc_scr):
    n = len(score_fns)
    scores, probs = {}, {}
    k = PIPE_SKEW
    for t in range(n + 2 * k):
        if t < n:
            scores[t] = score_fns[t]()
        c = t - k
        if 0 <= c < n:
            probs[c] = _softmax_probs(row_slices[c], m_scr, l_scr, scores.pop(c))
        c = t - 2 * k
        if 0 <= c < n:
            rows = row_slices[c]
            alpha, p = probs.pop(c)
            acc_scr[rows, :] = alpha * acc_scr[rows, :] + _dot(p, v)


def _softmax_probs(rows, m_scr, l_scr, s):
    sub, tk = s.shape
    m_prev = m_scr[rows, :]
    m_new = jnp.maximum(m_prev, jnp.max(s, axis=1, keepdims=True))
    alpha = jnp.exp2(m_prev - m_new)
    m_scr[rows, :] = m_new
    p = jnp.exp2(s - m_new)
    if l_scr is not None:
        lsum = p[:, 0:LANES]
        for c in range(1, tk // LANES):
            lsum = lsum + p[:, c * LANES:(c + 1) * LANES]
        l_scr[rows, :] = alpha * l_scr[rows, :] + lsum
    return alpha, p.astype(BF16)


def _gqa_kernel(tq, sub, q_ref, k_ref, v_ref, o_ref, m_scr, acc_scr):
    kj = pl.program_id(2)

    @pl.when(kj == 0)
    def _():
        m_scr[...] = jnp.full(m_scr.shape, NEG_INF, F32)
        acc_scr[...] = jnp.zeros(acc_scr.shape, F32)

    k = k_ref[...]
    v = jnp.concatenate([v_ref[...], jnp.ones(v_ref.shape, BF16)], axis=1)
    nb = tq // sub
    fns, rows = [], []
    for g in range(GQA_GROUP):
        for b in range(nb):
            fns.append(lambda g=g, b=b: _dot_nt(q_ref[b * sub:(b + 1) * sub, g * HEAD:(g + 1) * HEAD], k))
            rows.append(slice(g * tq + b * sub, g * tq + (b + 1) * sub))
    _softmax_pipeline(fns, v, rows, m_scr, None, acc_scr)

    @pl.when(kj == pl.num_programs(2) - 1)
    def _():
        out = acc_scr[:, :HEAD] / acc_scr[:, HEAD:HEAD + 1]
        for g in range(GQA_GROUP):
            o_ref[:, g * HEAD:(g + 1) * HEAD] = out[g * tq:(g + 1) * tq, :].astype(o_ref.dtype)


def _gqa_attention(q_arr, kv_arr, n_kv, k_col, v_col, tq, tk):
    s_len = q_arr.shape[0]
    lk = kv_arr.shape[0]
    tq = min(tq, s_len)
    tk = min(tk, lk)
    assert s_len % tq == 0 and lk % tk == 0 and tk % LANES == 0
    gw = GQA_GROUP * HEAD
    sub = min(tq, GQA_SUB_ROWS)
    return pl.pallas_call(
        functools.partial(_gqa_kernel, tq, sub),
        grid=(n_kv, s_len // tq, lk // tk),
        in_specs=[
            pl.BlockSpec((tq, gw), lambda n, i, j: (i, n)),
            pl.BlockSpec((tk, HEAD), lambda n, i, j: (j, k_col + n)),
            pl.BlockSpec((tk, HEAD), lambda n, i, j: (j, v_col + n)),
        ],
        out_specs=pl.BlockSpec((tq, gw), lambda n, i, j: (i, n)),
        out_shape=jax.ShapeDtypeStruct((s_len, n_kv * gw), BF16),
        scratch_shapes=[
            pltpu.VMEM((GQA_GROUP * tq, 1), F32),
            pltpu.VMEM((GQA_GROUP * tq, 2 * HEAD), F32),
        ],
        compiler_params=_cparams(("parallel", "parallel", "arbitrary"), ATTN_FLAGS),
        name="gqa_flash",
    )(q_arr, kv_arr, kv_arr)


def _diff_kernel(tq, sub, lam_init, q_ref, k_ref, v_ref, lq1_ref, lk1_ref, lq2_ref, lk2_ref, sg_ref, o_ref,
                 m_scr, l_scr, acc_scr):
    kj = pl.program_id(2)

    @pl.when(kj == 0)
    def _():
        m_scr[...] = jnp.full(m_scr.shape, NEG_INF, F32)
        l_scr[...] = jnp.zeros(l_scr.shape, F32)
        acc_scr[...] = jnp.zeros(acc_scr.shape, F32)

    v = v_ref[...]
    nb = tq // sub
    fns, rows = [], []
    for b in range(nb):
        for r in range(2):
            fns.append(lambda r=r, b=b: _dot_nt(q_ref[b * sub:(b + 1) * sub, r * HEAD:(r + 1) * HEAD],
                                                k_ref[:, r * HEAD:(r + 1) * HEAD]))
            rows.append(slice(r * tq + b * sub, r * tq + (b + 1) * sub))
    _softmax_pipeline(fns, v, rows, m_scr, l_scr, acc_scr)

    @pl.when(kj == pl.num_programs(2) - 1)
    def _():
        lam = (jnp.exp(jnp.sum(lq1_ref[...] * lk1_ref[...], axis=1, keepdims=True))
               - jnp.exp(jnp.sum(lq2_ref[...] * lk2_ref[...], axis=1, keepdims=True)) + lam_init)
        o = acc_scr[...] / jnp.sum(l_scr[...], axis=1, keepdims=True)
        o = o[:tq, :] - lam * o[tq:, :]
        o = _rms(o) * sg_ref[...] * (1.0 - lam_init)
        o_ref[...] = o.astype(o_ref.dtype)


def _diff_attention(q_arr, kv_arr, n_heads, lam_init, lq1, lk1, lq2, lk2, sub_gain, tq, tk):
    s_len = q_arr.shape[0]
    lk = kv_arr.shape[0]
    tq = min(tq, s_len)
    tk = min(tk, lk)
    assert s_len % tq == 0 and lk % tk == 0 and tk % LANES == 0
    sub = min(tq, DIFF_SUB_ROWS)
    vec = pl.BlockSpec((1, HEAD), lambda h, i, j: (0, 0))
    return pl.pallas_call(
        functools.partial(_diff_kernel, tq, sub, lam_init),
        grid=(n_heads, s_len // tq, lk // tk),
        in_specs=[
            pl.BlockSpec((tq, V_HEAD), lambda h, i, j: (i, h)),
            pl.BlockSpec((tk, V_HEAD), lambda h, i, j: (j, n_heads + h)),
            pl.BlockSpec((tk, V_HEAD), lambda h, i, j: (j, 2 * n_heads + h)),
            vec, vec, vec, vec,
            pl.BlockSpec((1, V_HEAD), lambda h, i, j: (0, 0)),
        ],
        out_specs=pl.BlockSpec((tq, V_HEAD), lambda h, i, j: (i, h)),
        out_shape=jax.ShapeDtypeStruct((s_len, n_heads * V_HEAD), BF16),
        scratch_shapes=[
            pltpu.VMEM((2 * tq, 1), F32),
            pltpu.VMEM((2 * tq, LANES), F32),
            pltpu.VMEM((2 * tq, V_HEAD), F32),
        ],
        compiler_params=_cparams(("parallel", "parallel", "arbitrary"), ATTN_FLAGS),
        name="diff_flash",
    )(q_arr, kv_arr, kv_arr, lq1, lk1, lq2, lk2, sub_gain)


def _split3(x):
    hi = x.astype(BF16)
    r1 = x - hi.astype(F32)
    mid = r1.astype(BF16)
    lo = (r1 - mid.astype(F32)).astype(BF16)
    return hi, mid, lo


def _log_sigmoid(x):
    return jnp.minimum(x, 0.0) - jnp.log(1.0 + jnp.exp(-jnp.abs(x)))


def _mlstm_kernel(nh, chunk, qf_ref, kf_ref, vf_ref, gcf_ref, grf_ref, qb_ref, kb_ref, vb_ref, gcb_ref, grb_ref,
                  hf_ref, hb_ref, c_scr, n_scr, m_scr):
    t = pl.program_id(0)

    @pl.when(t == 0)
    def _():
        c_scr[...] = jnp.zeros(c_scr.shape, F32)
        n_scr[...] = jnp.zeros(n_scr.shape, F32)
        m_scr[...] = jnp.full(m_scr.shape, M_INIT, F32)

    row = lax.broadcasted_iota(jnp.int32, (chunk, chunk), 0)
    col = lax.broadcasted_iota(jnp.int32, (chunk, chunk), 1)
    lower = col <= row
    upper = col >= row
    lower_b = lower.astype(BF16)
    upper_b = upper.astype(BF16)

    for rev in (False, True):
        q_ref, k_ref, v_ref, gc_ref, gr_ref, h_ref = (
            (qb_ref, kb_ref, vb_ref, gcb_ref, grb_ref, hb_ref) if rev else
            (qf_ref, kf_ref, vf_ref, gcf_ref, grf_ref, hf_ref))
        mask = upper if rev else lower
        gcol = gc_ref[...]
        grow = gr_ref[...]
        cum_col = sum(_dot(upper_b if rev else lower_b, piece) for piece in _split3(_log_sigmoid(gcol)))
        cum_row = sum(_dot(piece, lower_b if rev else upper_b) for piece in _split3(_log_sigmoid(grow)))
        for hd in range(nh):
            idx = (1 if rev else 0) * nh + hd
            ki = (2 if rev else 0) * nh + hd
            kf = ki + nh
            qc = q_ref[:, hd * HEAD:(hd + 1) * HEAD]
            kc = k_ref[:, hd * HEAD:(hd + 1) * HEAD]
            vc = v_ref[:, hd * V_HEAD:(hd + 1) * V_HEAD]
            i_col = gcol[:, ki:ki + 1]
            i_row = grow[ki:ki + 1, :]
            b_col = cum_col[:, kf:kf + 1]
            b_row = cum_row[kf:kf + 1, :]
            c_mem = c_scr[idx]
            n_mem = n_scr[idx]
            m = m_scr[idx][:, :1]

            dlog = jnp.where(mask, b_col + (i_row - b_row), NEG_INF)
            inter = b_col + m
            m_out = jnp.maximum(inter, jnp.max(dlog, axis=1, keepdims=True))
            w_intra = jnp.exp(dlog - m_out)
            w_inter = jnp.exp(inter - m_out)
            qk = _dot_nt(qc, kc) * w_intra
            num = w_inter * _dot(qc, c_mem.astype(BF16)) + _dot(qk.astype(BF16), vc)
            den = (w_inter * jnp.sum(qc.astype(F32) * n_mem, axis=1, keepdims=True)
                   + jnp.sum(qk, axis=1, keepdims=True))
            h = num / jnp.maximum(jnp.abs(den), jnp.exp(-m_out))
            h_ref[:, hd * V_HEAD:(hd + 1) * V_HEAD] = h

            b_last = b_col[0:1, :] if rev else b_col[chunk - 1:chunk, :]
            g_col = b_last - b_col + i_col
            m_new = jnp.maximum(b_last + m, jnp.max(g_col, axis=0, keepdims=True))
            a_dec = jnp.exp(b_last + m - m_new)
            kw = kc.astype(F32) * jnp.exp(g_col - m_new)
            c_scr[idx] = a_dec * c_mem + _dot_tn(kw.astype(BF16), vc)
            n_scr[idx] = a_dec * n_mem + jnp.sum(kw, axis=0, keepdims=True)
            m_scr[idx] = jnp.broadcast_to(m_new, (1, LANES))


def _mlstm(qkv, gates_col, gates_row, nh, q_col, k_col, v_col, n_lat, n_ctx):
    t_len = qkv.shape[0]
    chunk = MLSTM_CHUNK
    assert n_lat % chunk == 0 and n_ctx % chunk == 0 and t_len == n_lat + n_ctx
    nlat, nctx = n_lat // chunk, n_ctx // chunk
    nc = nlat + nctx

    def fwd(t):
        return jnp.where(t < nctx, nlat + t, t - nctx)

    def bwd(t):
        return jnp.where(t < nctx, nc - 1 - t, nlat - 1 - (t - nctx))

    qw, vw = nh * HEAD, nh * V_HEAD

    def specs(order):
        return [
            pl.BlockSpec((chunk, qw), lambda t: (order(t), q_col)),
            pl.BlockSpec((chunk, qw), lambda t: (order(t), k_col)),
            pl.BlockSpec((chunk, vw), lambda t: (order(t), v_col)),
            pl.BlockSpec((chunk, LANES), lambda t: (order(t), 0)),
            pl.BlockSpec((4 * nh, chunk), lambda t: (0, order(t))),
        ]

    return pl.pallas_call(
        functools.partial(_mlstm_kernel, nh, chunk),
        grid=(nc,),
        in_specs=specs(fwd) + specs(bwd),
        out_specs=[pl.BlockSpec((chunk, vw), lambda t: (fwd(t), 0)),
                   pl.BlockSpec((chunk, vw), lambda t: (bwd(t), 0))],
        out_shape=[jax.ShapeDtypeStruct((t_len, vw), F32)] * 2,
        scratch_shapes=[
            pltpu.VMEM((2 * nh, HEAD, V_HEAD), F32),
            pltpu.VMEM((2 * nh, 1, HEAD), F32),
            pltpu.VMEM((2 * nh, 1, LANES), F32),
        ],
        compiler_params=_cparams(("arbitrary",)),
        name="mlstm_scan",
    )(qkv, qkv, qkv, gates_col, gates_row, qkv, qkv, qkv, gates_col, gates_row)


def _out_kernel(merge, nh, *refs):
    if merge:
        a_ref, hf_ref, hb_ref, o_ref, hg_ref, w_ref, x_ref, g_ref, y_ref, mix_scr = refs
        aw = a_ref.shape[1]

        @pl.when(pl.program_id(1) == 0)
        def _():
            mix_scr[:, :aw] = a_ref[...]
            for hd in range(nh):
                sl = slice(hd * V_HEAD, (hd + 1) * V_HEAD)
                hm = _rms(hf_ref[:, sl] + hb_ref[:, sl]) * hg_ref[:, sl] * jax.nn.sigmoid(o_ref[:, sl])
                mix_scr[:, aw + hd * V_HEAD:aw + (hd + 1) * V_HEAD] = hm.astype(BF16)

        lhs = mix_scr[...]
    else:
        a_ref, w_ref, x_ref, g_ref, y_ref = refs
        lhs = a_ref[...]
    y_ref[...] = x_ref[...] + g_ref[...] * _dot(lhs, w_ref[...])


def _out_proj(x, gate, w, a, merge=None, row_off=0, tm=512, tn=1024):
    m, d = x.shape
    kdim, n = w.shape
    tm = min(tm, m)
    assert m % tm == 0 and n % tn == 0 and row_off % tm == 0
    off = row_off // tm
    in_specs = [pl.BlockSpec((tm, a.shape[1]), lambda i, j: (i, 0))]
    args = [a]
    scratch = []
    nh = 0
    if merge is not None:
        hf, hb, o, hg, nh = merge
        hw = hf.shape[1]
        in_specs += [pl.BlockSpec((tm, hw), lambda i, j: (i + off, 0)),
                     pl.BlockSpec((tm, hw), lambda i, j: (i + off, 0)),
                     pl.BlockSpec((tm, hw), lambda i, j: (i, 0)),
                     pl.BlockSpec((1, hw), lambda i, j: (0, 0))]
        args += [hf, hb, o, hg]
        scratch = [pltpu.VMEM((tm, kdim), BF16)]
    in_specs += [pl.BlockSpec((kdim, tn), lambda i, j: (0, j)),
                 pl.BlockSpec((tm, tn), lambda i, j: (i, j)),
                 pl.BlockSpec((1, tn), lambda i, j: (0, j))]
    args += [w, x, gate]
    return pl.pallas_call(
        functools.partial(_out_kernel, merge is not None, nh),
        grid=(m // tm, n // tn),
        in_specs=in_specs,
        out_specs=pl.BlockSpec((tm, tn), lambda i, j: (i, j)),
        out_shape=jax.ShapeDtypeStruct((m, n), F32),
        scratch_shapes=scratch,
        compiler_params=_cparams(("parallel", "arbitrary")),
        name="out_proj_residual",
    )(*args)


def _ffn_kernel(x_ref, sh_ref, sc_ref, g_ref, wg_ref, wu_ref, wd_ref, y_ref, h_scr, acc_scr):
    f = pl.program_id(1)

    @pl.when(f == 0)
    def _():
        h = _rms(x_ref[...]) * (1.0 + sc_ref[...]) + sh_ref[...]
        h_scr[...] = h.astype(BF16)
        acc_scr[...] = jnp.zeros(acc_scr.shape, F32)

    h = h_scr[...]
    gt = _dot(h, wg_ref[...])
    up = _dot(h, wu_ref[...])
    act = (gt * jax.nn.sigmoid(gt)) * up
    acc_scr[...] += _dot(act.astype(BF16), wd_ref[...])

    @pl.when(f == pl.num_programs(1) - 1)
    def _():
        y_ref[...] = x_ref[...] + g_ref[...] * acc_scr[...]


def _ffn_dense(x, shift, scale, gate, wg, wu, wd, tm=512, tf=512):
    m, d = x.shape
    ff = wg.shape[1]
    tm = min(tm, m)
    assert m % tm == 0 and ff % tf == 0
    vec = pl.BlockSpec((1, d), lambda i, f: (0, 0))
    return pl.pallas_call(
        _ffn_kernel,
        grid=(m // tm, ff // tf),
        in_specs=[
            pl.BlockSpec((tm, d), lambda i, f: (i, 0)), vec, vec, vec,
            pl.BlockSpec((d, tf), lambda i, f: (0, f)),
            pl.BlockSpec((d, tf), lambda i, f: (0, f)),
            pl.BlockSpec((tf, d), lambda i, f: (f, 0)),
        ],
        out_specs=pl.BlockSpec((tm, d), lambda i, f: (i, 0)),
        out_shape=jax.ShapeDtypeStruct((m, d), F32),
        scratch_shapes=[pltpu.VMEM((tm, d), BF16), pltpu.VMEM((tm, d), F32)],
        compiler_params=_cparams(("parallel", "arbitrary")),
        name="ffn_dense",
    )(x, shift, scale, gate, wg, wu, wd)


R_E1, R_E2, R_W1, R_W2, R_RANK1, R_RANK2 = range(6)


def _lane_pick(a, lane, k):
    return jnp.sum(jnp.where(lane == k, a, 0.0), axis=1, keepdims=True)


def _router_kernel(n_exp, x_ref, sh_ref, sc_ref, r_ref, info_ref, cnt_ref, carry_scr):
    @pl.when(pl.program_id(0) == 0)
    def _():
        carry_scr[...] = jnp.zeros(carry_scr.shape, F32)

    tm = x_ref.shape[0]
    h = _rms(x_ref[...]) * (1.0 + sc_ref[...]) + sh_ref[...]
    logits = jnp.dot(h, r_ref[...], preferred_element_type=F32, precision=lax.Precision.HIGHEST)
    lane = lax.broadcasted_iota(jnp.int32, logits.shape, 1)
    logits = jnp.where(lane < n_exp, logits, NEG_INF)
    v1 = jnp.max(logits, axis=1, keepdims=True)
    i1 = jnp.min(jnp.where(logits == v1, lane, LANES), axis=1, keepdims=True)
    rest = jnp.where(lane == i1, NEG_INF, logits)
    v2 = jnp.max(rest, axis=1, keepdims=True)
    i2 = jnp.min(jnp.where(rest == v2, lane, LANES), axis=1, keepdims=True)
    e2 = jnp.exp(v2 - v1)
    w1 = 1.0 / (1.0 + e2)
    w2 = e2 / (1.0 + e2)

    chosen = jnp.where(jnp.logical_or(lane == i1, lane == i2), 1.0, 0.0)
    r_i = lax.broadcasted_iota(jnp.int32, (tm, tm), 0)
    c_i = lax.broadcasted_iota(jnp.int32, (tm, tm), 1)
    before = (c_i < r_i).astype(BF16)
    rank = _dot(before, chosen.astype(BF16)) + carry_scr[...]
    carry_scr[...] += jnp.sum(chosen, axis=0, keepdims=True)
    cnt_ref[...] = carry_scr[...]
    rank1 = jnp.sum(jnp.where(lane == i1, rank, 0.0), axis=1, keepdims=True)
    rank2 = jnp.sum(jnp.where(lane == i2, rank, 0.0), axis=1, keepdims=True)
    info = jnp.zeros(logits.shape, F32)
    for k, val in ((R_E1, i1.astype(F32)), (R_E2, i2.astype(F32)), (R_W1, w1), (R_W2, w2),
                   (R_RANK1, rank1), (R_RANK2, rank2)):
        info = jnp.where(lane == k, val, info)
    info_ref[...] = info


def _router(x, shift, scale, router, tm=512):
    m, d = x.shape
    n_exp = router.shape[1]
    vec = pl.BlockSpec((1, d), lambda i: (0, 0))
    router_pad = jnp.zeros((d, LANES), F32).at[:, :n_exp].set(router)
    return pl.pallas_call(
        functools.partial(_router_kernel, n_exp),
        grid=(m // tm,),
        in_specs=[pl.BlockSpec((tm, d), lambda i: (i, 0)), vec, vec, pl.BlockSpec((d, LANES), lambda i: (0, 0))],
        out_specs=[pl.BlockSpec((tm, LANES), lambda i: (i, 0)), pl.BlockSpec((1, LANES), lambda i: (0, 0))],
        out_shape=[jax.ShapeDtypeStruct((m, LANES), F32), jax.ShapeDtypeStruct((1, LANES), F32)],
        scratch_shapes=[pltpu.VMEM((1, LANES), F32)],
        compiler_params=_cparams(("arbitrary",)),
        name="moe_router",
    )(x, shift, scale, router_pad)


def _gather_rows(src_hbm, idx_ref, buf, sem, start):
    n_rows = buf.shape[0]
    if not start:
        pltpu.make_async_copy(src_hbm.at[pl.ds(0, n_rows), :], buf, sem).wait()
        return

    def body(r, carry):
        pltpu.make_async_copy(src_hbm.at[pl.ds(idx_ref[0, r], 1), :], buf.at[pl.ds(r, 1), :], sem).start()
        return carry
    lax.fori_loop(0, n_rows, body, 0, unroll=GATHER_UNROLL)


def _moe_ffn_kernel(te_ref, nu_ref, src_ref, nsrc_ref, x_hbm, sh_ref, sc_ref, wg_ref, wu_ref, wd_ref, y_ref,
                    xbuf, hbuf, sem):
    i = pl.program_id(0)
    f = pl.program_id(1)
    n_used = nu_ref[0]
    active = i < n_used

    @pl.when(f == 0)
    def _():
        y_ref[...] = jnp.zeros(y_ref.shape, F32)

    @pl.when(jnp.logical_and(active, f == 0))
    def _():
        pl.when(i == 0)(lambda: _gather_rows(x_hbm, src_ref, xbuf, sem, True))
        _gather_rows(x_hbm, src_ref, xbuf, sem, False)
        h = _rms(xbuf[...]) * (1.0 + sc_ref[...]) + sh_ref[...]
        hbuf[...] = h.astype(BF16)
        pl.when(i + 1 < n_used)(lambda: _gather_rows(x_hbm, nsrc_ref, xbuf, sem, True))

    @pl.when(active)
    def _():
        h = hbuf[...]
        gt = _dot(h, wg_ref[...])
        up = _dot(h, wu_ref[...])
        act = (gt * jax.nn.sigmoid(gt)) * up
        y_ref[...] += _dot(act.astype(BF16), wd_ref[...])


def _moe_ffn(x, shift, scale, tile_exp, n_used, src, wg, wu, wd, tm, tf=1024):
    m, d = x.shape
    n_exp, _, ff = wg.shape
    n_tiles = src.shape[0]
    tf = min(tf, ff)
    assert ff % tf == 0
    nf = ff // tf
    vec = pl.BlockSpec((1, d), lambda i, f, te, nu: (0, 0))
    idx = pl.BlockSpec((None, 1, tm), lambda i, f, te, nu: (i, 0, 0), memory_space=pltpu.SMEM)
    idx_next = pl.BlockSpec((None, 1, tm), lambda i, f, te, nu: (jnp.minimum(i + 1, n_tiles - 1), 0, 0),
                            memory_space=pltpu.SMEM)

    def f_eff(i, f, nu):
        return jnp.where(i < nu[0], f, nf - 1)

    grid_spec = pltpu.PrefetchScalarGridSpec(
        num_scalar_prefetch=2,
        grid=(n_tiles, nf),
        in_specs=[
            idx, idx_next,
            pl.BlockSpec(memory_space=pl.ANY),
            vec, vec,
            pl.BlockSpec((None, d, tf), lambda i, f, te, nu: (te[i], 0, f_eff(i, f, nu))),
            pl.BlockSpec((None, d, tf), lambda i, f, te, nu: (te[i], 0, f_eff(i, f, nu))),
            pl.BlockSpec((None, tf, d), lambda i, f, te, nu: (te[i], f_eff(i, f, nu), 0)),
        ],
        out_specs=pl.BlockSpec((tm, d), lambda i, f, te, nu: (i, 0)),
        scratch_shapes=[pltpu.VMEM((tm, d), F32), pltpu.VMEM((tm, d), BF16), pltpu.SemaphoreType.DMA(())],
    )
    return pl.pallas_call(
        _moe_ffn_kernel,
        grid_spec=grid_spec,
        out_shape=jax.ShapeDtypeStruct((n_tiles * tm, d), F32),
        compiler_params=_cparams(("arbitrary", "arbitrary")),
        name="moe_grouped_ffn",
    )(tile_exp, n_used, src, src, x, shift, scale, wg, wu, wd)


def _moe_combine_kernel(pos_ref, npos_ref, x_ref, g_ref, info_ref, y_hbm, o_ref, ybuf, sem):
    i = pl.program_id(0)
    tm = x_ref.shape[0]
    slot = i % 2
    pl.when(i == 0)(lambda: _gather_rows(y_hbm, pos_ref, ybuf.at[0], sem.at[0], True))
    for s in range(2):
        @pl.when(slot == s)
        def _(s=s):
            _gather_rows(y_hbm, pos_ref, ybuf.at[s], sem.at[s], False)
            pl.when(i + 1 < pl.num_programs(0))(
                lambda: _gather_rows(y_hbm, npos_ref, ybuf.at[1 - s], sem.at[1 - s], True))
            info = info_ref[...]
            lane = lax.broadcasted_iota(jnp.int32, info.shape, 1)
            w1 = _lane_pick(info, lane, R_W1)
            w2 = _lane_pick(info, lane, R_W2)
            o_ref[...] = x_ref[...] + g_ref[...] * (w1 * ybuf[s, :tm, :] + w2 * ybuf[s, tm:, :])


def _moe_combine(x, gate, info, y, pos, tm=256):
    m, d = x.shape
    nb = m // tm
    assert m % tm == 0 and pos.shape == (nb, 1, 2 * tm)
    idx = pl.BlockSpec((None, 1, 2 * tm), lambda i: (i, 0, 0), memory_space=pltpu.SMEM)
    idx_next = pl.BlockSpec((None, 1, 2 * tm), lambda i: (jnp.minimum(i + 1, nb - 1), 0, 0), memory_space=pltpu.SMEM)
    return pl.pallas_call(
        _moe_combine_kernel,
        grid=(nb,),
        in_specs=[
            idx, idx_next,
            pl.BlockSpec((tm, d), lambda i: (i, 0)),
            pl.BlockSpec((1, d), lambda i: (0, 0)),
            pl.BlockSpec((tm, LANES), lambda i: (i, 0)),
            pl.BlockSpec(memory_space=pl.ANY),
        ],
        out_specs=pl.BlockSpec((tm, d), lambda i: (i, 0)),
        out_shape=jax.ShapeDtypeStruct((m, d), F32),
        scratch_shapes=[pltpu.VMEM((2, 2 * tm, d), F32), pltpu.SemaphoreType.DMA((2,))],
        compiler_params=_cparams(("arbitrary",)),
        name="moe_combine",
    )(pos, pos, x, gate, info, y)


def _moe_top2(x, shift, scale, gate, router, wg, wu, wd, tm=512):
    m, d = x.shape
    n_exp = router.shape[1]
    info, counts = _router(x, shift, scale, router)
    e1, e2 = info[:, R_E1].astype(jnp.int32), info[:, R_E2].astype(jnp.int32)
    r1, r2 = info[:, R_RANK1].astype(jnp.int32), info[:, R_RANK2].astype(jnp.int32)
    cnt = counts[0, :n_exp].astype(jnp.int32)
    padded = (cnt + tm - 1) // tm * tm
    ends = jnp.cumsum(padded)
    offs = ends - padded
    pos1, pos2 = offs[e1] + r1, offs[e2] + r2
    n_tiles = TOP_K * m // tm + n_exp
    n_rows = n_tiles * tm
    tok = jnp.arange(m, dtype=jnp.int32)
    src = jnp.zeros((n_rows,), jnp.int32).at[jnp.concatenate([pos1, pos2])].set(
        jnp.concatenate([tok, tok]), unique_indices=True)
    n_used = ends[-1] // tm
    tile_start = jnp.arange(n_tiles, dtype=jnp.int32) * tm
    tile_exp = jnp.searchsorted(ends, jnp.minimum(tile_start, ends[-1] - 1), side="right").astype(jnp.int32)
    tile_exp = jnp.minimum(tile_exp, n_exp - 1)
    y = _moe_ffn(x, shift, scale, tile_exp, n_used.reshape(1).astype(jnp.int32), src.reshape(n_tiles, 1, tm),
                 wg, wu, wd, tm)
    tc = min(COMBINE_ROWS, m)
    pos = jnp.concatenate([pos1.reshape(m // tc, 1, tc), pos2.reshape(m // tc, 1, tc)], axis=2)
    return _moe_combine(x, gate, info, y, pos, tc)


def _rope_tables(rows):
    n_freq = HEAD // 4
    inv = ROPE_THETA ** (-jnp.arange(n_freq, dtype=F32) / n_freq)
    ang_r = jnp.arange(rows, dtype=F32)[:, None] * inv
    ang_c = jnp.arange(GRID_W, dtype=F32)[:, None] * inv

    def full(fn):
        t = jnp.concatenate([jnp.broadcast_to(fn(ang_r)[:, None, :], (rows, GRID_W, n_freq)),
                             jnp.broadcast_to(fn(ang_c)[None, :, :], (rows, GRID_W, n_freq))], axis=-1)
        return t.reshape(rows * GRID_W, 2 * n_freq)

    cos, sin = full(jnp.cos), full(jnp.sin)
    return jnp.concatenate([cos, cos], axis=-1), jnp.concatenate([-sin, sin], axis=-1)


def _halves(a):
    shp = a.shape
    a = a.reshape(shp[:-1] + (shp[-1] // HEAD, HEAD // 2, 2))
    return jnp.swapaxes(a, -1, -2).reshape(shp)


def kernel(x, c, ctx, c_ctx, ada_w, ada_b, ev_w_in, ev_gate_b, ev_q_gain, ev_k_gain, ev_h_gain, ev_w_out,
           ev_ff_gate, ev_ff_up, ev_ff_down, od_w_in, od_q_gain, od_k_gain, od_lam_q1, od_lam_k1, od_lam_q2,
           od_lam_k2, od_sub_gain, od_w_out, od_router, od_ex_gate, od_ex_up, od_ex_down):
    bsz, s_len, d = x.shape
    n_ctx = ctx.shape[1]
    assert bsz == 1 and ada_w.shape[0] == 2
    a_heads = d // (2 * HEAD)
    a_kv = a_heads // GQA_GROUP
    b_heads = d // (2 * V_HEAD)
    c_heads = d // V_HEAD
    a_q, a_kvw, b_qk, b_v = a_heads * HEAD, a_kv * HEAD, b_heads * HEAD, b_heads * V_HEAD
    score_scale = HEAD ** -0.5 * math.log2(math.e)

    x_lat, x_ctx = x[0], ctx[0]
    cvecs = jnp.zeros((8, d), F32).at[0].set(c[0]).at[1].set(c_ctx)
    mods = _modulation(cvecs, ada_w, ada_b)

    def mod(layer, who):
        return [mods[layer, who, k * d:(k + 1) * d][None, :] for k in range(6)]

    cos_l, sin_l = _rope_tables(s_len // GRID_W)
    cos_c, sin_c = jnp.ones((n_ctx, HEAD), F32), jnp.zeros((n_ctx, HEAD), F32)

    w_in = ev_w_in[0]
    o0 = 0
    wqa = w_in[:, o0:o0 + a_q]; o0 += a_q
    wka = w_in[:, o0:o0 + a_kvw]; o0 += a_kvw
    wva = w_in[:, o0:o0 + a_kvw]; o0 += a_kvw
    wqb = w_in[:, o0:o0 + b_qk]; o0 += b_qk
    wkb = w_in[:, o0:o0 + b_qk]; o0 += b_qk
    wvb = w_in[:, o0:o0 + b_v]; o0 += b_v
    wob = w_in[:, o0:o0 + b_v]; o0 += b_v
    wgt = w_in[:, o0:]
    n_gates = wgt.shape[1]
    w0 = jnp.concatenate([_halves(wqa), wvb, wqb, wkb, _halves(wka), wva, wob], axis=1).astype(BF16)
    ones = lambda n: jnp.ones((n,), F32)
    cs0 = jnp.concatenate([
        jnp.tile(_halves(ev_q_gain[0]), a_heads) * score_scale, ones(b_v), ones(b_qk) * (HEAD ** -0.5), ones(b_qk),
        jnp.tile(_halves(ev_k_gain[0]), a_kv), ones(a_kvw), ones(b_v)])[None, :]
    modes0 = (["R"] * a_heads + ["P"] * (b_v // HEAD) + ["P"] * (2 * b_qk // HEAD) + ["R"] * a_kv
              + ["P"] * a_kv + ["P"] * (b_v // HEAD))
    nbf0 = a_q + b_v + 2 * b_qk + 2 * a_kvw
    wg0 = jnp.zeros((d, LANES), F32).at[:, :n_gates].set(wgt).astype(BF16)
    gb0 = jnp.zeros((1, LANES), F32).at[0, :n_gates].set(ev_gate_b[0])
    col_vb, col_qb, col_kb = a_q // b_v, (a_q + b_v) // b_qk, (a_q + b_v + b_qk) // b_qk
    col_ka = (a_q + b_v + 2 * b_qk) // HEAD
    col_va = col_ka + a_kv

    ml, mc = mod(0, 0), mod(0, 1)
    pl_b, pl_o, pl_g = _proj(x_lat, ml[0], ml[1], w0, cs0, cos_l, sin_l, modes0, nbf0, wg0, gb0)
    pc_b, pc_o, pc_g = _proj(x_ctx, mc[0], mc[1], w0, cs0, cos_c, sin_c, modes0, nbf0, wg0, gb0)
    p_all = jnp.concatenate([pl_b, pc_b], axis=0)
    g_all = jnp.concatenate([pl_g, pc_g], axis=0)
    g_row = g_all[:, :4 * b_heads].T

    att_l = _gqa_attention(pl_b, p_all, a_kv, col_ka, col_va, tq=512, tk=3328)
    att_c = _gqa_attention(pc_b, pc_b, a_kv, col_ka, col_va, tq=256, tk=256)
    hf, hb = _mlstm(p_all, g_all, g_row, b_heads, col_qb, col_kb, col_vb, s_len, n_ctx)

    hg = ev_h_gain[0].reshape(1, b_v)
    w_out0 = ev_w_out[0].astype(BF16)
    x_lat = _out_proj(x_lat, ml[2], w_out0, att_l, merge=(hf, hb, pl_o, hg, b_heads))
    x_ctx = _out_proj(x_ctx, mc[2], w_out0, att_c, merge=(hf, hb, pc_o, hg, b_heads), row_off=s_len)
    ffw = (ev_ff_gate[0].astype(BF16), ev_ff_up[0].astype(BF16), ev_ff_down[0].astype(BF16))
    x_lat = _ffn_dense(x_lat, ml[3], ml[4], ml[5], *ffw)
    x_ctx = _ffn_dense(x_ctx, mc[3], mc[4], mc[5], *ffw)

    c_qk = c_heads * 2 * HEAD
    w1 = od_w_in[0]
    w1 = jnp.concatenate([_halves(w1[:, :2 * c_qk]), w1[:, 2 * c_qk:]], axis=1).astype(BF16)
    cs1 = jnp.concatenate([jnp.tile(_halves(od_q_gain[0]), 2 * c_heads) * score_scale,
                           jnp.tile(_halves(od_k_gain[0]), 2 * c_heads), ones(c_heads * V_HEAD)])[None, :]
    modes1 = ["R"] * (4 * c_heads) + ["P"] * (c_heads * V_HEAD // HEAD)
    ml, mc = mod(1, 0), mod(1, 1)
    (ql,) = _proj(x_lat, ml[0], ml[1], w1, cs1, cos_l, sin_l, modes1, w1.shape[1])
    (qc,) = _proj(x_ctx, mc[0], mc[1], w1, cs1, cos_c, sin_c, modes1, w1.shape[1])
    kv_all = jnp.concatenate([ql, qc], axis=0)
    lam_init = 0.8 - 0.6 * math.exp(-0.3 * 1)
    vec = lambda a: a[0].reshape(1, -1)
    o_l = _diff_attention(ql, kv_all, c_heads, lam_init, vec(od_lam_q1), vec(od_lam_k1), vec(od_lam_q2),
                          vec(od_lam_k2), vec(od_sub_gain), tq=1024, tk=3328)
    x_lat = _out_proj(x_lat, ml[2], od_w_out[0].astype(BF16), o_l)
    x_lat = _moe_top2(x_lat, ml[3], ml[4], ml[5], od_router[0], od_ex_gate[0].astype(BF16),
                      od_ex_up[0].astype(BF16), od_ex_down[0].astype(BF16))
    return x_lat[None]
```

```python
import functools
import math

import jax
import jax.numpy as jnp
from jax import lax
from jax.experimental import pallas as pl
from jax.experimental.pallas import tpu as pltpu

F32 = jnp.float32
BF16 = jnp.bfloat16

GRID_W = 64
ROPE_THETA = 10000.0
EPS = 1e-6
M_INIT = -1e30
HEAD = 128
V_HEAD = 256
GQA_GROUP = 4
TOP_K = 2
MLSTM_CHUNK = 256
GQA_SUB_ROWS = 256
DIFF_SUB_ROWS = 128
GATHER_UNROLL = 8
COMBINE_ROWS = 256
PIPE_SKEW = 1
LANES = 128
VMEM_LIMIT = 60 * 1024 * 1024
NEG_INF = float("-inf")


def _cparams(sem, flags=None):
    return pltpu.CompilerParams(dimension_semantics=sem, vmem_limit_bytes=VMEM_LIMIT, flags=flags)


ATTN_FLAGS = None


def _dot(a, b):
    return jnp.dot(a, b, preferred_element_type=F32)


def _dot_nt(a, b):
    return lax.dot_general(a, b, (((1,), (1,)), ((), ())), preferred_element_type=F32)


def _dot_tn(a, b):
    return lax.dot_general(a, b, (((0,), (0,)), ((), ())), preferred_element_type=F32)


def _rms(x):
    return x * lax.rsqrt(jnp.mean(x * x, axis=-1, keepdims=True) + EPS)


def _mod_kernel(c_ref, w_ref, b_ref, o_ref):
    c = c_ref[...]
    s = c * jax.nn.sigmoid(c)
    o_ref[...] = _dot(s.astype(BF16), w_ref[...].astype(BF16)) + b_ref[...]


def _modulation(cvecs, ada_w, ada_b):
    depth, d, n = ada_w.shape
    tn = 1024
    return pl.pallas_call(
        _mod_kernel,
        grid=(depth, n // tn),
        in_specs=[
            pl.BlockSpec((8, d), lambda l, j: (0, 0)),
            pl.BlockSpec((None, d, tn), lambda l, j: (l, 0, j)),
            pl.BlockSpec((None, 1, tn), lambda l, j: (l, 0, j)),
        ],
        out_specs=pl.BlockSpec((None, 8, tn), lambda l, j: (l, 0, j)),
        out_shape=jax.ShapeDtypeStruct((depth, 8, n), F32),
        compiler_params=_cparams(("parallel", "parallel")),
        name="adaln_mod",
    )(cvecs, ada_w, ada_b.reshape(depth, 1, n))


def _proj_kernel(patterns, n_bf_tiles, has_f32, has_gates, tn, *refs):
    it = iter(refs)
    x_ref, sh_ref, sc_ref, w_ref, cs_ref, cos_ref, sin_ref = (next(it) for _ in range(7))
    if has_gates:
        wg_ref, gb_ref = next(it), next(it)
    ob_ref = next(it)
    of_ref = next(it) if has_f32 else None
    og_ref = next(it) if has_gates else None
    h_scr = next(it)

    j = pl.program_id(1)

    @pl.when(j == 0)
    def _():
        h = _rms(x_ref[...]) * (1.0 + sc_ref[...]) + sh_ref[...]
        h_scr[...] = h.astype(BF16)
        if has_gates:
            og_ref[...] = _dot(h_scr[...], wg_ref[...]) + gb_ref[...]

    acc = _dot(h_scr[...], w_ref[...])
    cs = cs_ref[...]

    for lo, hi, pat in patterns:
        @pl.when(jnp.logical_and(j >= lo, j < hi))
        def _(pat=pat, lo=lo):
            to_f32 = has_f32 and lo >= n_bf_tiles
            out_ref = of_ref if to_f32 else ob_ref
            for hidx, mode in enumerate(pat):
                sl = slice(hidx * HEAD, (hidx + 1) * HEAD)
                y = acc[:, sl]
                if mode == "R":
                    y = _rms(y) * cs[:, sl]
                    y = y * cos_ref[...] + pltpu.roll(y, HEAD // 2, 1) * sin_ref[...]
                else:
                    y = y * cs[:, sl]
                out_ref[:, sl] = y.astype(out_ref.dtype)


def _proj(x, shift, scale, w, colscale, cos2, sin2, head_modes, n_bf_cols, wg=None, gb=None, tm=1024, tn=512):
    m, d = x.shape
    n = w.shape[1]
    tm = min(tm, m)
    assert m % tm == 0 and n % tn == 0 and n_bf_cols % tn == 0
    nj = n // tn
    n_bf_tiles = n_bf_cols // tn
    has_f32 = n_bf_cols < n
    has_gates = wg is not None
    hpt = tn // HEAD
    tiles = [tuple(head_modes[t * hpt:(t + 1) * hpt]) for t in range(nj)]
    patterns = []
    for t, pat in enumerate(tiles):
        if patterns and patterns[-1][2] == pat and not (has_f32 and t == n_bf_tiles):
            patterns[-1] = (patterns[-1][0], t + 1, pat)
        else:
            patterns.append((t, t + 1, pat))

    in_specs = [
        pl.BlockSpec((tm, d), lambda i, j: (i, 0)),
        pl.BlockSpec((1, d), lambda i, j: (0, 0)),
        pl.BlockSpec((1, d), lambda i, j: (0, 0)),
        pl.BlockSpec((d, tn), lambda i, j: (0, j)),
        pl.BlockSpec((1, tn), lambda i, j: (0, j)),
        pl.BlockSpec((tm, HEAD), lambda i, j: (i, 0)),
        pl.BlockSpec((tm, HEAD), lambda i, j: (i, 0)),
    ]
    args = [x, shift, scale, w, colscale, cos2, sin2]
    if has_gates:
        in_specs += [pl.BlockSpec((d, LANES), lambda i, j: (0, 0)), pl.BlockSpec((1, LANES), lambda i, j: (0, 0))]
        args += [wg, gb]
    last_bf = n_bf_tiles - 1
    out_specs = [pl.BlockSpec((tm, tn), lambda i, j: (i, jnp.minimum(j, last_bf)))]
    out_shape = [jax.ShapeDtypeStruct((m, n_bf_cols), BF16)]
    if has_f32:
        out_specs.append(pl.BlockSpec((tm, tn), lambda i, j: (i, jnp.maximum(j - n_bf_tiles, 0))))
        out_shape.append(jax.ShapeDtypeStruct((m, n - n_bf_cols), F32))
    if has_gates:
        out_specs.append(pl.BlockSpec((tm, LANES), lambda i, j: (i, 0)))
        out_shape.append(jax.ShapeDtypeStruct((m, LANES), F32))
    return pl.pallas_call(
        functools.partial(_proj_kernel, tuple(patterns), n_bf_tiles, has_f32, has_gates, tn),
        grid=(m // tm, nj),
        in_specs=in_specs,
        out_specs=out_specs,
        out_shape=out_shape,
        scratch_shapes=[pltpu.VMEM((tm, d), BF16)],
        compiler_params=_cparams(("parallel", "arbitrary")),
        name="norm_mod_proj",
    )(*args)


def _softmax_pipeline(score_fns, v, row_slices, m_scr, l_scr, acc_scr):
    n = len(score_fns)
    scores, probs = {}, {}
    k = PIPE_SKEW
    for t in range(n + 2 * k):
        if t < n:
            scores[t] = score_fns[t]()
        c = t - k
        if 0 <= c < n:
            probs[c] = _softmax_probs(row_slices[c], m_scr, l_scr, scores.pop(c))
        c = t - 2 * k
        if 0 <= c < n:
            rows = row_slices[c]
            alpha, p = probs.pop(c)
            acc_scr[rows, :] = alpha * acc_scr[rows, :] + _dot(p, v)


def _softmax_probs(rows, m_scr, l_scr, s):
    sub, tk = s.shape
    m_prev = m_scr[rows, :]
    m_new = jnp.maximum(m_prev, jnp.max(s, axis=1, keepdims=True))
    alpha = jnp.exp2(m_prev - m_new)
    m_scr[rows, :] = m_new
    p = jnp.exp2(s - m_new)
    if l_scr is not None:
        lsum = p[:, 0:LANES]
        for c in range(1, tk // LANES):
            lsum = lsum + p[:, c * LANES:(c + 1) * LANES]
        l_scr[rows, :] = alpha * l_scr[rows, :] + lsum
    return alpha, p.astype(BF16)


def _gqa_kernel(tq, sub, q_ref, k_ref, v_ref, o_ref, m_scr, acc_scr):
    kj = pl.program_id(2)

    @pl.when(kj == 0)
    def _():
        m_scr[...] = jnp.full(m_scr.shape, NEG_INF, F32)
        acc_scr[...] = jnp.zeros(acc_scr.shape, F32)

    k = k_ref[...]
    v = jnp.concatenate([v_ref[...], jnp.ones(v_ref.shape, BF16)], axis=1)
    nb = tq // sub
    fns, rows = [], []
    for g in range(GQA_GROUP):
        for b in range(nb):
            fns.append(lambda g=g, b=b: _dot_nt(q_ref[b * sub:(b + 1) * sub, g * HEAD:(g + 1) * HEAD], k))
            rows.append(slice(g * tq + b * sub, g * tq + (b + 1) * sub))
    _softmax_pipeline(fns, v, rows, m_scr, None, acc_scr)

    @pl.when(kj == pl.num_programs(2) - 1)
    def _():
        out = acc_scr[:, :HEAD] / acc_scr[:, HEAD:HEAD + 1]
        for g in range(GQA_GROUP):
            o_ref[:, g * HEAD:(g + 1) * HEAD] = out[g * tq:(g + 1) * tq, :].astype(o_ref.dtype)


def _gqa_attention(q_arr, kv_arr, n_kv, k_col, v_col, tq, tk):
    s_len = q_arr.shape[0]
    lk = kv_arr.shape[0]
    tq = min(tq, s_len)
    tk = min(tk, lk)
    assert s_len % tq == 0 and lk % tk == 0 and tk % LANES == 0
    gw = GQA_GROUP * HEAD
    sub = min(tq, GQA_SUB_ROWS)
    return pl.pallas_call(
        functools.partial(_gqa_kernel, tq, sub),
        grid=(n_kv, s_len // tq, lk // tk),
        in_specs=[
            pl.BlockSpec((tq, gw), lambda n, i, j: (i, n)),
            pl.BlockSpec((tk, HEAD), lambda n, i, j: (j, k_col + n)),
            pl.BlockSpec((tk, HEAD), lambda n, i, j: (j, v_col + n)),
        ],
        out_specs=pl.BlockSpec((tq, gw), lambda n, i, j: (i, n)),
        out_shape=jax.ShapeDtypeStruct((s_len, n_kv * gw), BF16),
        scratch_shapes=[
            pltpu.VMEM((GQA_GROUP * tq, 1), F32),
            pltpu.VMEM((GQA_GROUP * tq, 2 * HEAD), F32),
        ],
        compiler_params=_cparams(("parallel", "parallel", "arbitrary"), ATTN_FLAGS),
        name="gqa_flash",
    )(q_arr, kv_arr, kv_arr)


def _diff_kernel(tq, sub, lam_init, q_ref, k_ref, v_ref, lq1_ref, lk1_ref, lq2_ref, lk2_ref, sg_ref, o_ref,
                 m_scr, l_scr, acc_scr):
    kj = pl.program_id(2)

    @pl.when(kj == 0)
    def _():
        m_scr[...] = jnp.full(m_scr.shape, NEG_INF, F32)
        l_scr[...] = jnp.zeros(l_scr.shape, F32)
        acc_scr[...] = jnp.zeros(acc_scr.shape, F32)

    v = v_ref[...]
    nb = tq // sub
    fns, rows = [], []
    for b in range(nb):
        for r in range(2):
            fns.append(lambda r=r, b=b: _dot_nt(q_ref[b * sub:(b + 1) * sub, r * HEAD:(r + 1) * HEAD],
                                                k_ref[:, r * HEAD:(r + 1) * HEAD]))
            rows.append(slice(r * tq + b * sub, r * tq + (b + 1) * sub))
    _softmax_pipeline(fns, v, rows, m_scr, l_scr, acc_scr)

    @pl.when(kj == pl.num_programs(2) - 1)
    def _():
        lam = (jnp.exp(jnp.sum(lq1_ref[...] * lk1_ref[...], axis=1, keepdims=True))
               - jnp.exp(jnp.sum(lq2_ref[...] * lk2_ref[...], axis=1, keepdims=True)) + lam_init)
        o = acc_scr[...] / jnp.sum(l_scr[...], axis=1, keepdims=True)
        o = o[:tq, :] - lam * o[tq:, :]
        o = _rms(o) * sg_ref[...] * (1.0 - lam_init)
        o_ref[...] = o.astype(o_ref.dtype)


def _diff_attention(q_arr, kv_arr, n_heads, lam_init, lq1, lk1, lq2, lk2, sub_gain, tq, tk):
    s_len = q_arr.shape[0]
    lk = kv_arr.shape[0]
    tq = min(tq, s_len)
    tk = min(tk, lk)
    assert s_len % tq == 0 and lk % tk == 0 and tk % LANES == 0
    sub = min(tq, DIFF_SUB_ROWS)
    vec = pl.BlockSpec((1, HEAD), lambda h, i, j: (0, 0))
    return pl.pallas_call(
        functools.partial(_diff_kernel, tq, sub, lam_init),
        grid=(n_heads, s_len // tq, lk // tk),
        in_specs=[
            pl.BlockSpec((tq, V_HEAD), lambda h, i, j: (i, h)),
            pl.BlockSpec((tk, V_HEAD), lambda h, i, j: (j, n_heads + h)),
            pl.BlockSpec((tk, V_HEAD), lambda h, i, j: (j, 2 * n_heads + h)),
            vec, vec, vec, vec,
            pl.BlockSpec((1, V_HEAD), lambda h, i, j: (0, 0)),
        ],
        out_specs=pl.BlockSpec((tq, V_HEAD), lambda h, i, j: (i, h)),
        out_shape=jax.ShapeDtypeStruct((s_len, n_heads * V_HEAD), BF16),
        scratch_shapes=[
            pltpu.VMEM((2 * tq, 1), F32),
            pltpu.VMEM((2 * tq, LANES), F32),
            pltpu.VMEM((2 * tq, V_HEAD), F32),
        ],
        compiler_params=_cparams(("parallel", "parallel", "arbitrary"), ATTN_FLAGS),
        name="diff_flash",
    )(q_arr, kv_arr, kv_arr, lq1, lk1, lq2, lk2, sub_gain)


def _split3(x):
    hi = x.astype(BF16)
    r1 = x - hi.astype(F32)
    mid = r1.astype(BF16)
    lo = (r1 - mid.astype(F32)).astype(BF16)
    return hi, mid, lo


def _log_sigmoid(x):
    return jnp.minimum(x, 0.0) - jnp.log(1.0 + jnp.exp(-jnp.abs(x)))


def _mlstm_kernel(nh, chunk, qf_ref, kf_ref, vf_ref, gcf_ref, grf_ref, qb_ref, kb_ref, vb_ref, gcb_ref, grb_ref,
                  hf_ref, hb_ref, c_scr, n_scr, m_scr):
    t = pl.program_id(0)

    @pl.when(t == 0)
    def _():
        c_scr[...] = jnp.zeros(c_scr.shape, F32)
        n_scr[...] = jnp.zeros(n_scr.shape, F32)
        m_scr[...] = jnp.full(m_scr.shape, M_INIT, F32)

    row = lax.broadcasted_iota(jnp.int32, (chunk, chunk), 0)
    col = lax.broadcasted_iota(jnp.int32, (chunk, chunk), 1)
    lower = col <= row
    upper = col >= row
    lower_b = lower.astype(BF16)
    upper_b = upper.astype(BF16)

    for rev in (False, True):
        q_ref, k_ref, v_ref, gc_ref, gr_ref, h_ref = (
            (qb_ref, kb_ref, vb_ref, gcb_ref, grb_ref, hb_ref) if rev else
            (qf_ref, kf_ref, vf_ref, gcf_ref, grf_ref, hf_ref))
        mask = upper if rev else lower
        gcol = gc_ref[...]
        grow = gr_ref[...]
        cum_col = sum(_dot(upper_b if rev else lower_b, piece) for piece in _split3(_log_sigmoid(gcol)))
        cum_row = sum(_dot(piece, lower_b if rev else upper_b) for piece in _split3(_log_sigmoid(grow)))
        for hd in range(nh):
            idx = (1 if rev else 0) * nh + hd
            ki = (2 if rev else 0) * nh + hd
            kf = ki + nh
            qc = q_ref[:, hd * HEAD:(hd + 1) * HEAD]
            kc = k_ref[:, hd * HEAD:(hd + 1) * HEAD]
            vc = v_ref[:, hd * V_HEAD:(hd + 1) * V_HEAD]
            i_col = gcol[:, ki:ki + 1]
            i_row = grow[ki:ki + 1, :]
            b_col = cum_col[:, kf:kf + 1]
            b_row = cum_row[kf:kf + 1, :]
            c_mem = c_scr[idx]
            n_mem = n_scr[idx]
            m = m_scr[idx][:, :1]

            dlog = jnp.where(mask, b_col + (i_row - b_row), NEG_INF)
            inter = b_col + m
            m_out = jnp.maximum(inter, jnp.max(dlog, axis=1, keepdims=True))
            w_intra = jnp.exp(dlog - m_out)
            w_inter = jnp.exp(inter - m_out)
            qk = _dot_nt(qc, kc) * w_intra
            num = w_inter * _dot(qc, c_mem.astype(BF16)) + _dot(qk.astype(BF16), vc)
            den = (w_inter * jnp.sum(qc.astype(F32) * n_mem, axis=1, keepdims=True)
                   + jnp.sum(qk, axis=1, keepdims=True))
            h = num / jnp.maximum(jnp.abs(den), jnp.exp(-m_out))
            h_ref[:, hd * V_HEAD:(hd + 1) * V_HEAD] = h

            b_last = b_col[0:1, :] if rev else b_col[chunk - 1:chunk, :]
            g_col = b_last - b_col + i_col
            m_new = jnp.maximum(b_last + m, jnp.max(g_col, axis=0, keepdims=True))
            a_dec = jnp.exp(b_last + m - m_new)
            kw = kc.astype(F32) * jnp.exp(g_col - m_new)
            c_scr[idx] = a_dec * c_mem + _dot_tn(kw.astype(BF16), vc)
            n_scr[idx] = a_dec * n_mem + jnp.sum(kw, axis=0, keepdims=True)
            m_scr[idx] = jnp.broadcast_to(m_new, (1, LANES))


def _mlstm(qkv, gates_col, gates_row, nh, q_col, k_col, v_col, n_lat, n_ctx):
    t_len = qkv.shape[0]
    chunk = MLSTM_CHUNK
    assert n_lat % chunk == 0 and n_ctx % chunk == 0 and t_len == n_lat + n_ctx
    nlat, nctx = n_lat // chunk, n_ctx // chunk
    nc = nlat + nctx

    def fwd(t):
        return jnp.where(t < nctx, nlat + t, t - nctx)

    def bwd(t):
        return jnp.where(t < nctx, nc - 1 - t, nlat - 1 - (t - nctx))

    qw, vw = nh * HEAD, nh * V_HEAD

    def specs(order):
        return [
            pl.BlockSpec((chunk, qw), lambda t: (order(t), q_col)),
            pl.BlockSpec((chunk, qw), lambda t: (order(t), k_col)),
            pl.BlockSpec((chunk, vw), lambda t: (order(t), v_col)),
            pl.BlockSpec((chunk, LANES), lambda t: (order(t), 0)),
            pl.BlockSpec((4 * nh, chunk), lambda t: (0, order(t))),
        ]

    return pl.pallas_call(
        functools.partial(_mlstm_kernel, nh, chunk),
        grid=(nc,),
        in_specs=specs(fwd) + specs(bwd),
        out_specs=[pl.BlockSpec((chunk, vw), lambda t: (fwd(t), 0)),
                   pl.BlockSpec((chunk, vw), lambda t: (bwd(t), 0))],
        out_shape=[jax.ShapeDtypeStruct((t_len, vw), F32)] * 2,
        scratch_shapes=[
            pltpu.VMEM((2 * nh, HEAD, V_HEAD), F32),
            pltpu.VMEM((2 * nh, 1, HEAD), F32),
            pltpu.VMEM((2 * nh, 1, LANES), F32),
        ],
        compiler_params=_cparams(("arbitrary",)),
        name="mlstm_scan",
    )(qkv, qkv, qkv, gates_col, gates_row, qkv, qkv, qkv, gates_col, gates_row)


def _out_kernel(merge, nh, *refs):
    if merge:
        a_ref, hf_ref, hb_ref, o_ref, hg_ref, w_ref, x_ref, g_ref, y_ref, mix_scr = refs
        aw = a_ref.shape[1]

        @pl.when(pl.program_id(1) == 0)
        def _():
            mix_scr[:, :aw] = a_ref[...]
            for hd in range(nh):
                sl = slice(hd * V_HEAD, (hd + 1) * V_HEAD)
                hm = _rms(hf_ref[:, sl] + hb_ref[:, sl]) * hg_ref[:, sl] * jax.nn.sigmoid(o_ref[:, sl])
                mix_scr[:, aw + hd * V_HEAD:aw + (hd + 1) * V_HEAD] = hm.astype(BF16)

        lhs = mix_scr[...]
    else:
        a_ref, w_ref, x_ref, g_ref, y_ref = refs
        lhs = a_ref[...]
    y_ref[...] = x_ref[...] + g_ref[...] * _dot(lhs, w_ref[...])


def _out_proj(x, gate, w, a, merge=None, row_off=0, tm=512, tn=1024):
    m, d = x.shape
    kdim, n = w.shape
    tm = min(tm, m)
    assert m % tm == 0 and n % tn == 0 and row_off % tm == 0
    off = row_off // tm
    in_specs = [pl.BlockSpec((tm, a.shape[1]), lambda i, j: (i, 0))]
    args = [a]
    scratch = []
    nh = 0
    if merge is not None:
        hf, hb, o, hg, nh = merge
        hw = hf.shape[1]
        in_specs += [pl.BlockSpec((tm, hw), lambda i, j: (i + off, 0)),
                     pl.BlockSpec((tm, hw), lambda i, j: (i + off, 0)),
                     pl.BlockSpec((tm, hw), lambda i, j: (i, 0)),
                     pl.BlockSpec((1, hw), lambda i, j: (0, 0))]
        args += [hf, hb, o, hg]
        scratch = [pltpu.VMEM((tm, kdim), BF16)]
    in_specs += [pl.BlockSpec((kdim, tn), lambda i, j: (0, j)),
                 pl.BlockSpec((tm, tn), lambda i, j: (i, j)),
                 pl.BlockSpec((1, tn), lambda i, j: (0, j))]
    args += [w, x, gate]
    return pl.pallas_call(
        functools.partial(_out_kernel, merge is not None, nh),
        grid=(m // tm, n // tn),
        in_specs=in_specs,
        out_specs=pl.BlockSpec((tm, tn), lambda i, j: (i, j)),
        out_shape=jax.ShapeDtypeStruct((m, n), F32),
        scratch_shapes=scratch,
        compiler_params=_cparams(("parallel", "arbitrary")),
        name="out_proj_residual",
    )(*args)


def _ffn_kernel(x_ref, sh_ref, sc_ref, g_ref, wg_ref, wu_ref, wd_ref, y_ref, h_scr, acc_scr):
    f = pl.program_id(1)

    @pl.when(f == 0)
    def _():
        h = _rms(x_ref[...]) * (1.0 + sc_ref[...]) + sh_ref[...]
        h_scr[...] = h.astype(BF16)
        acc_scr[...] = jnp.zeros(acc_scr.shape, F32)

    h = h_scr[...]
    gt = _dot(h, wg_ref[...])
    up = _dot(h, wu_ref[...])
    act = (gt * jax.nn.sigmoid(gt)) * up
    acc_scr[...] += _dot(act.astype(BF16), wd_ref[...])

    @pl.when(f == pl.num_programs(1) - 1)
    def _():
        y_ref[...] = x_ref[...] + g_ref[...] * acc_scr[...]


def _ffn_dense(x, shift, scale, gate, wg, wu, wd, tm=512, tf=512):
    m, d = x.shape
    ff = wg.shape[1]
    tm = min(tm, m)
    assert m % tm == 0 and ff % tf == 0
    vec = pl.BlockSpec((1, d), lambda i, f: (0, 0))
    return pl.pallas_call(
        _ffn_kernel,
        grid=(m // tm, ff // tf),
        in_specs=[
            pl.BlockSpec((tm, d), lambda i, f: (i, 0)), vec, vec, vec,
            pl.BlockSpec((d, tf), lambda i, f: (0, f)),
            pl.BlockSpec((d, tf), lambda i, f: (0, f)),
            pl.BlockSpec((tf, d), lambda i, f: (f, 0)),
        ],
        out_specs=pl.BlockSpec((tm, d), lambda i, f: (i, 0)),
        out_shape=jax.ShapeDtypeStruct((m, d), F32),
        scratch_shapes=[pltpu.VMEM((tm, d), BF16), pltpu.VMEM((tm, d), F32)],
        compiler_params=_cparams(("parallel", "arbitrary")),
        name="ffn_dense",
    )(x, shift, scale, gate, wg, wu, wd)


R_E1, R_E2, R_W1, R_W2, R_RANK1, R_RANK2 = range(6)


def _lane_pick(a, lane, k):
    return jnp.sum(jnp.where(lane == k, a, 0.0), axis=1, keepdims=True)


def _router_kernel(n_exp, x_ref, sh_ref, sc_ref, r_ref, info_ref, cnt_ref, carry_scr):
    @pl.when(pl.program_id(0) == 0)
    def _():
        carry_scr[...] = jnp.zeros(carry_scr.shape, F32)

    tm = x_ref.shape[0]
    h = _rms(x_ref[...]) * (1.0 + sc_ref[...]) + sh_ref[...]
    logits = jnp.dot(h, r_ref[...], preferred_element_type=F32, precision=lax.Precision.HIGHEST)
    lane = lax.broadcasted_iota(jnp.int32, logits.shape, 1)
    logits = jnp.where(lane < n_exp, logits, NEG_INF)
    v1 = jnp.max(logits, axis=1, keepdims=True)
    i1 = jnp.min(jnp.where(logits == v1, lane, LANES), axis=1, keepdims=True)
    rest = jnp.where(lane == i1, NEG_INF, logits)
    v2 = jnp.max(rest, axis=1, keepdims=True)
    i2 = jnp.min(jnp.where(rest == v2, lane, LANES), axis=1, keepdims=True)
    e2 = jnp.exp(v2 - v1)
    w1 = 1.0 / (1.0 + e2)
    w2 = e2 / (1.0 + e2)

    chosen = jnp.where(jnp.logical_or(lane == i1, lane == i2), 1.0, 0.0)
    r_i = lax.broadcasted_iota(jnp.int32, (tm, tm), 0)
    c_i = lax.broadcasted_iota(jnp.int32, (tm, tm), 1)
    before = (c_i < r_i).astype(BF16)
    rank = _dot(before, chosen.astype(BF16)) + carry_scr[...]
    carry_scr[...] += jnp.sum(chosen, axis=0, keepdims=True)
    cnt_ref[...] = carry_scr[...]
    rank1 = jnp.sum(jnp.where(lane == i1, rank, 0.0), axis=1, keepdims=True)
    rank2 = jnp.sum(jnp.where(lane == i2, rank, 0.0), axis=1, keepdims=True)
    info = jnp.zeros(logits.shape, F32)
    for k, val in ((R_E1, i1.astype(F32)), (R_E2, i2.astype(F32)), (R_W1, w1), (R_W2, w2),
                   (R_RANK1, rank1), (R_RANK2, rank2)):
        info = jnp.where(lane == k, val, info)
    info_ref[...] = info


def _router(x, shift, scale, router, tm=512):
    m, d = x.shape
    n_exp = router.shape[1]
    vec = pl.BlockSpec((1, d), lambda i: (0, 0))
    router_pad = jnp.zeros((d, LANES), F32).at[:, :n_exp].set(router)
    return pl.pallas_call(
        functools.partial(_router_kernel, n_exp),
        grid=(m // tm,),
        in_specs=[pl.BlockSpec((tm, d), lambda i: (i, 0)), vec, vec, pl.BlockSpec((d, LANES), lambda i: (0, 0))],
        out_specs=[pl.BlockSpec((tm, LANES), lambda i: (i, 0)), pl.BlockSpec((1, LANES), lambda i: (0, 0))],
        out_shape=[jax.ShapeDtypeStruct((m, LANES), F32), jax.ShapeDtypeStruct((1, LANES), F32)],
        scratch_shapes=[pltpu.VMEM((1, LANES), F32)],
        compiler_params=_cparams(("arbitrary",)),
        name="moe_router",
    )(x, shift, scale, router_pad)


def _gather_rows(src_hbm, idx_ref, buf, sem, start):
    n_rows = buf.shape[0]
    if not start:
        pltpu.make_async_copy(src_hbm.at[pl.ds(0, n_rows), :], buf, sem).wait()
        return

    def body(r, carry):
        pltpu.make_async_copy(src_hbm.at[pl.ds(idx_ref[0, r], 1), :], buf.at[pl.ds(r, 1), :], sem).start()
        return carry
    lax.fori_loop(0, n_rows, body, 0, unroll=GATHER_UNROLL)


def _moe_ffn_kernel(te_ref, nu_ref, src_ref, nsrc_ref, x_hbm, sh_ref, sc_ref, wg_ref, wu_ref, wd_ref, y_ref,
                    xbuf, hbuf, sem):
    i = pl.program_id(0)
    f = pl.program_id(1)
    n_used = nu_ref[0]
    active = i < n_used

    @pl.when(f == 0)
    def _():
        y_ref[...] = jnp.zeros(y_ref.shape, F32)

    @pl.when(jnp.logical_and(active, f == 0))
    def _():
        pl.when(i == 0)(lambda: _gather_rows(x_hbm, src_ref, xbuf, sem, True))
        _gather_rows(x_hbm, src_ref, xbuf, sem, False)
        h = _rms(xbuf[...]) * (1.0 + sc_ref[...]) + sh_ref[...]
        hbuf[...] = h.astype(BF16)
        pl.when(i + 1 < n_used)(lambda: _gather_rows(x_hbm, nsrc_ref, xbuf, sem, True))

    @pl.when(active)
    def _():
        h = hbuf[...]
        gt = _dot(h, wg_ref[...])
        up = _dot(h, wu_ref[...])
        act = (gt * jax.nn.sigmoid(gt)) * up
        y_ref[...] += _dot(act.astype(BF16), wd_ref[...])


def _moe_ffn(x, shift, scale, tile_exp, n_used, src, wg, wu, wd, tm, tf=1024):
    m, d = x.shape
    n_exp, _, ff = wg.shape
    n_tiles = src.shape[0]
    tf = min(tf, ff)
    assert ff % tf == 0
    nf = ff // tf
    vec = pl.BlockSpec((1, d), lambda i, f, te, nu: (0, 0))
    idx = pl.BlockSpec((None, 1, tm), lambda i, f, te, nu: (i, 0, 0), memory_space=pltpu.SMEM)
    idx_next = pl.BlockSpec((None, 1, tm), lambda i, f, te, nu: (jnp.minimum(i + 1, n_tiles - 1), 0, 0),
                            memory_space=pltpu.SMEM)

    def f_eff(i, f, nu):
        return jnp.where(i < nu[0], f, nf - 1)

    grid_spec = pltpu.PrefetchScalarGridSpec(
        num_scalar_prefetch=2,
        grid=(n_tiles, nf),
        in_specs=[
            idx, idx_next,
            pl.BlockSpec(memory_space=pl.ANY),
            vec, vec,
            pl.BlockSpec((None, d, tf), lambda i, f, te, nu: (te[i], 0, f_eff(i, f, nu))),
            pl.BlockSpec((None, d, tf), lambda i, f, te, nu: (te[i], 0, f_eff(i, f, nu))),
            pl.BlockSpec((None, tf, d), lambda i, f, te, nu: (te[i], f_eff(i, f, nu), 0)),
        ],
        out_specs=pl.BlockSpec((tm, d), lambda i, f, te, nu: (i, 0)),
        scratch_shapes=[pltpu.VMEM((tm, d), F32), pltpu.VMEM((tm, d), BF16), pltpu.SemaphoreType.DMA(())],
    )
    return pl.pallas_call(
        _moe_ffn_kernel,
        grid_spec=grid_spec,
        out_shape=jax.ShapeDtypeStruct((n_tiles * tm, d), F32),
        compiler_params=_cparams(("arbitrary", "arbitrary")),
        name="moe_grouped_ffn",
    )(tile_exp, n_used, src, src, x, shift, scale, wg, wu, wd)


def _moe_combine_kernel(pos_ref, npos_ref, x_ref, g_ref, info_ref, y_hbm, o_ref, ybuf, sem):
    i = pl.program_id(0)
    tm = x_ref.shape[0]
    slot = i % 2
    pl.when(i == 0)(lambda: _gather_rows(y_hbm, pos_ref, ybuf.at[0], sem.at[0], True))
    for s in range(2):
        @pl.when(slot == s)
        def _(s=s):
            _gather_rows(y_hbm, pos_ref, ybuf.at[s], sem.at[s], False)
            pl.when(i + 1 < pl.num_programs(0))(
                lambda: _gather_rows(y_hbm, npos_ref, ybuf.at[1 - s], sem.at[1 - s], True))
            info = info_ref[...]
            lane = lax.broadcasted_iota(jnp.int32, info.shape, 1)
            w1 = _lane_pick(info, lane, R_W1)
            w2 = _lane_pick(info, lane, R_W2)
            o_ref[...] = x_ref[...] + g_ref[...] * (w1 * ybuf[s, :tm, :] + w2 * ybuf[s, tm:, :])


def _moe_combine(x, gate, info, y, pos, tm=256):
    m, d = x.shape
    nb = m // tm
    assert m % tm == 0 and pos.shape == (nb, 1, 2 * tm)
    idx = pl.BlockSpec((None, 1, 2 * tm), lambda i: (i, 0, 0), memory_space=pltpu.SMEM)
    idx_next = pl.BlockSpec((None, 1, 2 * tm), lambda i: (jnp.minimum(i + 1, nb - 1), 0, 0), memory_space=pltpu.SMEM)
    return pl.pallas_call(
        _moe_combine_kernel,
        grid=(nb,),
        in_specs=[
            idx, idx_next,
            pl.BlockSpec((tm, d), lambda i: (i, 0)),
            pl.BlockSpec((1, d), lambda i: (0, 0)),
            pl.BlockSpec((tm, LANES), lambda i: (i, 0)),
            pl.BlockSpec(memory_space=pl.ANY),
        ],
        out_specs=pl.BlockSpec((tm, d), lambda i: (i, 0)),
        out_shape=jax.ShapeDtypeStruct((m, d), F32),
        scratch_shapes=[pltpu.VMEM((2, 2 * tm, d), F32), pltpu.SemaphoreType.DMA((2,))],
        compiler_params=_cparams(("arbitrary",)),
        name="moe_combine",
    )(pos, pos, x, gate, info, y)


def _moe_top2(x, shift, scale, gate, router, wg, wu, wd, tm=512):
    m, d = x.shape
    n_exp = router.shape[1]
    info, counts = _router(x, shift, scale, router)
    e1, e2 = info[:, R_E1].astype(jnp.int32), info[:, R_E2].astype(jnp.int32)
    r1, r2 = info[:, R_RANK1].astype(jnp.int32), info[:, R_RANK2].astype(jnp.int32)
    cnt = counts[0, :n_exp].astype(jnp.int32)
    padded = (cnt + tm - 1) // tm * tm
    ends = jnp.cumsum(padded)
    offs = ends - padded
    pos1, pos2 = offs[e1] + r1, offs[e2] + r2
    n_tiles = TOP_K * m // tm + n_exp
    n_rows = n_tiles * tm
    tok = jnp.arange(m, dtype=jnp.int32)
    src = jnp.zeros((n_rows,), jnp.int32).at[jnp.concatenate([pos1, pos2])].set(
        jnp.concatenate([tok, tok]), unique_indices=True)
    n_used = ends[-1] // tm
    tile_start = jnp.arange(n_tiles, dtype=jnp.int32) * tm
    tile_exp = jnp.searchsorted(ends, jnp.minimum(tile_start, ends[-1] - 1), side="right").astype(jnp.int32)
    tile_exp = jnp.minimum(tile_exp, n_exp - 1)
    y = _moe_ffn(x, shift, scale, tile_exp, n_used.reshape(1).astype(jnp.int32), src.reshape(n_tiles, 1, tm),
                 wg, wu, wd, tm)
    tc = min(COMBINE_ROWS, m)
    pos = jnp.concatenate([pos1.reshape(m // tc, 1, tc), pos2.reshape(m // tc, 1, tc)], axis=2)
    return _moe_combine(x, gate, info, y, pos, tc)


def _rope_tables(rows):
    n_freq = HEAD // 4
    inv = ROPE_THETA ** (-jnp.arange(n_freq, dtype=F32) / n_freq)
    ang_r = jnp.arange(rows, dtype=F32)[:, None] * inv
    ang_c = jnp.arange(GRID_W, dtype=F32)[:, None] * inv

    def full(fn):
        t = jnp.concatenate([jnp.broadcast_to(fn(ang_r)[:, None, :], (rows, GRID_W, n_freq)),
                             jnp.broadcast_to(fn(ang_c)[None, :, :], (rows, GRID_W, n_freq))], axis=-1)
        return t.reshape(rows * GRID_W, 2 * n_freq)

    cos, sin = full(jnp.cos), full(jnp.sin)
    return jnp.concatenate([cos, cos], axis=-1), jnp.concatenate([-sin, sin], axis=-1)


def _halves(a):
    shp = a.shape
    a = a.reshape(shp[:-1] + (shp[-1] // HEAD, HEAD // 2, 2))
    return jnp.swapaxes(a, -1, -2).reshape(shp)


def kernel(x, c, ctx, c_ctx, ada_w, ada_b, ev_w_in, ev_gate_b, ev_q_gain, ev_k_gain, ev_h_gain, ev_w_out,
           ev_ff_gate, ev_ff_up, ev_ff_down, od_w_in, od_q_gain, od_k_gain, od_lam_q1, od_lam_k1, od_lam_q2,
           od_lam_k2, od_sub_gain, od_w_out, od_router, od_ex_gate, od_ex_up, od_ex_down):
    bsz, s_len, d = x.shape
    n_ctx = ctx.shape[1]
    assert bsz == 1 and ada_w.shape[0] == 2
    a_heads = d // (2 * HEAD)
    a_kv = a_heads // GQA_GROUP
    b_heads = d // (2 * V_HEAD)
    c_heads = d // V_HEAD
    a_q, a_kvw, b_qk, b_v = a_heads * HEAD, a_kv * HEAD, b_heads * HEAD, b_heads * V_HEAD
    score_scale = HEAD ** -0.5 * math.log2(math.e)

    x_lat, x_ctx = x[0], ctx[0]
    cvecs = jnp.zeros((8, d), F32).at[0].set(c[0]).at[1].set(c_ctx)
    mods = _modulation(cvecs, ada_w, ada_b)

    def mod(layer, who):
        return [mods[layer, who, k * d:(k + 1) * d][None, :] for k in range(6)]

    cos_l, sin_l = _rope_tables(s_len // GRID_W)
    cos_c, sin_c = jnp.ones((n_ctx, HEAD), F32), jnp.zeros((n_ctx, HEAD), F32)

    w_in = ev_w_in[0]
    o0 = 0
    wqa = w_in[:, o0:o0 + a_q]; o0 += a_q
    wka = w_in[:, o0:o0 + a_kvw]; o0 += a_kvw
    wva = w_in[:, o0:o0 + a_kvw]; o0 += a_kvw
    wqb = w_in[:, o0:o0 + b_qk]; o0 += b_qk
    wkb = w_in[:, o0:o0 + b_qk]; o0 += b_qk
    wvb = w_in[:, o0:o0 + b_v]; o0 += b_v
    wob = w_in[:, o0:o0 + b_v]; o0 += b_v
    wgt = w_in[:, o0:]
    n_gates = wgt.shape[1]
    w0 = jnp.concatenate([_halves(wqa), wvb, wqb, wkb, _halves(wka), wva, wob], axis=1).astype(BF16)
    ones = lambda n: jnp.ones((n,), F32)
    cs0 = jnp.concatenate([
        jnp.tile(_halves(ev_q_gain[0]), a_heads) * score_scale, ones(b_v), ones(b_qk) * (HEAD ** -0.5), ones(b_qk),
        jnp.tile(_halves(ev_k_gain[0]), a_kv), ones(a_kvw), ones(b_v)])[None, :]
    modes0 = (["R"] * a_heads + ["P"] * (b_v // HEAD) + ["P"] * (2 * b_qk // HEAD) + ["R"] * a_kv
              + ["P"] * a_kv + ["P"] * (b_v // HEAD))
    nbf0 = a_q + b_v + 2 * b_qk + 2 * a_kvw
    wg0 = jnp.zeros((d, LANES), F32).at[:, :n_gates].set(wgt).astype(BF16)
    gb0 = jnp.zeros((1, LANES), F32).at[0, :n_gates].set(ev_gate_b[0])
    col_vb, col_qb, col_kb = a_q // b_v, (a_q + b_v) // b_qk, (a_q + b_v + b_qk) // b_qk
    col_ka = (a_q + b_v + 2 * b_qk) // HEAD
    col_va = col_ka + a_kv

    ml, mc = mod(0, 0), mod(0, 1)
    pl_b, pl_o, pl_g = _proj(x_lat, ml[0], ml[1], w0, cs0, cos_l, sin_l, modes0, nbf0, wg0, gb0)
    pc_b, pc_o, pc_g = _proj(x_ctx, mc[0], mc[1], w0, cs0, cos_c, sin_c, modes0, nbf0, wg0, gb0)
    p_all = jnp.concatenate([pl_b, pc_b], axis=0)
    g_all = jnp.concatenate([pl_g, pc_g], axis=0)
    g_row = g_all[:, :4 * b_heads].T

    att_l = _gqa_attention(pl_b, p_all, a_kv, col_ka, col_va, tq=512, tk=3328)
    att_c = _gqa_attention(pc_b, pc_b, a_kv, col_ka, col_va, tq=256, tk=256)
    hf, hb = _mlstm(p_all, g_all, g_row, b_heads, col_qb, col_kb, col_vb, s_len, n_ctx)

    hg = ev_h_gain[0].reshape(1, b_v)
    w_out0 = ev_w_out[0].astype(BF16)
    x_lat = _out_proj(x_lat, ml[2], w_out0, att_l, merge=(hf, hb, pl_o, hg, b_heads))
    x_ctx = _out_proj(x_ctx, mc[2], w_out0, att_c, merge=(hf, hb, pc_o, hg, b_heads), row_off=s_len)
    ffw = (ev_ff_gate[0].astype(BF16), ev_ff_up[0].astype(BF16), ev_ff_down[0].astype(BF16))
    x_lat = _ffn_dense(x_lat, ml[3], ml[4], ml[5], *ffw)
    x_ctx = _ffn_dense(x_ctx, mc[3], mc[4], mc[5], *ffw)

    c_qk = c_heads * 2 * HEAD
    w1 = od_w_in[0]
    w1 = jnp.concatenate([_halves(w1[:, :2 * c_qk]), w1[:, 2 * c_qk:]], axis=1).astype(BF16)
    cs1 = jnp.concatenate([jnp.tile(_halves(od_q_gain[0]), 2 * c_heads) * score_scale,
                           jnp.tile(_halves(od_k_gain[0]), 2 * c_heads), ones(c_heads * V_HEAD)])[None, :]
    modes1 = ["R"] * (4 * c_heads) + ["P"] * (c_heads * V_HEAD // HEAD)
    ml, mc = mod(1, 0), mod(1, 1)
    (ql,) = _proj(x_lat, ml[0], ml[1], w1, cs1, cos_l, sin_l, modes1, w1.shape[1])
    (qc,) = _proj(x_ctx, mc[0], mc[1], w1, cs1, cos_c, sin_c, modes1, w1.shape[1])
    kv_all = jnp.concatenate([ql, qc], axis=0)
    lam_init = 0.8 - 0.6 * math.exp(-0.3 * 1)
    vec = lambda a: a[0].reshape(1, -1)
    o_l = _diff_attention(ql, kv_all, c_heads, lam_init, vec(od_lam_q1), vec(od_lam_k1), vec(od_lam_q2),
                          vec(od_lam_k2), vec(od_sub_gain), tq=2048, tk=3328)
    x_lat = _out_proj(x_lat, ml[2], od_w_out[0].astype(BF16), o_l)
    x_lat = _moe_top2(x_lat, ml[3], ml[4], ml[5], od_router[0], od_ex_gate[0].astype(BF16),
                      od_ex_up[0].astype(BF16), od_ex_down[0].astype(BF16))
    return x_lat[None]
```

```python
import functools
import math

import jax
import jax.numpy as jnp
from jax import lax
from jax.experimental import pallas as pl
from jax.experimental.pallas import tpu as pltpu

F32 = jnp.float32
BF16 = jnp.bfloat16

GRID_W = 64
ROPE_THETA = 10000.0
EPS = 1e-6
M_INIT = -1e30
HEAD = 128
V_HEAD = 256
GQA_GROUP = 4
TOP_K = 2
MLSTM_CHUNK = 256
GQA_SUB_ROWS = 256
DIFF_SUB_ROWS = 128
GATHER_UNROLL = 8
COMBINE_ROWS = 256
PIPE_SKEW = 1
LANES = 128
VMEM_LIMIT = 60 * 1024 * 1024
NEG_INF = float("-inf")


def _cparams(sem, flags=None):
    return pltpu.CompilerParams(dimension_semantics=sem, vmem_limit_bytes=VMEM_LIMIT, flags=flags)


ATTN_FLAGS = None


def _dot(a, b):
    return jnp.dot(a, b, preferred_element_type=F32)


def _dot_nt(a, b):
    return lax.dot_general(a, b, (((1,), (1,)), ((), ())), preferred_element_type=F32)


def _dot_tn(a, b):
    return lax.dot_general(a, b, (((0,), (0,)), ((), ())), preferred_element_type=F32)


def _rms(x):
    return x * lax.rsqrt(jnp.mean(x * x, axis=-1, keepdims=True) + EPS)


def _mod_kernel(c_ref, w_ref, b_ref, o_ref):
    c = c_ref[...]
    s = c * jax.nn.sigmoid(c)
    o_ref[...] = _dot(s.astype(BF16), w_ref[...].astype(BF16)) + b_ref[...]


def _modulation(cvecs, ada_w, ada_b):
    depth, d, n = ada_w.shape
    tn = 1024
    return pl.pallas_call(
        _mod_kernel,
        grid=(depth, n // tn),
        in_specs=[
            pl.BlockSpec((8, d), lambda l, j: (0, 0)),
            pl.BlockSpec((None, d, tn), lambda l, j: (l, 0, j)),
            pl.BlockSpec((None, 1, tn), lambda l, j: (l, 0, j)),
        ],
        out_specs=pl.BlockSpec((None, 8, tn), lambda l, j: (l, 0, j)),
        out_shape=jax.ShapeDtypeStruct((depth, 8, n), F32),
        compiler_params=_cparams(("parallel", "parallel")),
        name="adaln_mod",
    )(cvecs, ada_w, ada_b.reshape(depth, 1, n))


def _proj_kernel(patterns, n_bf_tiles, has_f32, has_gates, tn, *refs):
    it = iter(refs)
    x_ref, sh_ref, sc_ref, w_ref, cs_ref, cos_ref, sin_ref = (next(it) for _ in range(7))
    if has_gates:
        wg_ref, gb_ref = next(it), next(it)
    ob_ref = next(it)
    of_ref = next(it) if has_f32 else None
    og_ref = next(it) if has_gates else None
    h_scr = next(it)

    j = pl.program_id(1)

    @pl.when(j == 0)
    def _():
        h = _rms(x_ref[...]) * (1.0 + sc_ref[...]) + sh_ref[...]
        h_scr[...] = h.astype(BF16)
        if has_gates:
            og_ref[...] = _dot(h_scr[...], wg_ref[...]) + gb_ref[...]

    acc = _dot(h_scr[...], w_ref[...])
    cs = cs_ref[...]

    for lo, hi, pat in patterns:
        @pl.when(jnp.logical_and(j >= lo, j < hi))
        def _(pat=pat, lo=lo):
            to_f32 = has_f32 and lo >= n_bf_tiles
            out_ref = of_ref if to_f32 else ob_ref
            for hidx, mode in enumerate(pat):
                sl = slice(hidx * HEAD, (hidx + 1) * HEAD)
                y = acc[:, sl]
                if mode == "R":
                    y = _rms(y) * cs[:, sl]
                    y = y * cos_ref[...] + pltpu.roll(y, HEAD // 2, 1) * sin_ref[...]
                else:
                    y = y * cs[:, sl]
                out_ref[:, sl] = y.astype(out_ref.dtype)


def _proj(x, shift, scale, w, colscale, cos2, sin2, head_modes, n_bf_cols, wg=None, gb=None, tm=1024, tn=512):
    m, d = x.shape
    n = w.shape[1]
    tm = min(tm, m)
    assert m % tm == 0 and n % tn == 0 and n_bf_cols % tn == 0
    nj = n // tn
    n_bf_tiles = n_bf_cols // tn
    has_f32 = n_bf_cols < n
    has_gates = wg is not None
    hpt = tn // HEAD
    tiles = [tuple(head_modes[t * hpt:(t + 1) * hpt]) for t in range(nj)]
    patterns = []
    for t, pat in enumerate(tiles):
        if patterns and patterns[-1][2] == pat and not (has_f32 and t == n_bf_tiles):
            patterns[-1] = (patterns[-1][0], t + 1, pat)
        else:
            patterns.append((t, t + 1, pat))

    in_specs = [
        pl.BlockSpec((tm, d), lambda i, j: (i, 0)),
        pl.BlockSpec((1, d), lambda i, j: (0, 0)),
        pl.BlockSpec((1, d), lambda i, j: (0, 0)),
        pl.BlockSpec((d, tn), lambda i, j: (0, j)),
        pl.BlockSpec((1, tn), lambda i, j: (0, j)),
        pl.BlockSpec((tm, HEAD), lambda i, j: (i, 0)),
        pl.BlockSpec((tm, HEAD), lambda i, j: (i, 0)),
    ]
    args = [x, shift, scale, w, colscale, cos2, sin2]
    if has_gates:
        in_specs += [pl.BlockSpec((d, LANES), lambda i, j: (0, 0)), pl.BlockSpec((1, LANES), lambda i, j: (0, 0))]
        args += [wg, gb]
    last_bf = n_bf_tiles - 1
    out_specs = [pl.BlockSpec((tm, tn), lambda i, j: (i, jnp.minimum(j, last_bf)))]
    out_shape = [jax.ShapeDtypeStruct((m, n_bf_cols), BF16)]
    if has_f32:
        out_specs.append(pl.BlockSpec((tm, tn), lambda i, j: (i, jnp.maximum(j - n_bf_tiles, 0))))
        out_shape.append(jax.ShapeDtypeStruct((m, n - n_bf_cols), F32))
    if has_gates:
        out_specs.append(pl.BlockSpec((tm, LANES), lambda i, j: (i, 0)))
        out_shape.append(jax.ShapeDtypeStruct((m, LANES), F32))
    return pl.pallas_call(
        functools.partial(_proj_kernel, tuple(patterns), n_bf_tiles, has_f32, has_gates, tn),
        grid=(m // tm, nj),
        in_specs=in_specs,
        out_specs=out_specs,
        out_shape=out_shape,
        scratch_shapes=[pltpu.VMEM((tm, d), BF16)],
        compiler_params=_cparams(("parallel", "arbitrary")),
        name="norm_mod_proj",
    )(*args)


def _softmax_pipeline(score_fns, v, row_slices, m_scr, l_scr, acc_scr):
    n = len(score_fns)
    scores, probs = {}, {}
    k = PIPE_SKEW
    for t in range(n + 2 * k):
        if t < n:
            scores[t] = score_fns[t]()
        c = t - k
        if 0 <= c < n:
            probs[c] = _softmax_probs(row_slices[c], m_scr, l_scr, scores.pop(c))
        c = t - 2 * k
        if 0 <= c < n:
            rows = row_slices[c]
            alpha, p = probs.pop(c)
            acc_scr[rows, :] = alpha * acc_scr[rows, :] + _dot(p, v)


def _softmax_probs(rows, m_scr, l_scr, s):
    sub, tk = s.shape
    m_prev = m_scr[rows, :]
    m_new = jnp.maximum(m_prev, jnp.max(s, axis=1, keepdims=True))
    alpha = jnp.exp2(m_prev - m_new)
    m_scr[rows, :] = m_new
    if l_scr is None:
        return alpha, jnp.exp2((s - m_new).astype(BF16))
    p = jnp.exp2(s - m_new)
    if l_scr is not None:
        lsum = p[:, 0:LANES]
        for c in range(1, tk // LANES):
            lsum = lsum + p[:, c * LANES:(c + 1) * LANES]
        l_scr[rows, :] = alpha * l_scr[rows, :] + lsum
    return alpha, p.astype(BF16)


def _gqa_kernel(tq, sub, q_ref, k_ref, v_ref, o_ref, m_scr, acc_scr):
    kj = pl.program_id(2)

    @pl.when(kj == 0)
    def _():
        m_scr[...] = jnp.full(m_scr.shape, NEG_INF, F32)
        acc_scr[...] = jnp.zeros(acc_scr.shape, F32)

    k = k_ref[...]
    v = jnp.concatenate([v_ref[...], jnp.ones(v_ref.shape, BF16)], axis=1)
    nb = tq // sub
    fns, rows = [], []
    for g in range(GQA_GROUP):
        for b in range(nb):
            fns.append(lambda g=g, b=b: _dot_nt(q_ref[b * sub:(b + 1) * sub, g * HEAD:(g + 1) * HEAD], k))
            rows.append(slice(g * tq + b * sub, g * tq + (b + 1) * sub))
    _softmax_pipeline(fns, v, rows, m_scr, None, acc_scr)

    @pl.when(kj == pl.num_programs(2) - 1)
    def _():
        out = acc_scr[:, :HEAD] / acc_scr[:, HEAD:HEAD + 1]
        for g in range(GQA_GROUP):
            o_ref[:, g * HEAD:(g + 1) * HEAD] = out[g * tq:(g + 1) * tq, :].astype(o_ref.dtype)


def _gqa_attention(q_arr, kv_arr, n_kv, k_col, v_col, tq, tk):
    s_len = q_arr.shape[0]
    lk = kv_arr.shape[0]
    tq = min(tq, s_len)
    tk = min(tk, lk)
    assert s_len % tq == 0 and lk % tk == 0 and tk % LANES == 0
    gw = GQA_GROUP * HEAD
    sub = min(tq, GQA_SUB_ROWS)
    return pl.pallas_call(
        functools.partial(_gqa_kernel, tq, sub),
        grid=(n_kv, s_len // tq, lk // tk),
        in_specs=[
            pl.BlockSpec((tq, gw), lambda n, i, j: (i, n)),
            pl.BlockSpec((tk, HEAD), lambda n, i, j: (j, k_col + n)),
            pl.BlockSpec((tk, HEAD), lambda n, i, j: (j, v_col + n)),
        ],
        out_specs=pl.BlockSpec((tq, gw), lambda n, i, j: (i, n)),
        out_shape=jax.ShapeDtypeStruct((s_len, n_kv * gw), BF16),
        scratch_shapes=[
            pltpu.VMEM((GQA_GROUP * tq, 1), F32),
            pltpu.VMEM((GQA_GROUP * tq, 2 * HEAD), F32),
        ],
        compiler_params=_cparams(("parallel", "parallel", "arbitrary"), ATTN_FLAGS),
        name="gqa_flash",
    )(q_arr, kv_arr, kv_arr)


def _diff_kernel(tq, sub, lam_init, q_ref, k_ref, v_ref, lq1_ref, lk1_ref, lq2_ref, lk2_ref, sg_ref, o_ref,
                 m_scr, l_scr, acc_scr):
    kj = pl.program_id(2)

    @pl.when(kj == 0)
    def _():
        m_scr[...] = jnp.full(m_scr.shape, NEG_INF, F32)
        l_scr[...] = jnp.zeros(l_scr.shape, F32)
        acc_scr[...] = jnp.zeros(acc_scr.shape, F32)

    v = v_ref[...]
    nb = tq // sub
    fns, rows = [], []
    for b in range(nb):
        for r in range(2):
            fns.append(lambda r=r, b=b: _dot_nt(q_ref[b * sub:(b + 1) * sub, r * HEAD:(r + 1) * HEAD],
                                                k_ref[:, r * HEAD:(r + 1) * HEAD]))
            rows.append(slice(r * tq + b * sub, r * tq + (b + 1) * sub))
    _softmax_pipeline(fns, v, rows, m_scr, l_scr, acc_scr)

    @pl.when(kj == pl.num_programs(2) - 1)
    def _():
        lam = (jnp.exp(jnp.sum(lq1_ref[...] * lk1_ref[...], axis=1, keepdims=True))
               - jnp.exp(jnp.sum(lq2_ref[...] * lk2_ref[...], axis=1, keepdims=True)) + lam_init)
        o = acc_scr[...] / jnp.sum(l_scr[...], axis=1, keepdims=True)
        o = o[:tq, :] - lam * o[tq:, :]
        o = _rms(o) * sg_ref[...] * (1.0 - lam_init)
        o_ref[...] = o.astype(o_ref.dtype)


def _diff_attention(q_arr, kv_arr, n_heads, lam_init, lq1, lk1, lq2, lk2, sub_gain, tq, tk):
    s_len = q_arr.shape[0]
    lk = kv_arr.shape[0]
    tq = min(tq, s_len)
    tk = min(tk, lk)
    assert s_len % tq == 0 and lk % tk == 0 and tk % LANES == 0
    sub = min(tq, DIFF_SUB_ROWS)
    vec = pl.BlockSpec((1, HEAD), lambda h, i, j: (0, 0))
    return pl.pallas_call(
        functools.partial(_diff_kernel, tq, sub, lam_init),
        grid=(n_heads, s_len // tq, lk // tk),
        in_specs=[
            pl.BlockSpec((tq, V_HEAD), lambda h, i, j: (i, h)),
            pl.BlockSpec((tk, V_HEAD), lambda h, i, j: (j, n_heads + h)),
            pl.BlockSpec((tk, V_HEAD), lambda h, i, j: (j, 2 * n_heads + h)),
            vec, vec, vec, vec,
            pl.BlockSpec((1, V_HEAD), lambda h, i, j: (0, 0)),
        ],
        out_specs=pl.BlockSpec((tq, V_HEAD), lambda h, i, j: (i, h)),
        out_shape=jax.ShapeDtypeStruct((s_len, n_heads * V_HEAD), BF16),
        scratch_shapes=[
            pltpu.VMEM((2 * tq, 1), F32),
            pltpu.VMEM((2 * tq, LANES), F32),
            pltpu.VMEM((2 * tq, V_HEAD), F32),
        ],
        compiler_params=_cparams(("parallel", "parallel", "arbitrary"), ATTN_FLAGS),
        name="diff_flash",
    )(q_arr, kv_arr, kv_arr, lq1, lk1, lq2, lk2, sub_gain)


def _split3(x):
    hi = x.astype(BF16)
    r1 = x - hi.astype(F32)
    mid = r1.astype(BF16)
    lo = (r1 - mid.astype(F32)).astype(BF16)
    return hi, mid, lo


def _log_sigmoid(x):
    return jnp.minimum(x, 0.0) - jnp.log(1.0 + jnp.exp(-jnp.abs(x)))


def _mlstm_kernel(nh, chunk, qf_ref, kf_ref, vf_ref, gcf_ref, grf_ref, qb_ref, kb_ref, vb_ref, gcb_ref, grb_ref,
                  hf_ref, hb_ref, c_scr, n_scr, m_scr):
    t = pl.program_id(0)

    @pl.when(t == 0)
    def _():
        c_scr[...] = jnp.zeros(c_scr.shape, F32)
        n_scr[...] = jnp.zeros(n_scr.shape, F32)
        m_scr[...] = jnp.full(m_scr.shape, M_INIT, F32)

    row = lax.broadcasted_iota(jnp.int32, (chunk, chunk), 0)
    col = lax.broadcasted_iota(jnp.int32, (chunk, chunk), 1)
    lower = col <= row
    upper = col >= row
    lower_b = lower.astype(BF16)
    upper_b = upper.astype(BF16)

    for rev in (False, True):
        q_ref, k_ref, v_ref, gc_ref, gr_ref, h_ref = (
            (qb_ref, kb_ref, vb_ref, gcb_ref, grb_ref, hb_ref) if rev else
            (qf_ref, kf_ref, vf_ref, gcf_ref, grf_ref, hf_ref))
        mask = upper if rev else lower
        gcol = gc_ref[...]
        grow = gr_ref[...]
        cum_col = sum(_dot(upper_b if rev else lower_b, piece) for piece in _split3(_log_sigmoid(gcol)))
        cum_row = sum(_dot(piece, lower_b if rev else upper_b) for piece in _split3(_log_sigmoid(grow)))
        for hd in range(nh):
            idx = (1 if rev else 0) * nh + hd
            ki = (2 if rev else 0) * nh + hd
            kf = ki + nh
            qc = q_ref[:, hd * HEAD:(hd + 1) * HEAD]
            kc = k_ref[:, hd * HEAD:(hd + 1) * HEAD]
            vc = v_ref[:, hd * V_HEAD:(hd + 1) * V_HEAD]
            i_col = gcol[:, ki:ki + 1]
            i_row = grow[ki:ki + 1, :]
            b_col = cum_col[:, kf:kf + 1]
            b_row = cum_row[kf:kf + 1, :]
            c_mem = c_scr[idx]
            n_mem = n_scr[idx]
            m = m_scr[idx][:, :1]

            dlog = jnp.where(mask, b_col + (i_row - b_row), NEG_INF)
            inter = b_col + m
            m_out = jnp.maximum(inter, jnp.max(dlog, axis=1, keepdims=True))
            w_intra = jnp.exp(dlog - m_out)
            w_inter = jnp.exp(inter - m_out)
            qk = _dot_nt(qc, kc) * w_intra
            num = w_inter * _dot(qc, c_mem.astype(BF16)) + _dot(qk.astype(BF16), vc)
            den = (w_inter * jnp.sum(qc.astype(F32) * n_mem, axis=1, keepdims=True)
                   + jnp.sum(qk, axis=1, keepdims=True))
            h = num / jnp.maximum(jnp.abs(den), jnp.exp(-m_out))
            h_ref[:, hd * V_HEAD:(hd + 1) * V_HEAD] = h

            b_last = b_col[0:1, :] if rev else b_col[chunk - 1:chunk, :]
            g_col = b_last - b_col + i_col
            m_new = jnp.maximum(b_last + m, jnp.max(g_col, axis=0, keepdims=True))
            a_dec = jnp.exp(b_last + m - m_new)
            kw = kc.astype(F32) * jnp.exp(g_col - m_new)
            c_scr[idx] = a_dec * c_mem + _dot_tn(kw.astype(BF16), vc)
            n_scr[idx] = a_dec * n_mem + jnp.sum(kw, axis=0, keepdims=True)
            m_scr[idx] = jnp.broadcast_to(m_new, (1, LANES))


def _mlstm(qkv, gates_col, gates_row, nh, q_col, k_col, v_col, n_lat, n_ctx):
    t_len = qkv.shape[0]
    chunk = MLSTM_CHUNK
    assert n_lat % chunk == 0 and n_ctx % chunk == 0 and t_len == n_lat + n_ctx
    nlat, nctx = n_lat // chunk, n_ctx // chunk
    nc = nlat + nctx

    def fwd(t):
        return jnp.where(t < nctx, nlat + t, t - nctx)

    def bwd(t):
        return jnp.where(t < nctx, nc - 1 - t, nlat - 1 - (t - nctx))

    qw, vw = nh * HEAD, nh * V_HEAD

    def specs(order):
        return [
            pl.BlockSpec((chunk, qw), lambda t: (order(t), q_col)),
            pl.BlockSpec((chunk, qw), lambda t: (order(t), k_col)),
            pl.BlockSpec((chunk, vw), lambda t: (order(t), v_col)),
            pl.BlockSpec((chunk, LANES), lambda t: (order(t), 0)),
            pl.BlockSpec((4 * nh, chunk), lambda t: (0, order(t))),
        ]

    return pl.pallas_call(
        functools.partial(_mlstm_kernel, nh, chunk),
        grid=(nc,),
        in_specs=specs(fwd) + specs(bwd),
        out_specs=[pl.BlockSpec((chunk, vw), lambda t: (fwd(t), 0)),
                   pl.BlockSpec((chunk, vw), lambda t: (bwd(t), 0))],
        out_shape=[jax.ShapeDtypeStruct((t_len, vw), F32)] * 2,
        scratch_shapes=[
            pltpu.VMEM((2 * nh, HEAD, V_HEAD), F32),
            pltpu.VMEM((2 * nh, 1, HEAD), F32),
            pltpu.VMEM((2 * nh, 1, LANES), F32),
        ],
        compiler_params=_cparams(("arbitrary",)),
        name="mlstm_scan",
    )(qkv, qkv, qkv, gates_col, gates_row, qkv, qkv, qkv, gates_col, gates_row)


def _out_kernel(merge, nh, *refs):
    if merge:
        a_ref, hf_ref, hb_ref, o_ref, hg_ref, w_ref, x_ref, g_ref, y_ref, mix_scr = refs
        aw = a_ref.shape[1]

        @pl.when(pl.program_id(1) == 0)
        def _():
            mix_scr[:, :aw] = a_ref[...]
            for hd in range(nh):
                sl = slice(hd * V_HEAD, (hd + 1) * V_HEAD)
                hm = _rms(hf_ref[:, sl] + hb_ref[:, sl]) * hg_ref[:, sl] * jax.nn.sigmoid(o_ref[:, sl])
                mix_scr[:, aw + hd * V_HEAD:aw + (hd + 1) * V_HEAD] = hm.astype(BF16)

        lhs = mix_scr[...]
    else:
        a_ref, w_ref, x_ref, g_ref, y_ref = refs
        lhs = a_ref[...]
    y_ref[...] = x_ref[...] + g_ref[...] * _dot(lhs, w_ref[...])


def _out_proj(x, gate, w, a, merge=None, row_off=0, tm=512, tn=1024):
    m, d = x.shape
    kdim, n = w.shape
    tm = min(tm, m)
    assert m % tm == 0 and n % tn == 0 and row_off % tm == 0
    off = row_off // tm
    in_specs = [pl.BlockSpec((tm, a.shape[1]), lambda i, j: (i, 0))]
    args = [a]
    scratch = []
    nh = 0
    if merge is not None:
        hf, hb, o, hg, nh = merge
        hw = hf.shape[1]
        in_specs += [pl.BlockSpec((tm, hw), lambda i, j: (i + off, 0)),
                     pl.BlockSpec((tm, hw), lambda i, j: (i + off, 0)),
                     pl.BlockSpec((tm, hw), lambda i, j: (i, 0)),
                     pl.BlockSpec((1, hw), lambda i, j: (0, 0))]
        args += [hf, hb, o, hg]
        scratch = [pltpu.VMEM((tm, kdim), BF16)]
    in_specs += [pl.BlockSpec((kdim, tn), lambda i, j: (0, j)),
                 pl.BlockSpec((tm, tn), lambda i, j: (i, j)),
                 pl.BlockSpec((1, tn), lambda i, j: (0, j))]
    args += [w, x, gate]
    return pl.pallas_call(
        functools.partial(_out_kernel, merge is not None, nh),
        grid=(m // tm, n // tn),
        in_specs=in_specs,
        out_specs=pl.BlockSpec((tm, tn), lambda i, j: (i, j)),
        out_shape=jax.ShapeDtypeStruct((m, n), F32),
        scratch_shapes=scratch,
        compiler_params=_cparams(("parallel", "arbitrary")),
        name="out_proj_residual",
    )(*args)


def _ffn_kernel(x_ref, sh_ref, sc_ref, g_ref, wg_ref, wu_ref, wd_ref, y_ref, h_scr, acc_scr):
    f = pl.program_id(1)

    @pl.when(f == 0)
    def _():
        h = _rms(x_ref[...]) * (1.0 + sc_ref[...]) + sh_ref[...]
        h_scr[...] = h.astype(BF16)
        acc_scr[...] = jnp.zeros(acc_scr.shape, F32)

    h = h_scr[...]
    gt = _dot(h, wg_ref[...])
    up = _dot(h, wu_ref[...])
    act = (gt * jax.nn.sigmoid(gt)) * up
    acc_scr[...] += _dot(act.astype(BF16), wd_ref[...])

    @pl.when(f == pl.num_programs(1) - 1)
    def _():
        y_ref[...] = x_ref[...] + g_ref[...] * acc_scr[...]


def _ffn_dense(x, shift, scale, gate, wg, wu, wd, tm=512, tf=512):
    m, d = x.shape
    ff = wg.shape[1]
    tm = min(tm, m)
    assert m % tm == 0 and ff % tf == 0
    vec = pl.BlockSpec((1, d), lambda i, f: (0, 0))
    return pl.pallas_call(
        _ffn_kernel,
        grid=(m // tm, ff // tf),
        in_specs=[
            pl.BlockSpec((tm, d), lambda i, f: (i, 0)), vec, vec, vec,
            pl.BlockSpec((d, tf), lambda i, f: (0, f)),
            pl.BlockSpec((d, tf), lambda i, f: (0, f)),
            pl.BlockSpec((tf, d), lambda i, f: (f, 0)),
        ],
        out_specs=pl.BlockSpec((tm, d), lambda i, f: (i, 0)),
        out_shape=jax.ShapeDtypeStruct((m, d), F32),
        scratch_shapes=[pltpu.VMEM((tm, d), BF16), pltpu.VMEM((tm, d), F32)],
        compiler_params=_cparams(("parallel", "arbitrary")),
        name="ffn_dense",
    )(x, shift, scale, gate, wg, wu, wd)


R_E1, R_E2, R_W1, R_W2, R_RANK1, R_RANK2 = range(6)


def _lane_pick(a, lane, k):
    return jnp.sum(jnp.where(lane == k, a, 0.0), axis=1, keepdims=True)


def _router_kernel(n_exp, x_ref, sh_ref, sc_ref, r_ref, info_ref, cnt_ref, carry_scr):
    @pl.when(pl.program_id(0) == 0)
    def _():
        carry_scr[...] = jnp.zeros(carry_scr.shape, F32)

    tm = x_ref.shape[0]
    h = _rms(x_ref[...]) * (1.0 + sc_ref[...]) + sh_ref[...]
    logits = jnp.dot(h, r_ref[...], preferred_element_type=F32, precision=lax.Precision.HIGHEST)
    lane = lax.broadcasted_iota(jnp.int32, logits.shape, 1)
    logits = jnp.where(lane < n_exp, logits, NEG_INF)
    v1 = jnp.max(logits, axis=1, keepdims=True)
    i1 = jnp.min(jnp.where(logits == v1, lane, LANES), axis=1, keepdims=True)
    rest = jnp.where(lane == i1, NEG_INF, logits)
    v2 = jnp.max(rest, axis=1, keepdims=True)
    i2 = jnp.min(jnp.where(rest == v2, lane, LANES), axis=1, keepdims=True)
    e2 = jnp.exp(v2 - v1)
    w1 = 1.0 / (1.0 + e2)
    w2 = e2 / (1.0 + e2)

    chosen = jnp.where(jnp.logical_or(lane == i1, lane == i2), 1.0, 0.0)
    r_i = lax.broadcasted_iota(jnp.int32, (tm, tm), 0)
    c_i = lax.broadcasted_iota(jnp.int32, (tm, tm), 1)
    before = (c_i < r_i).astype(BF16)
    rank = _dot(before, chosen.astype(BF16)) + carry_scr[...]
    carry_scr[...] += jnp.sum(chosen, axis=0, keepdims=True)
    cnt_ref[...] = carry_scr[...]
    rank1 = jnp.sum(jnp.where(lane == i1, rank, 0.0), axis=1, keepdims=True)
    rank2 = jnp.sum(jnp.where(lane == i2, rank, 0.0), axis=1, keepdims=True)
    info = jnp.zeros(logits.shape, F32)
    for k, val in ((R_E1, i1.astype(F32)), (R_E2, i2.astype(F32)), (R_W1, w1), (R_W2, w2),
                   (R_RANK1, rank1), (R_RANK2, rank2)):
        info = jnp.where(lane == k, val, info)
    info_ref[...] = info


def _router(x, shift, scale, router, tm=512):
    m, d = x.shape
    n_exp = router.shape[1]
    vec = pl.BlockSpec((1, d), lambda i: (0, 0))
    router_pad = jnp.zeros((d, LANES), F32).at[:, :n_exp].set(router)
    return pl.pallas_call(
        functools.partial(_router_kernel, n_exp),
        grid=(m // tm,),
        in_specs=[pl.BlockSpec((tm, d), lambda i: (i, 0)), vec, vec, pl.BlockSpec((d, LANES), lambda i: (0, 0))],
        out_specs=[pl.BlockSpec((tm, LANES), lambda i: (i, 0)), pl.BlockSpec((1, LANES), lambda i: (0, 0))],
        out_shape=[jax.ShapeDtypeStruct((m, LANES), F32), jax.ShapeDtypeStruct((1, LANES), F32)],
        scratch_shapes=[pltpu.VMEM((1, LANES), F32)],
        compiler_params=_cparams(("arbitrary",)),
        name="moe_router",
    )(x, shift, scale, router_pad)


def _gather_rows(src_hbm, idx_ref, buf, sem, start):
    n_rows = buf.shape[0]
    if not start:
        pltpu.make_async_copy(src_hbm.at[pl.ds(0, n_rows), :], buf, sem).wait()
        return

    def body(r, carry):
        pltpu.make_async_copy(src_hbm.at[pl.ds(idx_ref[0, r], 1), :], buf.at[pl.ds(r, 1), :], sem).start()
        return carry
    lax.fori_loop(0, n_rows, body, 0, unroll=GATHER_UNROLL)


def _moe_ffn_kernel(te_ref, nu_ref, src_ref, nsrc_ref, x_hbm, sh_ref, sc_ref, wg_ref, wu_ref, wd_ref, y_ref,
                    xbuf, hbuf, sem):
    i = pl.program_id(0)
    f = pl.program_id(1)
    n_used = nu_ref[0]
    active = i < n_used

    @pl.when(f == 0)
    def _():
        y_ref[...] = jnp.zeros(y_ref.shape, F32)

    @pl.when(jnp.logical_and(active, f == 0))
    def _():
        pl.when(i == 0)(lambda: _gather_rows(x_hbm, src_ref, xbuf, sem, True))
        _gather_rows(x_hbm, src_ref, xbuf, sem, False)
        h = _rms(xbuf[...]) * (1.0 + sc_ref[...]) + sh_ref[...]
        hbuf[...] = h.astype(BF16)
        pl.when(i + 1 < n_used)(lambda: _gather_rows(x_hbm, nsrc_ref, xbuf, sem, True))

    @pl.when(active)
    def _():
        h = hbuf[...]
        gt = _dot(h, wg_ref[...])
        up = _dot(h, wu_ref[...])
        act = (gt * jax.nn.sigmoid(gt)) * up
        y_ref[...] += _dot(act.astype(BF16), wd_ref[...])


def _moe_ffn(x, shift, scale, tile_exp, n_used, src, wg, wu, wd, tm, tf=1024):
    m, d = x.shape
    n_exp, _, ff = wg.shape
    n_tiles = src.shape[0]
    tf = min(tf, ff)
    assert ff % tf == 0
    nf = ff // tf
    vec = pl.BlockSpec((1, d), lambda i, f, te, nu: (0, 0))
    idx = pl.BlockSpec((None, 1, tm), lambda i, f, te, nu: (i, 0, 0), memory_space=pltpu.SMEM)
    idx_next = pl.BlockSpec((None, 1, tm), lambda i, f, te, nu: (jnp.minimum(i + 1, n_tiles - 1), 0, 0),
                            memory_space=pltpu.SMEM)

    def f_eff(i, f, nu):
        return jnp.where(i < nu[0], f, nf - 1)

    grid_spec = pltpu.PrefetchScalarGridSpec(
        num_scalar_prefetch=2,
        grid=(n_tiles, nf),
        in_specs=[
            idx, idx_next,
            pl.BlockSpec(memory_space=pl.ANY),
            vec, vec,
            pl.BlockSpec((None, d, tf), lambda i, f, te, nu: (te[i], 0, f_eff(i, f, nu))),
            pl.BlockSpec((None, d, tf), lambda i, f, te, nu: (te[i], 0, f_eff(i, f, nu))),
            pl.BlockSpec((None, tf, d), lambda i, f, te, nu: (te[i], f_eff(i, f, nu), 0)),
        ],
        out_specs=pl.BlockSpec((tm, d), lambda i, f, te, nu: (i, 0)),
        scratch_shapes=[pltpu.VMEM((tm, d), F32), pltpu.VMEM((tm, d), BF16), pltpu.SemaphoreType.DMA(())],
    )
    return pl.pallas_call(
        _moe_ffn_kernel,
        grid_spec=grid_spec,
        out_shape=jax.ShapeDtypeStruct((n_tiles * tm, d), F32),
        compiler_params=_cparams(("arbitrary", "arbitrary")),
        name="moe_grouped_ffn",
    )(tile_exp, n_used, src, src, x, shift, scale, wg, wu, wd)


def _moe_combine_kernel(pos_ref, npos_ref, x_ref, g_ref, info_ref, y_hbm, o_ref, ybuf, sem):
    i = pl.program_id(0)
    tm = x_ref.shape[0]
    slot = i % 2
    pl.when(i == 0)(lambda: _gather_rows(y_hbm, pos_ref, ybuf.at[0], sem.at[0], True))
    for s in range(2):
        @pl.when(slot == s)
        def _(s=s):
            _gather_rows(y_hbm, pos_ref, ybuf.at[s], sem.at[s], False)
            pl.when(i + 1 < pl.num_programs(0))(
                lambda: _gather_rows(y_hbm, npos_ref, ybuf.at[1 - s], sem.at[1 - s], True))
            info = info_ref[...]
            lane = lax.broadcasted_iota(jnp.int32, info.shape, 1)
            w1 = _lane_pick(info, lane, R_W1)
            w2 = _lane_pick(info, lane, R_W2)
            o_ref[...] = x_ref[...] + g_ref[...] * (w1 * ybuf[s, :tm, :] + w2 * ybuf[s, tm:, :])


def _moe_combine(x, gate, info, y, pos, tm=256):
    m, d = x.shape
    nb = m // tm
    assert m % tm == 0 and pos.shape == (nb, 1, 2 * tm)
    idx = pl.BlockSpec((None, 1, 2 * tm), lambda i: (i, 0, 0), memory_space=pltpu.SMEM)
    idx_next = pl.BlockSpec((None, 1, 2 * tm), lambda i: (jnp.minimum(i + 1, nb - 1), 0, 0), memory_space=pltpu.SMEM)
    return pl.pallas_call(
        _moe_combine_kernel,
        grid=(nb,),
        in_specs=[
            idx, idx_next,
            pl.BlockSpec((tm, d), lambda i: (i, 0)),
            pl.BlockSpec((1, d), lambda i: (0, 0)),
            pl.BlockSpec((tm, LANES), lambda i: (i, 0)),
            pl.BlockSpec(memory_space=pl.ANY),
        ],
        out_specs=pl.BlockSpec((tm, d), lambda i: (i, 0)),
        out_shape=jax.ShapeDtypeStruct((m, d), F32),
        scratch_shapes=[pltpu.VMEM((2, 2 * tm, d), F32), pltpu.SemaphoreType.DMA((2,))],
        compiler_params=_cparams(("arbitrary",)),
        name="moe_combine",
    )(pos, pos, x, gate, info, y)


def _moe_top2(x, shift, scale, gate, router, wg, wu, wd, tm=512):
    m, d = x.shape
    n_exp = router.shape[1]
    info, counts = _router(x, shift, scale, router)
    e1, e2 = info[:, R_E1].astype(jnp.int32), info[:, R_E2].astype(jnp.int32)
    r1, r2 = info[:, R_RANK1].astype(jnp.int32), info[:, R_RANK2].astype(jnp.int32)
    cnt = counts[0, :n_exp].astype(jnp.int32)
    padded = (cnt + tm - 1) // tm * tm
    ends = jnp.cumsum(padded)
    offs = ends - padded
    pos1, pos2 = offs[e1] + r1, offs[e2] + r2
    n_tiles = TOP_K * m // tm + n_exp
    n_rows = n_tiles * tm
    tok = jnp.arange(m, dtype=jnp.int32)
    src = jnp.zeros((n_rows,), jnp.int32).at[jnp.concatenate([pos1, pos2])].set(
        jnp.concatenate([tok, tok]), unique_indices=True)
    n_used = ends[-1] // tm
    tile_start = jnp.arange(n_tiles, dtype=jnp.int32) * tm
    tile_exp = jnp.searchsorted(ends, jnp.minimum(tile_start, ends[-1] - 1), side="right").astype(jnp.int32)
    tile_exp = jnp.minimum(tile_exp, n_exp - 1)
    y = _moe_ffn(x, shift, scale, tile_exp, n_used.reshape(1).astype(jnp.int32), src.reshape(n_tiles, 1, tm),
                 wg, wu, wd, tm)
    tc = min(COMBINE_ROWS, m)
    pos = jnp.concatenate([pos1.reshape(m // tc, 1, tc), pos2.reshape(m // tc, 1, tc)], axis=2)
    return _moe_combine(x, gate, info, y, pos, tc)


def _rope_tables(rows):
    n_freq = HEAD // 4
    inv = ROPE_THETA ** (-jnp.arange(n_freq, dtype=F32) / n_freq)
    ang_r = jnp.arange(rows, dtype=F32)[:, None] * inv
    ang_c = jnp.arange(GRID_W, dtype=F32)[:, None] * inv

    def full(fn):
        t = jnp.concatenate([jnp.broadcast_to(fn(ang_r)[:, None, :], (rows, GRID_W, n_freq)),
                             jnp.broadcast_to(fn(ang_c)[None, :, :], (rows, GRID_W, n_freq))], axis=-1)
        return t.reshape(rows * GRID_W, 2 * n_freq)

    cos, sin = full(jnp.cos), full(jnp.sin)
    return jnp.concatenate([cos, cos], axis=-1), jnp.concatenate([-sin, sin], axis=-1)


def _halves(a):
    shp = a.shape
    a = a.reshape(shp[:-1] + (shp[-1] // HEAD, HEAD // 2, 2))
    return jnp.swapaxes(a, -1, -2).reshape(shp)


def kernel(x, c, ctx, c_ctx, ada_w, ada_b, ev_w_in, ev_gate_b, ev_q_gain, ev_k_gain, ev_h_gain, ev_w_out,
           ev_ff_gate, ev_ff_up, ev_ff_down, od_w_in, od_q_gain, od_k_gain, od_lam_q1, od_lam_k1, od_lam_q2,
           od_lam_k2, od_sub_gain, od_w_out, od_router, od_ex_gate, od_ex_up, od_ex_down):
    bsz, s_len, d = x.shape
    n_ctx = ctx.shape[1]
    assert bsz == 1 and ada_w.shape[0] == 2
    a_heads = d // (2 * HEAD)
    a_kv = a_heads // GQA_GROUP
    b_heads = d // (2 * V_HEAD)
    c_heads = d // V_HEAD
    a_q, a_kvw, b_qk, b_v = a_heads * HEAD, a_kv * HEAD, b_heads * HEAD, b_heads * V_HEAD
    score_scale = HEAD ** -0.5 * math.log2(math.e)

    x_lat, x_ctx = x[0], ctx[0]
    cvecs = jnp.zeros((8, d), F32).at[0].set(c[0]).at[1].set(c_ctx)
    mods = _modulation(cvecs, ada_w, ada_b)

    def mod(layer, who):
        return [mods[layer, who, k * d:(k + 1) * d][None, :] for k in range(6)]

    cos_l, sin_l = _rope_tables(s_len // GRID_W)
    cos_c, sin_c = jnp.ones((n_ctx, HEAD), F32), jnp.zeros((n_ctx, HEAD), F32)

    w_in = ev_w_in[0]
    o0 = 0
    wqa = w_in[:, o0:o0 + a_q]; o0 += a_q
    wka = w_in[:, o0:o0 + a_kvw]; o0 += a_kvw
    wva = w_in[:, o0:o0 + a_kvw]; o0 += a_kvw
    wqb = w_in[:, o0:o0 + b_qk]; o0 += b_qk
    wkb = w_in[:, o0:o0 + b_qk]; o0 += b_qk
    wvb = w_in[:, o0:o0 + b_v]; o0 += b_v
    wob = w_in[:, o0:o0 + b_v]; o0 += b_v
    wgt = w_in[:, o0:]
    n_gates = wgt.shape[1]
    w0 = jnp.concatenate([_halves(wqa), wvb, wqb, wkb, _halves(wka), wva, wob], axis=1).astype(BF16)
    ones = lambda n: jnp.ones((n,), F32)
    cs0 = jnp.concatenate([
        jnp.tile(_halves(ev_q_gain[0]), a_heads) * score_scale, ones(b_v), ones(b_qk) * (HEAD ** -0.5), ones(b_qk),
        jnp.tile(_halves(ev_k_gain[0]), a_kv), ones(a_kvw), ones(b_v)])[None, :]
    modes0 = (["R"] * a_heads + ["P"] * (b_v // HEAD) + ["P"] * (2 * b_qk // HEAD) + ["R"] * a_kv
              + ["P"] * a_kv + ["P"] * (b_v // HEAD))
    nbf0 = a_q + b_v + 2 * b_qk + 2 * a_kvw
    wg0 = jnp.zeros((d, LANES), F32).at[:, :n_gates].set(wgt).astype(BF16)
    gb0 = jnp.zeros((1, LANES), F32).at[0, :n_gates].set(ev_gate_b[0])
    col_vb, col_qb, col_kb = a_q // b_v, (a_q + b_v) // b_qk, (a_q + b_v + b_qk) // b_qk
    col_ka = (a_q + b_v + 2 * b_qk) // HEAD
    col_va = col_ka + a_kv

    ml, mc = mod(0, 0), mod(0, 1)
    pl_b, pl_o, pl_g = _proj(x_lat, ml[0], ml[1], w0, cs0, cos_l, sin_l, modes0, nbf0, wg0, gb0)
    pc_b, pc_o, pc_g = _proj(x_ctx, mc[0], mc[1], w0, cs0, cos_c, sin_c, modes0, nbf0, wg0, gb0)
    p_all = jnp.concatenate([pl_b, pc_b], axis=0)
    g_all = jnp.concatenate([pl_g, pc_g], axis=0)
    g_row = g_all[:, :4 * b_heads].T

    att_l = _gqa_attention(pl_b, p_all, a_kv, col_ka, col_va, tq=512, tk=3328)
    att_c = _gqa_attention(pc_b, pc_b, a_kv, col_ka, col_va, tq=256, tk=256)
    hf, hb = _mlstm(p_all, g_all, g_row, b_heads, col_qb, col_kb, col_vb, s_len, n_ctx)

    hg = ev_h_gain[0].reshape(1, b_v)
    w_out0 = ev_w_out[0].astype(BF16)
    x_lat = _out_proj(x_lat, ml[2], w_out0, att_l, merge=(hf, hb, pl_o, hg, b_heads))
    x_ctx = _out_proj(x_ctx, mc[2], w_out0, att_c, merge=(hf, hb, pc_o, hg, b_heads), row_off=s_len)
    ffw = (ev_ff_gate[0].astype(BF16), ev_ff_up[0].astype(BF16), ev_ff_down[0].astype(BF16))
    x_lat = _ffn_dense(x_lat, ml[3], ml[4], ml[5], *ffw)
    x_ctx = _ffn_dense(x_ctx, mc[3], mc[4], mc[5], *ffw)

    c_qk = c_heads * 2 * HEAD
    w1 = od_w_in[0]
    w1 = jnp.concatenate([_halves(w1[:, :2 * c_qk]), w1[:, 2 * c_qk:]], axis=1).astype(BF16)
    cs1 = jnp.concatenate([jnp.tile(_halves(od_q_gain[0]), 2 * c_heads) * score_scale,
                           jnp.tile(_halves(od_k_gain[0]), 2 * c_heads), ones(c_heads * V_HEAD)])[None, :]
    modes1 = ["R"] * (4 * c_heads) + ["P"] * (c_heads * V_HEAD // HEAD)
    ml, mc = mod(1, 0), mod(1, 1)
    (ql,) = _proj(x_lat, ml[0], ml[1], w1, cs1, cos_l, sin_l, modes1, w1.shape[1])
    (qc,) = _proj(x_ctx, mc[0], mc[1], w1, cs1, cos_c, sin_c, modes1, w1.shape[1])
    kv_all = jnp.concatenate([ql, qc], axis=0)
    lam_init = 0.8 - 0.6 * math.exp(-0.3 * 1)
    vec = lambda a: a[0].reshape(1, -1)
    o_l = _diff_attention(ql, kv_all, c_heads, lam_init, vec(od_lam_q1), vec(od_lam_k1), vec(od_lam_q2),
                          vec(od_lam_k2), vec(od_sub_gain), tq=2048, tk=3328)
    x_lat = _out_proj(x_lat, ml[2], od_w_out[0].astype(BF16), o_l)
    x_lat = _moe_top2(x_lat, ml[3], ml[4], ml[5], od_router[0], od_ex_gate[0].astype(BF16),
                      od_ex_up[0].astype(BF16), od_ex_down[0].astype(BF16))
    return x_lat[None]
```
